```python
import math
import jax, jax.numpy as jnp
from jax import lax
import numpy as np


D_MODEL = 1024
BATCH = 4
SEQ = 4096
DEPTH = 2

RMS_EPS = 1e-6
D_FF = 2816
MACARON_WEIGHT = 0.5
N_EVEN = (DEPTH + 1) // 2
N_ODD = DEPTH // 2

POOL_WIDTH = D_MODEL // 2
POOL_GROUPS = 4
POOL_GROUP_DIM = POOL_WIDTH // POOL_GROUPS
POOL_WINDOWS = (2, 4, 8, 16)

HYENA_WIDTH = D_MODEL // 2
HYENA_ORDER = 2
HYENA_SHORT_CONV = 3
HYENA_POS_DIM = 33
HYENA_FILTER_HIDDEN = 64
HYENA_FAST_DECAY_PCT = 0.3
HYENA_SLOW_DECAY_PCT = 1.5
HYENA_DECAY_TARGET = 1e-2

EVEN_IN_WIDTH = POOL_WIDTH + (HYENA_ORDER + 1) * HYENA_WIDTH
EVEN_OUT_WIDTH = POOL_WIDTH + HYENA_WIDTH

MLA_HEADS = 16
MLA_Q_LORA = 256
MLA_KV_LORA = 128
MLA_NOPE = 64
MLA_ROPE = 32
MLA_V = 64
ROPE_THETA = 10000.0
Q_BLOCK = 128

kernel_name = "hybrid_pool_hyena_mla_macaron_encoder"


def rmsnorm(x, g):
    x32 = x.astype(jnp.float32)
    y = x32 * lax.rsqrt(jnp.mean(x32 * x32, axis=-1, keepdims=True) + RMS_EPS)
    return (y * g.astype(jnp.float32)).astype(x.dtype)


def swiglu(h, w_gate, w_up, w_down):
    return (jax.nn.silu(h @ w_gate) * (h @ w_up)) @ w_down


def centred_pool_minus_self(a):
    L = a.shape[1]
    a32 = a.astype(jnp.float32)
    csum = jnp.pad(jnp.cumsum(a32, axis=1), ((0, 0), (1, 0), (0, 0)))
    t = jnp.arange(L)
    outs = []
    for g, w in enumerate(POOL_WINDOWS):
        cg = csum[..., g * POOL_GROUP_DIM:(g + 1) * POOL_GROUP_DIM]
        lo = jnp.clip(t - w // 2, 0, L)
        hi = jnp.clip(t - w // 2 + w, 0, L)
        cnt = (hi - lo).astype(jnp.float32)[None, :, None]
        outs.append((jnp.take(cg, hi, axis=1) - jnp.take(cg, lo, axis=1)) / cnt)
    return (jnp.concatenate(outs, axis=-1) - a32).astype(a.dtype)


def pool_mixer(a, pool_w, pool_scale):
    B, L, _ = a.shape
    p = centred_pool_minus_self(a).reshape(B, L, POOL_GROUPS, POOL_GROUP_DIM)
    y = jnp.einsum('blgc,gcd->blgd', p, pool_w).reshape(B, L, POOL_WIDTH)
    return y * pool_scale


def short_conv_centred(u, w, b):
    L = u.shape[1]
    pad = HYENA_SHORT_CONV // 2
    up = jnp.pad(u, ((0, 0), (pad, HYENA_SHORT_CONV - 1 - pad), (0, 0)))
    y = b
    for k in range(HYENA_SHORT_CONV):
        y = y + up[:, k:k + L] * w[k]
    return y


def hyena_position_features(L):
    f32 = jnp.float32
    t = jnp.linspace(0.0, 1.0, L, dtype=f32)
    bands = (HYENA_POS_DIM - 1) // 2
    w = 2.0 * math.pi * jnp.arange(L, dtype=f32) / L
    f = jnp.linspace(1e-4, bands - 1, bands, dtype=f32)
    phase = w[:, None] * f[None, :]
    z = jnp.concatenate([t[:, None], jnp.cos(phase), -jnp.sin(phase)], axis=-1)
    return t, z


def hyena_filter_spectra(t, z, w1, b1, w2, b2, w3, b3, sin_freq, w_out, decay):
    f32 = jnp.float32
    L = t.shape[0]
    h = jnp.sin(sin_freq[0].astype(f32) * (z @ w1.astype(f32) + b1.astype(f32)))
    h = jnp.sin(sin_freq[1].astype(f32) * (h @ w2.astype(f32) + b2.astype(f32)))
    h = jnp.sin(sin_freq[2].astype(f32) * (h @ w3.astype(f32) + b3.astype(f32)))
    h = (h @ w_out.astype(f32)).reshape(L, HYENA_ORDER, 2, HYENA_WIDTH)
    h = h * jnp.exp(-t[:, None, None, None] * jnp.abs(decay.astype(f32)))
    fwd, bwd = h[:, :, 0], h[:, :, 1]
    two = jnp.concatenate(
        [fwd, jnp.zeros((1, HYENA_ORDER, HYENA_WIDTH), f32), bwd[:0:-1]], axis=0)
    two = two * lax.rsqrt(jnp.sum(two * two, axis=0, keepdims=True))
    return jnp.fft.rfft(two, axis=0)


def fft_long_conv(u, spec, bias):
    L = u.shape[1]
    u32 = u.astype(jnp.float32)
    y = jnp.fft.irfft(jnp.fft.rfft(u32, n=2 * L, axis=1) * spec[None], n=2 * L, axis=1)[:, :L]
    return (y + u32 * bias.astype(jnp.float32)).astype(u.dtype)


def hyena_mixer(u, conv_w, conv_b, spec, bias):
    u = short_conv_centred(u, conv_w, conv_b)
    parts = jnp.split(u, HYENA_ORDER + 1, axis=-1)
    z = parts[-1]
    for o in range(HYENA_ORDER):
        z = parts[o] * fft_long_conv(z, spec[:, o], bias[o])
    return z


def apply_rope(x, cos, sin):
    half = x.shape[-1] // 2
    x1, x2 = x[..., :half], x[..., half:]
    return jnp.concatenate([x1 * cos - x2 * sin, x1 * sin + x2 * cos], axis=-1)


def mla_mixer(h, w_dq, q_norm_g, w_uq, w_dkv, kv_norm_g, w_ukv, w_o, cos, sin):
    B, L, _ = h.shape
    cq = rmsnorm(h @ w_dq, q_norm_g)
    q = (cq @ w_uq).reshape(B, L, MLA_HEADS, MLA_NOPE + MLA_ROPE)
    q_nope, q_rope = q[..., :MLA_NOPE], q[..., MLA_NOPE:]
    q_rope = apply_rope(q_rope, cos[None, :, None, :], sin[None, :, None, :])
    ckv_full = h @ w_dkv
    ckv = rmsnorm(ckv_full[..., :MLA_KV_LORA], kv_norm_g)
    k_rope = apply_rope(ckv_full[..., MLA_KV_LORA:], cos[None], sin[None])
    kv = (ckv @ w_ukv).reshape(B, L, MLA_HEADS, MLA_NOPE + MLA_V)
    k_nope, v = kv[..., :MLA_NOPE], kv[..., MLA_NOPE:]
    scale = (MLA_NOPE + MLA_ROPE) ** -0.5
    nb = L // Q_BLOCK

    def to_blocks(a):
        return jnp.moveaxis(a.reshape(B, nb, Q_BLOCK, *a.shape[2:]), 1, 0)

    def attend(qs):
        qn, qr = qs
        s = (jnp.einsum('bqhd,bkhd->bhqk', qn, k_nope)
             + jnp.einsum('bqhr,bkr->bhqk', qr, k_rope))
        p = jax.nn.softmax(s.astype(jnp.float32) * scale, axis=-1).astype(v.dtype)
        return jnp.einsum('bhqk,bkhd->bqhd', p, v)

    o = lax.map(attend, (to_blocks(q_nope), to_blocks(q_rope)))
    o = jnp.moveaxis(o, 0, 1).reshape(B, L, MLA_HEADS * MLA_V)
    return o @ w_o


def setup_inputs(seed: int = 0) -> dict:
    key = jax.random.key(seed)
    ks = iter(jax.random.split(key, 32))

    def nrm(shape, scale):
        return jax.random.normal(next(ks), shape, jnp.float32) * scale

    D, F, W, FH = D_MODEL, D_FF, HYENA_WIDTH, HYENA_FILTER_HIDDEN
    decay_base = jnp.linspace(-math.log(HYENA_DECAY_TARGET) / HYENA_SLOW_DECAY_PCT,
                              -math.log(HYENA_DECAY_TARGET) / HYENA_FAST_DECAY_PCT, W,
                              dtype=jnp.float32)
    inp = {}
    inp["x"] = nrm((BATCH, SEQ, D), 1.0)
    inp["norm_g"] = 1.0 + nrm((DEPTH, 3, D), 0.05)
    inp["ffn_w_gate"] = nrm((DEPTH, 2, D, F), D ** -0.5)
    inp["ffn_w_up"] = nrm((DEPTH, 2, D, F), D ** -0.5)
    inp["ffn_w_down"] = nrm((DEPTH, 2, F, D), F ** -0.5)
    inp["mix_w_in"] = nrm((N_EVEN, D, EVEN_IN_WIDTH), D ** -0.5)
    inp["pool_w"] = nrm((N_EVEN, POOL_GROUPS, POOL_GROUP_DIM, POOL_GROUP_DIM), POOL_GROUP_DIM ** -0.5)
    inp["pool_scale"] = 1.0 + nrm((N_EVEN, POOL_WIDTH), 0.05)
    inp["hyena_conv_w"] = nrm((N_EVEN, HYENA_SHORT_CONV, (HYENA_ORDER + 1) * W), HYENA_SHORT_CONV ** -0.5)
    inp["hyena_conv_b"] = nrm((N_EVEN, (HYENA_ORDER + 1) * W), 0.02)
    inp["hyena_ffn_w1"] = nrm((N_EVEN, HYENA_POS_DIM, FH), HYENA_POS_DIM ** -0.5)
    inp["hyena_ffn_b1"] = nrm((N_EVEN, FH), 0.02)
    inp["hyena_ffn_w2"] = nrm((N_EVEN, FH, FH), FH ** -0.5)
    inp["hyena_ffn_b2"] = nrm((N_EVEN, FH), 0.02)
    inp["hyena_ffn_w3"] = nrm((N_EVEN, FH, FH), FH ** -0.5)
    inp["hyena_ffn_b3"] = nrm((N_EVEN, FH), 0.02)
    inp["hyena_sin_freq"] = 1.0 + nrm((N_EVEN, 3, FH), 0.1)
    inp["hyena_ffn_w_out"] = nrm((N_EVEN, FH, HYENA_ORDER * 2 * W), FH ** -0.5)
    inp["hyena_decay"] = decay_base * jnp.exp(nrm((N_EVEN, HYENA_ORDER, 2, W), 0.1))
    inp["hyena_bias"] = nrm((N_EVEN, HYENA_ORDER, W), 0.5)
    inp["mix_w_out"] = nrm((N_EVEN, EVEN_OUT_WIDTH, D), EVEN_OUT_WIDTH ** -0.5)
    inp["mla_w_dq"] = nrm((N_ODD, D, MLA_Q_LORA), D ** -0.5)
    inp["mla_q_norm_g"] = 1.0 + nrm((N_ODD, MLA_Q_LORA), 0.05)
    inp["mla_w_uq"] = nrm((N_ODD, MLA_Q_LORA, MLA_HEADS * (MLA_NOPE + MLA_ROPE)), MLA_Q_LORA ** -0.5)
    inp["mla_w_dkv"] = nrm((N_ODD, D, MLA_KV_LORA + MLA_ROPE), D ** -0.5)
    inp["mla_kv_norm_g"] = 1.0 + nrm((N_ODD, MLA_KV_LORA), 0.05)
    inp["mla_w_ukv"] = nrm((N_ODD, MLA_KV_LORA, MLA_HEADS * (MLA_NOPE + MLA_V)), MLA_KV_LORA ** -0.5)
    inp["mla_w_o"] = nrm((N_ODD, MLA_HEADS * MLA_V, D), (MLA_HEADS * MLA_V) ** -0.5)
    inp["final_norm_g"] = 1.0 + nrm((D,), 0.05)
    return inp


def reference(x, norm_g, ffn_w_gate, ffn_w_up, ffn_w_down, mix_w_in, pool_w, pool_scale,
              hyena_conv_w, hyena_conv_b, hyena_ffn_w1, hyena_ffn_b1, hyena_ffn_w2, hyena_ffn_b2,
              hyena_ffn_w3, hyena_ffn_b3, hyena_sin_freq, hyena_ffn_w_out, hyena_decay, hyena_bias,
              mix_w_out, mla_w_dq, mla_q_norm_g, mla_w_uq, mla_w_dkv, mla_kv_norm_g, mla_w_ukv,
              mla_w_o, final_norm_g):
    L = x.shape[1]
    t_pos, z_pos = hyena_position_features(L)
    inv_freq = ROPE_THETA ** (-jnp.arange(0, MLA_ROPE, 2, dtype=jnp.float32) / MLA_ROPE)
    ang = jnp.arange(L, dtype=jnp.float32)[:, None] * inv_freq[None, :]
    cos = jnp.cos(ang).astype(x.dtype)
    sin = jnp.sin(ang).astype(x.dtype)

    for i in range(DEPTH):
        x = x + MACARON_WEIGHT * swiglu(rmsnorm(x, norm_g[i, 0]),
                                        ffn_w_gate[i, 0], ffn_w_up[i, 0], ffn_w_down[i, 0])
        h = rmsnorm(x, norm_g[i, 1])
        j = i // 2
        if i % 2 == 0:
            proj = h @ mix_w_in[j]
            y_pool = pool_mixer(proj[..., :POOL_WIDTH], pool_w[j], pool_scale[j])
            spec = hyena_filter_spectra(t_pos, z_pos, hyena_ffn_w1[j], hyena_ffn_b1[j],
                                        hyena_ffn_w2[j], hyena_ffn_b2[j], hyena_ffn_w3[j],
                                        hyena_ffn_b3[j], hyena_sin_freq[j], hyena_ffn_w_out[j],
                                        hyena_decay[j])
            y_hyena = hyena_mixer(proj[..., POOL_WIDTH:], hyena_conv_w[j], hyena_conv_b[j],
                                  spec, hyena_bias[j])
            x = x + jnp.concatenate([y_pool, y_hyena], axis=-1) @ mix_w_out[j]
        else:
            x = x + mla_mixer(h, mla_w_dq[j], mla_q_norm_g[j], mla_w_uq[j], mla_w_dkv[j],
                              mla_kv_norm_g[j], mla_w_ukv[j], mla_w_o[j], cos, sin)
        x = x + MACARON_WEIGHT * swiglu(rmsnorm(x, norm_g[i, 2]),
                                        ffn_w_gate[i, 1], ffn_w_up[i, 1], ffn_w_down[i, 1])
    return rmsnorm(x, final_norm_g)
```

```python
import functools
import math

import numpy as np
import jax
import jax.numpy as jnp
from jax import lax
from jax.experimental import pallas as pl
from jax.experimental.pallas import tpu as pltpu

F32 = jnp.float32
BF16 = jnp.bfloat16

RMS_EPS = 1e-6
MACARON_WEIGHT = 0.5
POOL_WINDOWS = (2, 4, 8, 16)
HYENA_ORDER = 2
HYENA_SHORT_CONV = 3
MLA_HEADS = 16
MLA_Q_LORA = 256
MLA_KV_LORA = 128
MLA_NOPE = 64
MLA_ROPE = 32
MLA_V = 64
ROPE_THETA = 10000.0

LANES = 128
SUBLANES = 8
VMEM_LIMIT = 56 * 2**20

DFT_N1 = 64
DFT_N2 = 128
DFT_K1 = DFT_N1 // 2 + 1
DFT_QP = 40
DFT_PITCH = DFT_N2 + SUBLANES
HALO = 16


def _cparams(semantics):
    return pltpu.CompilerParams(dimension_semantics=semantics, vmem_limit_bytes=VMEM_LIMIT)


def _rms(x, g):
    return x * lax.rsqrt(jnp.mean(x * x, axis=-1, keepdims=True) + RMS_EPS) * g


def _dot(a, b):
    return jnp.dot(a, b, preferred_element_type=F32)


def _ffn_body(final_norm, x_ref, g_ref, wg_ref, wu_ref, wd_ref, fg_ref, o_ref, h_ref, acc_ref):
    j = pl.program_id(1)

    @pl.when(j == 0)
    def _():
        h_ref[...] = _rms(x_ref[...], g_ref[...]).astype(BF16)
        acc_ref[...] = jnp.zeros_like(acc_ref)

    h = h_ref[...]
    gate = _dot(h, wg_ref[...])
    up = _dot(h, wu_ref[...])
    act = gate / (1.0 + jnp.exp(-gate)) * up
    acc_ref[...] += _dot(act.astype(BF16), wd_ref[...])

    @pl.when(j == pl.num_programs(1) - 1)
    def _():
        y = x_ref[...] + MACARON_WEIGHT * acc_ref[...]
        if final_norm:
            y = _rms(y, fg_ref[...])
        o_ref[...] = y


def _ffn(x, g, wg, wu, wd, final_g=None, *, tm=1024, tf=256):
    m, d = x.shape
    f = wg.shape[1]
    final_norm = final_g is not None
    fg = final_g if final_norm else g
    return pl.pallas_call(
        functools.partial(_ffn_body, final_norm),
        out_shape=jax.ShapeDtypeStruct((m, d), F32),
        grid=(m // tm, f // tf),
        in_specs=[
            pl.BlockSpec((tm, d), lambda i, j: (i, 0)),
            pl.BlockSpec((1, d), lambda i, j: (0, 0)),
            pl.BlockSpec((d, tf), lambda i, j: (0, j)),
            pl.BlockSpec((d, tf), lambda i, j: (0, j)),
            pl.BlockSpec((tf, d), lambda i, j: (j, 0)),
            pl.BlockSpec((1, d), lambda i, j: (0, 0)),
        ],
        out_specs=pl.BlockSpec((tm, d), lambda i, j: (i, 0)),
        scratch_shapes=[pltpu.VMEM((tm, d), BF16), pltpu.VMEM((tm, d), F32)],
        compiler_params=_cparams(("parallel", "arbitrary")),
        name="ffn",
    )(x, g.reshape(1, d), wg, wu, wd, fg.reshape(1, d))


def _norm_proj_body(x_ref, g_ref, w_ref, o_ref):
    h = _rms(x_ref[...], g_ref[...]).astype(BF16)
    o_ref[...] = _dot(h, w_ref[...])


def _norm_proj(x, g, w, *, tm=512):
    m, d = x.shape
    n = w.shape[1]
    return pl.pallas_call(
        _norm_proj_body,
        out_shape=jax.ShapeDtypeStruct((m, n), F32),
        grid=(m // tm,),
        in_specs=[
            pl.BlockSpec((tm, d), lambda i: (i, 0)),
            pl.BlockSpec((1, d), lambda i: (0, 0)),
            pl.BlockSpec((d, n), lambda i: (0, 0)),
        ],
        out_specs=pl.BlockSpec((tm, n), lambda i: (i, 0)),
        compiler_params=_cparams(("parallel",)),
        name="norm_proj",
    )(x, g.reshape(1, d), w)


def _res_proj_body(n_act, x_ref, *refs):
    act_refs, w_ref, o_ref = refs[:n_act], refs[n_act], refs[n_act + 1]
    y = x_ref[...]
    row = 0
    for a_ref in act_refs:
        k = a_ref.shape[1]
        y = y + _dot(a_ref[...].astype(BF16), w_ref[row:row + k, :])
        row += k
    o_ref[...] = y


def _res_proj(x, acts, w, *, tm=512):
    m, d = x.shape
    return pl.pallas_call(
        functools.partial(_res_proj_body, len(acts)),
        out_shape=jax.ShapeDtypeStruct((m, d), F32),
        grid=(m // tm,),
        in_specs=[pl.BlockSpec((tm, d), lambda i: (i, 0))]
        + [pl.BlockSpec((tm, a.shape[1]), lambda i: (i, 0)) for a in acts]
        + [pl.BlockSpec(w.shape, lambda i: (0, 0))],
        out_specs=pl.BlockSpec((tm, d), lambda i: (i, 0)),
        compiler_params=_cparams(("parallel",)),
        name="res_proj",
    )(x, *acts, w)


def _pool_conv_body(seq_len, cur_ref, prev_ref, next_ref, pw_ref, ps_ref, cw_ref, cb_ref,
                    yp_ref, u_ref, ext_ref):
    i = pl.program_id(1)
    tl = cur_ref.shape[0]
    pool_width = yp_ref.shape[1]
    group = pool_width // len(POOL_WINDOWS)
    ext_ref[0:HALO, :] = jnp.where(i > 0, prev_ref[...], 0.0)
    ext_ref[HALO:HALO + tl, :] = cur_ref[...]
    ext_ref[HALO + tl:, :] = jnp.where(i < pl.num_programs(1) - 1, next_ref[...], 0.0)

    t = i * tl + lax.broadcasted_iota(jnp.int32, (tl, group), 0)
    for g, w in enumerate(POOL_WINDOWS):
        c0 = g * group
        lo = jnp.clip(t - w // 2, 0, seq_len)
        hi = jnp.clip(t - w // 2 + w, 0, seq_len)
        cnt = (hi - lo).astype(F32)
        s = ext_ref[HALO - w // 2:HALO - w // 2 + tl, c0:c0 + group]
        for d in range(1 - w // 2, w - w // 2):
            s = s + ext_ref[HALO + d:HALO + d + tl, c0:c0 + group]
        p = s / cnt - cur_ref[:, c0:c0 + group]
        y = _dot(p.astype(BF16), pw_ref[g])
        yp_ref[:, c0:c0 + group] = y * ps_ref[:, c0:c0 + group]

    pad = HYENA_SHORT_CONV // 2
    u = cb_ref[...]
    for k in range(HYENA_SHORT_CONV):
        u = u + ext_ref[HALO + k - pad:HALO + k - pad + tl, pool_width:] * cw_ref[k:k + 1, :]
    u_ref[...] = u


def _pool_conv(proj, pool_w, pool_scale, conv_w, conv_b, *, tl=256):
    b, seq_len, width = proj.shape
    pool_width = pool_scale.shape[0]
    hy_width = width - pool_width
    hb = tl // HALO
    last = seq_len // HALO - 1
    return pl.pallas_call(
        functools.partial(_pool_conv_body, seq_len),
        out_shape=(jax.ShapeDtypeStruct((b, seq_len, pool_width), F32),
                   jax.ShapeDtypeStruct((b, seq_len, hy_width), F32)),
        grid=(b, seq_len // tl),
        in_specs=[
            pl.BlockSpec((None, tl, width), lambda bi, i: (bi, i, 0)),
            pl.BlockSpec((None, HALO, width), lambda bi, i: (bi, jnp.maximum(i * hb - 1, 0), 0)),
            pl.BlockSpec((None, HALO, width), lambda bi, i: (bi, jnp.minimum((i + 1) * hb, last), 0)),
            pl.BlockSpec(pool_w.shape, lambda bi, i: (0, 0, 0)),
            pl.BlockSpec((1, pool_width), lambda bi, i: (0, 0)),
            pl.BlockSpec(conv_w.shape, lambda bi, i: (0, 0)),
            pl.BlockSpec((1, hy_width), lambda bi, i: (0, 0)),
        ],
        out_specs=(pl.BlockSpec((None, tl, pool_width), lambda bi, i: (bi, i, 0)),
                   pl.BlockSpec((None, tl, hy_width), lambda bi, i: (bi, i, 0))),
        scratch_shapes=[pltpu.VMEM((tl + 2 * HALO, width), F32)],
        compiler_params=_cparams(("parallel", "parallel")),
        name="pool_conv",
    )(proj, proj, proj, pool_w, pool_scale.reshape(1, pool_width), conv_w,
      conv_b.reshape(1, hy_width))


@functools.lru_cache(maxsize=None)
def _dft_constants():
    n = DFT_N1 * DFT_N2
    half = DFT_N1 // 2
    k1 = np.arange(DFT_K1)
    n1 = np.arange(half)
    ang = 2.0 * np.pi * np.outer(k1, n1) / DFT_N1
    f_fwd = np.zeros((2 * DFT_QP, half))
    f_fwd[:DFT_K1] = np.cos(ang)
    f_fwd[DFT_QP:DFT_QP + DFT_K1] = -np.sin(ang)
    n2 = np.arange(DFT_N2)
    k2 = np.arange(DFT_N2)
    g = np.zeros((DFT_K1, 2 * DFT_N2, 2 * DFT_N2))
    for a in range(DFT_K1):
        ph = 2.0 * np.pi * np.outer(a + DFT_N1 * k2, n2) / n
        gre, gim = np.cos(ph), -np.sin(ph)
        g[a] = np.block([[gre, -gim], [gim, gre]])
    weight = np.full(DFT_K1, 2.0)
    weight[0] = 1.0
    weight[-1] = 1.0
    f_inv = np.zeros((half, 2 * DFT_QP))
    f_inv[:, :DFT_K1] = weight * np.cos(ang.T) / n
    f_inv[:, DFT_QP:DFT_QP + DFT_K1] = -weight * np.sin(ang.T) / n
    return (f_fwd.astype(np.float32), g.astype(np.float32),
            np.transpose(g, (0, 2, 1)).astype(np.float32), f_inv.astype(np.float32))


def _pad_rows(dst_ref, src):
    for n1 in range(DFT_N1 // 2):
        dst_ref[n1 * DFT_PITCH:n1 * DFT_PITCH + DFT_N2, :] = src[n1 * DFT_N2:(n1 + 1) * DFT_N2, :]


def _dft_stage_n1(u_ref, a_ref, ff_ref):
    half = DFT_N1 // 2

    def step(n2, carry):
        x = u_ref[pl.ds(n2, half, stride=DFT_PITCH), :]
        a_ref[pl.ds(n2, 2 * DFT_QP, stride=DFT_PITCH), :] = _dot(ff_ref[...], x.astype(BF16))
        return carry

    lax.fori_loop(0, DFT_N2, step, 0)


def _load_planes(a_ref, k1):
    re = a_ref[pl.ds(pl.multiple_of(k1 * DFT_PITCH, SUBLANES), DFT_N2), :]
    im = a_ref[pl.ds(pl.multiple_of((DFT_QP + k1) * DFT_PITCH, SUBLANES), DFT_N2), :]
    return jnp.concatenate([re, im], axis=0)


def _store_planes(a_ref, k1, z):
    a_ref[pl.ds(pl.multiple_of(k1 * DFT_PITCH, SUBLANES), DFT_N2), :] = z[:DFT_N2]
    a_ref[pl.ds(pl.multiple_of((DFT_QP + k1) * DFT_PITCH, SUBLANES), DFT_N2), :] = z[DFT_N2:]


def _spectrum_body(fw_ref, bw_ref, ssf_ref, ssb_ref, ff_ref, g_ref, h_ref, u_ref, a_ref):
    norm = lax.rsqrt(ssf_ref[...] + ssb_ref[...])
    for part, src_ref in enumerate((fw_ref, bw_ref)):
        _pad_rows(u_ref, src_ref)
        _dft_stage_n1(u_ref, a_ref, ff_ref)
        sign = 1.0 if part == 0 else -1.0

        def step(k1, carry):
            xh = _dot(g_ref[k1], _load_planes(a_ref, k1).astype(BF16))
            re = xh[:DFT_N2] * norm
            im = xh[DFT_N2:] * (sign * norm)
            if part == 0:
                h_ref[k1, :DFT_N2, :] = re
                h_ref[k1, DFT_N2:, :] = im
            else:
                h_ref[k1, :DFT_N2, :] += re
                h_ref[k1, DFT_N2:, :] += im
            return carry

        lax.fori_loop(0, DFT_K1, step, 0)


def _filter_spectrum(taps, sumsq, width):
    seq_len = taps.shape[0]
    ff, g, _, _ = _dft_constants()
    cb = width // LANES
    half = DFT_N1 // 2
    return pl.pallas_call(
        _spectrum_body,
        out_shape=jax.ShapeDtypeStruct((HYENA_ORDER, DFT_K1, 2 * DFT_N2, width), F32),
        grid=(HYENA_ORDER, cb),
        in_specs=[
            pl.BlockSpec((seq_len, LANES), lambda o, c: (0, (2 * o) * cb + c)),
            pl.BlockSpec((seq_len, LANES), lambda o, c: (0, (2 * o + 1) * cb + c)),
            pl.BlockSpec((1, LANES), lambda o, c: (0, (2 * o) * cb + c)),
            pl.BlockSpec((1, LANES), lambda o, c: (0, (2 * o + 1) * cb + c)),
            pl.BlockSpec((2 * DFT_QP, half), lambda o, c: (0, 0)),
            pl.BlockSpec((DFT_K1, 2 * DFT_N2, 2 * DFT_N2), lambda o, c: (0, 0, 0)),
        ],
        out_specs=pl.BlockSpec((None, DFT_K1, 2 * DFT_N2, LANES), lambda o, c: (o, 0, 0, c)),
        scratch_shapes=[pltpu.VMEM((half * DFT_PITCH, LANES), F32),
                        pltpu.VMEM((2 * DFT_QP * DFT_PITCH, LANES), F32)],
        compiler_params=_cparams(("parallel", "parallel")),
        name="filter_spectrum",
    )(taps, taps, sumsq, sumsq, jnp.asarray(ff).astype(BF16), jnp.asarray(g).astype(BF16))


def _hyena_body(v_ref, gate_ref, h_ref, bias_ref, ff_ref, g_ref, gt_ref, fi_ref, o_ref,
                u_ref, y_ref, a_ref):
    order = pl.program_id(2)
    half = DFT_N1 // 2

    @pl.when(order == 0)
    def _():
        _pad_rows(u_ref, v_ref)

    _dft_stage_n1(u_ref, a_ref, ff_ref)

    def freq_step(k1, carry):
        xh = _dot(g_ref[k1], _load_planes(a_ref, k1).astype(BF16))
        xre, xim = xh[:DFT_N2], xh[DFT_N2:]
        hre, him = h_ref[k1, :DFT_N2, :], h_ref[k1, DFT_N2:, :]
        yh = jnp.concatenate([xre * hre - xim * him, xre * him + xim * hre], axis=0)
        _store_planes(a_ref, k1, _dot(gt_ref[k1], yh.astype(BF16)))
        return carry

    lax.fori_loop(0, DFT_K1, freq_step, 0)

    def time_step(n2, carry):
        z = a_ref[pl.ds(n2, 2 * DFT_QP, stride=DFT_PITCH), :]
        y_ref[pl.ds(n2, half, stride=DFT_PITCH), :] = _dot(fi_ref[...], z.astype(BF16))
        return carry

    lax.fori_loop(0, DFT_N2, time_step, 0)

    bias = bias_ref[...]
    for n1 in range(half):
        rows = slice(n1 * DFT_PITCH, n1 * DFT_PITCH + DFT_N2)
        z = gate_ref[n1 * DFT_N2:(n1 + 1) * DFT_N2, :] * (y_ref[rows, :] + u_ref[rows, :] * bias)
        u_ref[rows, :] = z
        o_ref[n1 * DFT_N2:(n1 + 1) * DFT_N2, :] = z


def _hyena_long_conv(u, spec, bias):
    b, seq_len, _ = u.shape
    width = bias.shape[1]
    cb = width // LANES
    half = DFT_N1 // 2
    ff, g, gt, fi = _dft_constants()
    const2 = lambda c, bi, o: (0, 0)
    const3 = lambda c, bi, o: (0, 0, 0)
    return pl.pallas_call(
        _hyena_body,
        out_shape=jax.ShapeDtypeStruct((b, seq_len, width), F32),
        grid=(cb, b, HYENA_ORDER),
        in_specs=[
            pl.BlockSpec((None, seq_len, LANES), lambda c, bi, o: (bi, 0, HYENA_ORDER * cb + c)),
            pl.BlockSpec((None, seq_len, LANES), lambda c, bi, o: (bi, 0, o * cb + c)),
            pl.BlockSpec((None, DFT_K1, 2 * DFT_N2, LANES), lambda c, bi, o: (o, 0, 0, c)),
            pl.BlockSpec((None, 1, LANES), lambda c, bi, o: (o, 0, c)),
            pl.BlockSpec((2 * DFT_QP, half), const2),
            pl.BlockSpec((DFT_K1, 2 * DFT_N2, 2 * DFT_N2), const3),
            pl.BlockSpec((DFT_K1, 2 * DFT_N2, 2 * DFT_N2), const3),
            pl.BlockSpec((half, 2 * DFT_QP), const2),
        ],
        out_specs=pl.BlockSpec((None, seq_len, LANES), lambda c, bi, o: (bi, 0, c)),
        scratch_shapes=[pltpu.VMEM((half * DFT_PITCH, LANES), F32),
                        pltpu.VMEM((half * DFT_PITCH, LANES), F32),
                        pltpu.VMEM((2 * DFT_QP * DFT_PITCH, LANES), F32)],
        compiler_params=_cparams(("parallel", "parallel", "arbitrary")),
        name="hyena_long_conv",
    )(u, u, spec, bias.reshape(HYENA_ORDER, 1, width), jnp.asarray(ff).astype(BF16),
      jnp.asarray(g).astype(BF16), jnp.asarray(gt).astype(BF16), jnp.asarray(fi).astype(BF16))


def _hdot(a, b):
    return jnp.dot(a, b, preferred_element_type=F32, precision=lax.Precision.HIGHEST)


def _filter_taps_body(bwd_cols, z_ref, t_ref, w1_ref, b1_ref, w2_ref, b2_ref, w3_ref, b3_ref,
                      sf_ref, wo_ref, decay_ref, taps_ref, ss_ref):
    i = pl.program_id(0)
    h = jnp.sin(sf_ref[0:1, :] * (_hdot(z_ref[...], w1_ref[...]) + b1_ref[...]))
    h = jnp.sin(sf_ref[1:2, :] * (_hdot(h, w2_ref[...]) + b2_ref[...]))
    h = jnp.sin(sf_ref[2:3, :] * (_hdot(h, w3_ref[...]) + b3_ref[...]))
    h = _hdot(h, wo_ref[...])
    h = h * jnp.exp(-t_ref[...] * jnp.abs(decay_ref[...]))
    row = i * h.shape[0] + lax.broadcasted_iota(jnp.int32, h.shape, 0)
    h = jnp.where((row == 0) & (bwd_cols[...] > 0.0), 0.0, h)
    taps_ref[...] = h

    @pl.when(i == 0)
    def _():
        ss_ref[...] = jnp.zeros_like(ss_ref)

    ss_ref[...] += jnp.sum(h * h, axis=0, keepdims=True)


def _filter_taps(z, t, w1, b1, w2, b2, w3, b3, sin_freq, w_out, decay, *, tl=512):
    seq_len = z.shape[0]
    hid = LANES
    n_out = w_out.shape[1]
    width = n_out // (2 * HYENA_ORDER)

    def pad2(a, rows, cols):
        return jnp.pad(a, ((0, rows - a.shape[0]), (0, cols - a.shape[1])))

    is_bwd = np.tile(np.repeat(np.array([0.0, 1.0], np.float32), width), HYENA_ORDER)[None, :]
    full = lambda a: pl.BlockSpec(a.shape, lambda i: (0, 0))
    args = (
        pad2(z, seq_len, hid), t.reshape(seq_len, 1),
        pad2(w1, hid, hid), pad2(b1[None, :], 1, hid),
        pad2(w2, hid, hid), pad2(b2[None, :], 1, hid),
        pad2(w3, hid, hid), pad2(b3[None, :], 1, hid),
        pad2(sin_freq, 3, hid), pad2(w_out, hid, n_out), decay.reshape(1, n_out),
    )
    bwd_cols = jnp.asarray(is_bwd)
    return pl.pallas_call(
        _filter_taps_body,
        out_shape=(jax.ShapeDtypeStruct((seq_len, n_out), F32),
                   jax.ShapeDtypeStruct((1, n_out), F32)),
        grid=(seq_len // tl,),
        in_specs=[full(bwd_cols),
                  pl.BlockSpec((tl, hid), lambda i: (i, 0)),
                  pl.BlockSpec((tl, 1), lambda i: (i, 0))] + [full(a) for a in args[2:]],
        out_specs=(pl.BlockSpec((tl, n_out), lambda i: (i, 0)),
                   pl.BlockSpec((1, n_out), lambda i: (0, 0))),
        compiler_params=_cparams(("arbitrary",)),
        name="filter_taps",
    )(bwd_cols, *args)


def _mla_qkv_body(c_ref, qg_ref, kvg_ref, wq_ref, wqr_ref, wk_ref, wv_ref, cos_ref, sin_ref,
                  q_ref, k_ref, v_ref):
    c = c_ref[...]
    cos = cos_ref[...]
    sin = sin_ref[...]
    cq = _rms(c[:, :MLA_Q_LORA], qg_ref[...]).astype(BF16)
    ckv = _rms(c[:, MLA_Q_LORA:MLA_Q_LORA + MLA_KV_LORA], kvg_ref[...]).astype(BF16)
    kr0 = MLA_Q_LORA + MLA_KV_LORA
    k_rope = c[:, kr0:kr0 + LANES] * cos + c[:, kr0 + LANES:kr0 + 2 * LANES] * sin
    for h in range(MLA_HEADS):
        cols = slice(h * LANES, (h + 1) * LANES)
        q = _dot(cq, wq_ref[:, cols]) * cos + _dot(cq, wqr_ref[:, cols]) * sin
        q_ref[h] = q.astype(BF16)
        k_ref[h] = (_dot(ckv, wk_ref[:, cols]) + k_rope).astype(BF16)
    v_ref[...] = _dot(ckv, wv_ref[...]).astype(BF16)


def _mla_qkv(c, q_norm_g, kv_norm_g, wq, wq_rot, wk, wv, cos_t, sin_t, batch, *, tl=512):
    m, cw = c.shape
    seq_len = m // batch
    nl = seq_len // tl
    hv = MLA_HEADS * MLA_V
    full = lambda a: pl.BlockSpec(a.shape, lambda bi, i: (0, 0))
    qg = q_norm_g.reshape(1, -1)
    kvg = kv_norm_g.reshape(1, -1)
    head_out = jax.ShapeDtypeStruct((batch, MLA_HEADS, seq_len, LANES), BF16)
    head_spec = pl.BlockSpec((None, MLA_HEADS, tl, LANES), lambda bi, i: (bi, 0, i, 0))
    return pl.pallas_call(
        _mla_qkv_body,
        out_shape=(head_out, head_out, jax.ShapeDtypeStruct((batch, seq_len, hv), BF16)),
        grid=(batch, nl),
        in_specs=[pl.BlockSpec((tl, cw), lambda bi, i: (bi * nl + i, 0)),
                  full(qg), full(kvg), full(wq), full(wq_rot), full(wk), full(wv),
                  pl.BlockSpec((tl, LANES), lambda bi, i: (i, 0)),
                  pl.BlockSpec((tl, LANES), lambda bi, i: (i, 0))],
        out_specs=(head_spec, head_spec, pl.BlockSpec((None, tl, hv), lambda bi, i: (bi, i, 0))),
        compiler_params=_cparams(("parallel", "parallel")),
        name="mla_qkv",
    )(c, qg, kvg, wq, wq_rot, wk, wv, cos_t, sin_t)


def _attention_body(exp2_scale, q_ref, k_ref, v_ref, o_ref):
    outs = []
    for h in range(q_ref.shape[0]):
        s = lax.dot_general(q_ref[h], k_ref[h], (((1,), (1,)), ((), ())),
                            preferred_element_type=F32)
        m = jnp.max(s, axis=-1, keepdims=True)
        p = jnp.exp2((s - m) * exp2_scale)
        denom = jnp.sum(p, axis=-1, keepdims=True)
        outs.append(_dot(p.astype(BF16), v_ref[...]) / denom)
    lane = lax.broadcasted_iota(jnp.int32, outs[0].shape, 1)
    o_ref[...] = jnp.where(lane < MLA_V, outs[0], outs[1]).astype(o_ref.dtype)


def _attention(q, k, v, *, tq=256):
    batch, heads, seq_len, _ = q.shape
    pair = LANES // MLA_V
    exp2_scale = (MLA_NOPE + MLA_ROPE) ** -0.5 * math.log2(math.e)
    return pl.pallas_call(
        functools.partial(_attention_body, exp2_scale),
        out_shape=jax.ShapeDtypeStruct((batch, seq_len, heads * MLA_V), BF16),
        grid=(batch, heads // pair, seq_len // tq),
        in_specs=[
            pl.BlockSpec((None, pair, tq, LANES), lambda bi, p, i: (bi, p, i, 0)),
            pl.BlockSpec((None, pair, seq_len, LANES), lambda bi, p, i: (bi, p, 0, 0)),
            pl.BlockSpec((None, seq_len, LANES), lambda bi, p, i: (bi, 0, p)),
        ],
        out_specs=pl.BlockSpec((None, tq, LANES), lambda bi, p, i: (bi, i, p)),
        compiler_params=_cparams(("parallel", "parallel", "parallel")),
        name="attention",
    )(q, k, v)


def _rope_rotation():
    half = MLA_ROPE // 2
    r = np.zeros((MLA_ROPE, MLA_ROPE), np.float32)
    for j in range(half):
        r[j + half, j] = -1.0
        r[j, j + half] = 1.0
    return jnp.asarray(r)


def _mla_weights(w_dq, w_uq, w_dkv, w_ukv):
    d = w_dq.shape[0]
    rot = _rope_rotation()
    qh = w_uq.reshape(MLA_Q_LORA, MLA_HEADS, MLA_NOPE + MLA_ROPE)
    zq = jnp.zeros((MLA_Q_LORA, MLA_HEADS, LANES - MLA_NOPE - MLA_ROPE), F32)
    wq = jnp.concatenate([qh, zq], axis=-1).reshape(MLA_Q_LORA, MLA_HEADS * LANES)
    q_rot = jnp.einsum("lhr,rs->lhs", qh[..., MLA_NOPE:], rot)
    wq_rot = jnp.concatenate([jnp.zeros((MLA_Q_LORA, MLA_HEADS, MLA_NOPE), F32), q_rot, zq],
                             axis=-1).reshape(MLA_Q_LORA, MLA_HEADS * LANES)
    kvh = w_ukv.reshape(MLA_KV_LORA, MLA_HEADS, MLA_NOPE + MLA_V)
    wk = jnp.concatenate([kvh[..., :MLA_NOPE],
                          jnp.zeros((MLA_KV_LORA, MLA_HEADS, LANES - MLA_NOPE), F32)],
                         axis=-1).reshape(MLA_KV_LORA, MLA_HEADS * LANES)
    wv = kvh[..., MLA_NOPE:].reshape(MLA_KV_LORA, MLA_HEADS * MLA_V)
    w_kr = w_dkv[:, MLA_KV_LORA:]

    def slab(w):
        return jnp.concatenate([jnp.zeros((d, MLA_NOPE), F32), w,
                                jnp.zeros((d, LANES - MLA_NOPE - MLA_ROPE), F32)], axis=-1)

    w_down = jnp.concatenate([w_dq, w_dkv[:, :MLA_KV_LORA], slab(w_kr), slab(w_kr @ rot)], axis=-1)
    return (w_down.astype(BF16), wq.astype(BF16), wq_rot.astype(BF16), wk.astype(BF16),
            wv.astype(BF16))


def _rope_tables(seq_len):
    inv_freq = ROPE_THETA ** (-jnp.arange(0, MLA_ROPE, 2, dtype=F32) / MLA_ROPE)
    ang = jnp.arange(seq_len, dtype=F32)[:, None] * inv_freq[None, :]
    cos = jnp.cos(ang)
    sin = jnp.sin(ang)
    tail = LANES - MLA_NOPE - MLA_ROPE
    cos_t = jnp.concatenate([jnp.ones((seq_len, MLA_NOPE), F32), cos, cos,
                             jnp.ones((seq_len, tail), F32)], axis=-1)
    sin_t = jnp.concatenate([jnp.zeros((seq_len, MLA_NOPE), F32), sin, sin,
                             jnp.zeros((seq_len, tail), F32)], axis=-1)
    return cos_t, sin_t


def _position_features(seq_len, pos_dim):
    t = jnp.linspace(0.0, 1.0, seq_len, dtype=F32)
    bands = (pos_dim - 1) // 2
    w = 2.0 * math.pi * jnp.arange(seq_len, dtype=F32) / seq_len
    f = jnp.linspace(1e-4, bands - 1, bands, dtype=F32)
    phase = w[:, None] * f[None, :]
    z = jnp.concatenate([t[:, None], jnp.cos(phase), -jnp.sin(phase)], axis=-1)
    return t, z


def kernel(x, norm_g, ffn_w_gate, ffn_w_up, ffn_w_down, mix_w_in, pool_w, pool_scale, hyena_conv_w, hyena_conv_b, hyena_ffn_w1, hyena_ffn_b1, hyena_ffn_w2, hyena_ffn_b2, hyena_ffn_w3, hyena_ffn_b3, hyena_sin_freq, hyena_ffn_w_out, hyena_decay, hyena_bias, mix_w_out, mla_w_dq, mla_q_norm_g, mla_w_uq, mla_w_dkv, mla_kv_norm_g, mla_w_ukv, mla_w_o, final_norm_g):
    batch, seq_len, d = x.shape
    depth = norm_g.shape[0]
    assert 2 * seq_len == DFT_N1 * DFT_N2
    hy_width = hyena_bias.shape[-1]
    t_pos, z_pos = _position_features(seq_len, hyena_ffn_w1.shape[1])
    cos_t, sin_t = _rope_tables(seq_len)

    x = x.reshape(batch * seq_len, d)
    for i in range(depth):
        j = i // 2
        x = _ffn(x, norm_g[i, 0], ffn_w_gate[i, 0].astype(BF16), ffn_w_up[i, 0].astype(BF16),
                 ffn_w_down[i, 0].astype(BF16))
        if i % 2 == 0:
            proj = _norm_proj(x, norm_g[i, 1], mix_w_in[j].astype(BF16))
            y_pool, u = _pool_conv(proj.reshape(batch, seq_len, -1), pool_w[j].astype(BF16),
                                   pool_scale[j], hyena_conv_w[j], hyena_conv_b[j])
            taps, sumsq = _filter_taps(z_pos, t_pos, hyena_ffn_w1[j], hyena_ffn_b1[j],
                                       hyena_ffn_w2[j], hyena_ffn_b2[j], hyena_ffn_w3[j],
                                       hyena_ffn_b3[j], hyena_sin_freq[j], hyena_ffn_w_out[j],
                                       hyena_decay[j])
            spec = _filter_spectrum(taps, sumsq, hy_width)
            y_hyena = _hyena_long_conv(u, spec, hyena_bias[j])
            x = _res_proj(x, [y_pool.reshape(batch * seq_len, -1),
                              y_hyena.reshape(batch * seq_len, -1)], mix_w_out[j].astype(BF16))
        else:
            w_down, wq, wq_rot, wk, wv = _mla_weights(mla_w_dq[j], mla_w_uq[j], mla_w_dkv[j],
                                                      mla_w_ukv[j])
            c = _norm_proj(x, norm_g[i, 1], w_down)
            q, k, v = _mla_qkv(c, mla_q_norm_g[j], mla_kv_norm_g[j], wq, wq_rot, wk, wv,
                               cos_t, sin_t, batch)
            o = _attention(q, k, v)
            x = _res_proj(x, [o.reshape(batch * seq_len, -1)], mla_w_o[j].astype(BF16))
        x = _ffn(x, norm_g[i, 2], ffn_w_gate[i, 1].astype(BF16), ffn_w_up[i, 1].astype(BF16),
                 ffn_w_down[i, 1].astype(BF16),
                 final_g=final_norm_g if i == depth - 1 else None)
    return x.reshape(batch, seq_len, d)
```

```python
import functools
import math

import numpy as np
import jax
import jax.numpy as jnp
from jax import lax
from jax.experimental import pallas as pl
from jax.experimental.pallas import tpu as pltpu

F32 = jnp.float32
BF16 = jnp.bfloat16

RMS_EPS = 1e-6
MACARON_WEIGHT = 0.5
POOL_WINDOWS = (2, 4, 8, 16)
HYENA_ORDER = 2
HYENA_SHORT_CONV = 3
MLA_HEADS = 16
MLA_Q_LORA = 256
MLA_KV_LORA = 128
MLA_NOPE = 64
MLA_ROPE = 32
MLA_V = 64
ROPE_THETA = 10000.0

LANES = 128
SUBLANES = 8
VMEM_LIMIT = 56 * 2**20

DFT_N1 = 64
DFT_N2 = 128
DFT_K1 = DFT_N1 // 2 + 1
DFT_QP = 40
DFT_PITCH = DFT_N2 + SUBLANES
PLANE_ROWS_PER_DOT = 8
PLANE_UNROLL = 4
K1_UNROLL = 11
HALO = 16


def _cparams(semantics):
    return pltpu.CompilerParams(dimension_semantics=semantics, vmem_limit_bytes=VMEM_LIMIT)


def _rms(x, g):
    return x * lax.rsqrt(jnp.mean(x * x, axis=-1, keepdims=True) + RMS_EPS) * g


def _dot(a, b):
    return jnp.dot(a, b, preferred_element_type=F32)


def _ffn_body(final_norm, x_ref, g_ref, wg_ref, wu_ref, wd_ref, fg_ref, o_ref):
    x = x_ref[...]
    h = _rms(x, g_ref[...]).astype(BF16)
    gate = _dot(h, wg_ref[...])
    up = _dot(h, wu_ref[...])
    act = gate / (1.0 + jnp.exp(-gate)) * up
    y = x + MACARON_WEIGHT * _dot(act.astype(BF16), wd_ref[...])
    if final_norm:
        y = _rms(y, fg_ref[...])
    o_ref[...] = y


def _resident(shape):
    return pl.BlockSpec(shape, lambda *_: (0,) * len(shape), pipeline_mode=pl.Buffered(1))


def _ffn(x, g, wg, wu, wd, final_g=None, *, tm=512):
    m, d = x.shape
    final_norm = final_g is not None
    fg = final_g if final_norm else g
    return pl.pallas_call(
        functools.partial(_ffn_body, final_norm),
        out_shape=jax.ShapeDtypeStruct((m, d), F32),
        grid=(m // tm,),
        in_specs=[
            pl.BlockSpec((tm, d), lambda i: (i, 0)),
            _resident((1, d)),
            _resident(wg.shape),
            _resident(wu.shape),
            _resident(wd.shape),
            _resident((1, d)),
        ],
        out_specs=pl.BlockSpec((tm, d), lambda i: (i, 0)),
        compiler_params=_cparams(("parallel",)),
        name="ffn",
    )(x, g.reshape(1, d), wg, wu, wd, fg.reshape(1, d))


def _norm_proj_body(x_ref, g_ref, w_ref, o_ref):
    h = _rms(x_ref[...], g_ref[...]).astype(BF16)
    o_ref[...] = _dot(h, w_ref[...])


def _norm_proj(x, g, w, *, tm=512):
    m, d = x.shape
    n = w.shape[1]
    return pl.pallas_call(
        _norm_proj_body,
        out_shape=jax.ShapeDtypeStruct((m, n), F32),
        grid=(m // tm,),
        in_specs=[
            pl.BlockSpec((tm, d), lambda i: (i, 0)),
            pl.BlockSpec((1, d), lambda i: (0, 0)),
            pl.BlockSpec((d, n), lambda i: (0, 0)),
        ],
        out_specs=pl.BlockSpec((tm, n), lambda i: (i, 0)),
        compiler_params=_cparams(("parallel",)),
        name="norm_proj",
    )(x, g.reshape(1, d), w)


def _res_proj_body(n_act, x_ref, *refs):
    act_refs, w_ref, o_ref = refs[:n_act], refs[n_act], refs[n_act + 1]
    y = x_ref[...]
    row = 0
    for a_ref in act_refs:
        k = a_ref.shape[1]
        y = y + _dot(a_ref[...].astype(BF16), w_ref[row:row + k, :])
        row += k
    o_ref[...] = y


def _res_proj(x, acts, w, *, tm=512):
    m, d = x.shape
    return pl.pallas_call(
        functools.partial(_res_proj_body, len(acts)),
        out_shape=jax.ShapeDtypeStruct((m, d), F32),
        grid=(m // tm,),
        in_specs=[pl.BlockSpec((tm, d), lambda i: (i, 0))]
        + [pl.BlockSpec((tm, a.shape[1]), lambda i: (i, 0)) for a in acts]
        + [pl.BlockSpec(w.shape, lambda i: (0, 0))],
        out_specs=pl.BlockSpec((tm, d), lambda i: (i, 0)),
        compiler_params=_cparams(("parallel",)),
        name="res_proj",
    )(x, *acts, w)


def _pool_conv_body(seq_len, cur_ref, prev_ref, next_ref, pw_ref, ps_ref, cw_ref, cb_ref,
                    yp_ref, u_ref, ext_ref):
    i = pl.program_id(1)
    tl = cur_ref.shape[0]
    pool_width = yp_ref.shape[1]
    group = pool_width // len(POOL_WINDOWS)
    ext_ref[0:HALO, :] = jnp.where(i > 0, prev_ref[...], 0.0)
    ext_ref[HALO:HALO + tl, :] = cur_ref[...]
    ext_ref[HALO + tl:, :] = jnp.where(i < pl.num_programs(1) - 1, next_ref[...], 0.0)

    t = i * tl + lax.broadcasted_iota(jnp.int32, (tl, group), 0)
    for g, w in enumerate(POOL_WINDOWS):
        c0 = g * group
        lo = jnp.clip(t - w // 2, 0, seq_len)
        hi = jnp.clip(t - w // 2 + w, 0, seq_len)
        cnt = (hi - lo).astype(F32)
        s = ext_ref[HALO - w // 2:HALO - w // 2 + tl, c0:c0 + group]
        for d in range(1 - w // 2, w - w // 2):
            s = s + ext_ref[HALO + d:HALO + d + tl, c0:c0 + group]
        p = s / cnt - cur_ref[:, c0:c0 + group]
        y = _dot(p.astype(BF16), pw_ref[g])
        yp_ref[:, c0:c0 + group] = y * ps_ref[:, c0:c0 + group]

    pad = HYENA_SHORT_CONV // 2
    u = cb_ref[...]
    for k in range(HYENA_SHORT_CONV):
        u = u + ext_ref[HALO + k - pad:HALO + k - pad + tl, pool_width:] * cw_ref[k:k + 1, :]
    u_ref[...] = u


def _pool_conv(proj, pool_w, pool_scale, conv_w, conv_b, *, tl=256):
    b, seq_len, width = proj.shape
    pool_width = pool_scale.shape[0]
    hy_width = width - pool_width
    hb = tl // HALO
    last = seq_len // HALO - 1
    return pl.pallas_call(
        functools.partial(_pool_conv_body, seq_len),
        out_shape=(jax.ShapeDtypeStruct((b, seq_len, pool_width), F32),
                   jax.ShapeDtypeStruct((b, seq_len, hy_width), F32)),
        grid=(b, seq_len // tl),
        in_specs=[
            pl.BlockSpec((None, tl, width), lambda bi, i: (bi, i, 0)),
            pl.BlockSpec((None, HALO, width), lambda bi, i: (bi, jnp.maximum(i * hb - 1, 0), 0)),
            pl.BlockSpec((None, HALO, width), lambda bi, i: (bi, jnp.minimum((i + 1) * hb, last), 0)),
            pl.BlockSpec(pool_w.shape, lambda bi, i: (0, 0, 0)),
            pl.BlockSpec((1, pool_width), lambda bi, i: (0, 0)),
            pl.BlockSpec(conv_w.shape, lambda bi, i: (0, 0)),
            pl.BlockSpec((1, hy_width), lambda bi, i: (0, 0)),
        ],
        out_specs=(pl.BlockSpec((None, tl, pool_width), lambda bi, i: (bi, i, 0)),
                   pl.BlockSpec((None, tl, hy_width), lambda bi, i: (bi, i, 0))),
        scratch_shapes=[pltpu.VMEM((tl + 2 * HALO, width), F32)],
        compiler_params=_cparams(("parallel", "parallel")),
        name="pool_conv",
    )(proj, proj, proj, pool_w, pool_scale.reshape(1, pool_width), conv_w,
      conv_b.reshape(1, hy_width))


@functools.lru_cache(maxsize=None)
def _dft_constants():
    n = DFT_N1 * DFT_N2
    half = DFT_N1 // 2
    k1 = np.arange(DFT_K1)
    n1 = np.arange(half)
    ang = 2.0 * np.pi * np.outer(k1, n1) / DFT_N1
    f_fwd = np.zeros((2 * DFT_QP, half))
    f_fwd[:DFT_K1] = np.cos(ang)
    f_fwd[DFT_QP:DFT_QP + DFT_K1] = -np.sin(ang)
    n2 = np.arange(DFT_N2)
    k2 = np.arange(DFT_N2)
    g = np.zeros((DFT_K1, 2 * DFT_N2, 2 * DFT_N2))
    for a in range(DFT_K1):
        ph = 2.0 * np.pi * np.outer(a + DFT_N1 * k2, n2) / n
        gre, gim = np.cos(ph), -np.sin(ph)
        g[a] = np.block([[gre, -gim], [gim, gre]])
    weight = np.full(DFT_K1, 2.0)
    weight[0] = 1.0
    weight[-1] = 1.0
    f_inv = np.zeros((half, 2 * DFT_QP))
    f_inv[:, :DFT_K1] = weight * np.cos(ang.T) / n
    f_inv[:, DFT_QP:DFT_QP + DFT_K1] = -weight * np.sin(ang.T) / n
    return (f_fwd.astype(np.float32), g.astype(np.float32),
            np.transpose(g, (0, 2, 1)).astype(np.float32), f_inv.astype(np.float32))


def _pad_rows(dst_ref, src):
    for n1 in range(DFT_N1 // 2):
        dst_ref[n1 * DFT_PITCH:n1 * DFT_PITCH + DFT_N2, :] = src[n1 * DFT_N2:(n1 + 1) * DFT_N2, :]


def _contract_planes(src_ref, dst_ref, f_ref):
    n_dst, n_src = f_ref.shape

    def step(j, carry):
        r0 = j * PLANE_ROWS_PER_DOT
        x = jnp.concatenate(
            [src_ref[pl.ds(r0 + r, n_src, stride=DFT_PITCH), :].astype(BF16)
             for r in range(PLANE_ROWS_PER_DOT)], axis=1)
        y = _dot(f_ref[...], x)
        for r in range(PLANE_ROWS_PER_DOT):
            dst_ref[pl.ds(r0 + r, n_dst, stride=DFT_PITCH), :] = y[:, r * LANES:(r + 1) * LANES]
        return carry

    lax.fori_loop(0, DFT_N2 // PLANE_ROWS_PER_DOT, step, 0, unroll=PLANE_UNROLL)


def _load_planes(a_ref, k1):
    re = a_ref[pl.ds(pl.multiple_of(k1 * DFT_PITCH, SUBLANES), DFT_N2), :]
    im = a_ref[pl.ds(pl.multiple_of((DFT_QP + k1) * DFT_PITCH, SUBLANES), DFT_N2), :]
    return jnp.concatenate([re, im], axis=0)


def _store_planes(a_ref, k1, z):
    a_ref[pl.ds(pl.multiple_of(k1 * DFT_PITCH, SUBLANES), DFT_N2), :] = z[:DFT_N2]
    a_ref[pl.ds(pl.multiple_of((DFT_QP + k1) * DFT_PITCH, SUBLANES), DFT_N2), :] = z[DFT_N2:]


def _spectrum_body(fw_ref, bw_ref, ssf_ref, ssb_ref, ff_ref, g_ref, h_ref, u_ref, a_ref):
    norm = lax.rsqrt(ssf_ref[...] + ssb_ref[...])
    for part, src_ref in enumerate((fw_ref, bw_ref)):
        _pad_rows(u_ref, src_ref)
        _contract_planes(u_ref, a_ref, ff_ref)
        sign = 1.0 if part == 0 else -1.0

        def step(k1, carry):
            xh = _dot(g_ref[k1], _load_planes(a_ref, k1).astype(BF16))
            re = xh[:DFT_N2] * norm
            im = xh[DFT_N2:] * (sign * norm)
            if part == 0:
                h_ref[k1, :DFT_N2, :] = re
                h_ref[k1, DFT_N2:, :] = im
            else:
                h_ref[k1, :DFT_N2, :] += re
                h_ref[k1, DFT_N2:, :] += im
            return carry

        lax.fori_loop(0, DFT_K1, step, 0, unroll=K1_UNROLL)


def _filter_spectrum(taps, sumsq, width):
    seq_len = taps.shape[0]
    ff, g, _, _ = _dft_constants()
    cb = width // LANES
    half = DFT_N1 // 2
    return pl.pallas_call(
        _spectrum_body,
        out_shape=jax.ShapeDtypeStruct((HYENA_ORDER, DFT_K1, 2 * DFT_N2, width), F32),
        grid=(HYENA_ORDER, cb),
        in_specs=[
            pl.BlockSpec((seq_len, LANES), lambda o, c: (0, (2 * o) * cb + c)),
            pl.BlockSpec((seq_len, LANES), lambda o, c: (0, (2 * o + 1) * cb + c)),
            pl.BlockSpec((1, LANES), lambda o, c: (0, (2 * o) * cb + c)),
            pl.BlockSpec((1, LANES), lambda o, c: (0, (2 * o + 1) * cb + c)),
            pl.BlockSpec((2 * DFT_QP, half), lambda o, c: (0, 0)),
            pl.BlockSpec((DFT_K1, 2 * DFT_N2, 2 * DFT_N2), lambda o, c: (0, 0, 0)),
        ],
        out_specs=pl.BlockSpec((None, DFT_K1, 2 * DFT_N2, LANES), lambda o, c: (o, 0, 0, c)),
        scratch_shapes=[pltpu.VMEM((half * DFT_PITCH, LANES), F32),
                        pltpu.VMEM((2 * DFT_QP * DFT_PITCH, LANES), F32)],
        compiler_params=_cparams(("parallel", "parallel")),
        name="filter_spectrum",
    )(taps, taps, sumsq, sumsq, jnp.asarray(ff).astype(BF16), jnp.asarray(g).astype(BF16))


def _hyena_body(v_ref, gate_ref, h_ref, bias_ref, ff_ref, g_ref, gt_ref, fi_ref, o_ref,
                u_ref, y_ref, a_ref):
    order = pl.program_id(2)
    half = DFT_N1 // 2

    @pl.when(order == 0)
    def _():
        _pad_rows(u_ref, v_ref)

    _contract_planes(u_ref, a_ref, ff_ref)

    def freq_step(k1, carry):
        xh = _dot(g_ref[k1], _load_planes(a_ref, k1).astype(BF16))
        xre, xim = xh[:DFT_N2], xh[DFT_N2:]
        hre, him = h_ref[k1, :DFT_N2, :], h_ref[k1, DFT_N2:, :]
        yh = jnp.concatenate([xre * hre - xim * him, xre * him + xim * hre], axis=0)
        _store_planes(a_ref, k1, _dot(gt_ref[k1], yh.astype(BF16)))
        return carry

    lax.fori_loop(0, DFT_K1, freq_step, 0, unroll=K1_UNROLL)
    _contract_planes(a_ref, y_ref, fi_ref)

    bias = bias_ref[...]
    for n1 in range(half):
        rows = slice(n1 * DFT_PITCH, n1 * DFT_PITCH + DFT_N2)
        z = gate_ref[n1 * DFT_N2:(n1 + 1) * DFT_N2, :] * (y_ref[rows, :] + u_ref[rows, :] * bias)
        u_ref[rows, :] = z
        o_ref[n1 * DFT_N2:(n1 + 1) * DFT_N2, :] = z


def _hyena_long_conv(u, spec, bias):
    b, seq_len, _ = u.shape
    width = bias.shape[1]
    cb = width // LANES
    half = DFT_N1 // 2
    ff, g, gt, fi = _dft_constants()
    const2 = lambda c, bi, o: (0, 0)
    const3 = lambda c, bi, o: (0, 0, 0)
    return pl.pallas_call(
        _hyena_body,
        out_shape=jax.ShapeDtypeStruct((b, seq_len, width), F32),
        grid=(cb, b, HYENA_ORDER),
        in_specs=[
            pl.BlockSpec((None, seq_len, LANES), lambda c, bi, o: (bi, 0, HYENA_ORDER * cb + c)),
            pl.BlockSpec((None, seq_len, LANES), lambda c, bi, o: (bi, 0, o * cb + c)),
            pl.BlockSpec((None, DFT_K1, 2 * DFT_N2, LANES), lambda c, bi, o: (o, 0, 0, c)),
            pl.BlockSpec((None, 1, LANES), lambda c, bi, o: (o, 0, c)),
            pl.BlockSpec((2 * DFT_QP, half), const2),
            pl.BlockSpec((DFT_K1, 2 * DFT_N2, 2 * DFT_N2), const3),
            pl.BlockSpec((DFT_K1, 2 * DFT_N2, 2 * DFT_N2), const3),
            pl.BlockSpec((half, 2 * DFT_QP), const2),
        ],
        out_specs=pl.BlockSpec((None, seq_len, LANES), lambda c, bi, o: (bi, 0, c)),
        scratch_shapes=[pltpu.VMEM((half * DFT_PITCH, LANES), F32),
                        pltpu.VMEM((half * DFT_PITCH, LANES), F32),
                        pltpu.VMEM((2 * DFT_QP * DFT_PITCH, LANES), F32)],
        compiler_params=_cparams(("parallel", "parallel", "arbitrary")),
        name="hyena_long_conv",
    )(u, u, spec, bias.reshape(HYENA_ORDER, 1, width), jnp.asarray(ff).astype(BF16),
      jnp.asarray(g).astype(BF16), jnp.asarray(gt).astype(BF16), jnp.asarray(fi).astype(BF16))


def _hdot(a, b):
    return jnp.dot(a, b, preferred_element_type=F32, precision=lax.Precision.HIGHEST)


def _filter_taps_body(bwd_cols, z_ref, t_ref, w1_ref, b1_ref, w2_ref, b2_ref, w3_ref, b3_ref,
                      sf_ref, wo_ref, decay_ref, taps_ref, ss_ref):
    i = pl.program_id(0)
    h = jnp.sin(sf_ref[0:1, :] * (_hdot(z_ref[...], w1_ref[...]) + b1_ref[...]))
    h = jnp.sin(sf_ref[1:2, :] * (_hdot(h, w2_ref[...]) + b2_ref[...]))
    h = jnp.sin(sf_ref[2:3, :] * (_hdot(h, w3_ref[...]) + b3_ref[...]))
    h = _hdot(h, wo_ref[...])
    h = h * jnp.exp(-t_ref[...] * jnp.abs(decay_ref[...]))
    row = i * h.shape[0] + lax.broadcasted_iota(jnp.int32, h.shape, 0)
    h = jnp.where((row == 0) & (bwd_cols[...] > 0.0), 0.0, h)
    taps_ref[...] = h

    @pl.when(i == 0)
    def _():
        ss_ref[...] = jnp.zeros_like(ss_ref)

    ss_ref[...] += jnp.sum(h * h, axis=0, keepdims=True)


def _filter_taps(z, t, w1, b1, w2, b2, w3, b3, sin_freq, w_out, decay, *, tl=512):
    seq_len = z.shape[0]
    hid = LANES
    n_out = w_out.shape[1]
    width = n_out // (2 * HYENA_ORDER)

    def pad2(a, rows, cols):
        return jnp.pad(a, ((0, rows - a.shape[0]), (0, cols - a.shape[1])))

    is_bwd = np.tile(np.repeat(np.array([0.0, 1.0], np.float32), width), HYENA_ORDER)[None, :]
    full = lambda a: pl.BlockSpec(a.shape, lambda i: (0, 0))
    args = (
        pad2(z, seq_len, hid), t.reshape(seq_len, 1),
        pad2(w1, hid, hid), pad2(b1[None, :], 1, hid),
        pad2(w2, hid, hid), pad2(b2[None, :], 1, hid),
        pad2(w3, hid, hid), pad2(b3[None, :], 1, hid),
        pad2(sin_freq, 3, hid), pad2(w_out, hid, n_out), decay.reshape(1, n_out),
    )
    bwd_cols = jnp.asarray(is_bwd)
    return pl.pallas_call(
        _filter_taps_body,
        out_shape=(jax.ShapeDtypeStruct((seq_len, n_out), F32),
                   jax.ShapeDtypeStruct((1, n_out), F32)),
        grid=(seq_len // tl,),
        in_specs=[full(bwd_cols),
                  pl.BlockSpec((tl, hid), lambda i: (i, 0)),
                  pl.BlockSpec((tl, 1), lambda i: (i, 0))] + [full(a) for a in args[2:]],
        out_specs=(pl.BlockSpec((tl, n_out), lambda i: (i, 0)),
                   pl.BlockSpec((1, n_out), lambda i: (0, 0))),
        compiler_params=_cparams(("arbitrary",)),
        name="filter_taps",
    )(bwd_cols, *args)


def _mla_qkv_body(c_ref, qg_ref, kvg_ref, wq_ref, wqr_ref, wk_ref, wv_ref, cos_ref, sin_ref,
                  q_ref, k_ref, v_ref):
    c = c_ref[...]
    cos = cos_ref[...]
    sin = sin_ref[...]
    cq = _rms(c[:, :MLA_Q_LORA], qg_ref[...]).astype(BF16)
    ckv = _rms(c[:, MLA_Q_LORA:MLA_Q_LORA + MLA_KV_LORA], kvg_ref[...]).astype(BF16)
    kr0 = MLA_Q_LORA + MLA_KV_LORA
    k_rope = c[:, kr0:kr0 + LANES] * cos + c[:, kr0 + LANES:kr0 + 2 * LANES] * sin
    for h in range(MLA_HEADS):
        cols = slice(h * LANES, (h + 1) * LANES)
        q = _dot(cq, wq_ref[:, cols]) * cos + _dot(cq, wqr_ref[:, cols]) * sin
        q_ref[h] = q.astype(BF16)
        k_ref[h] = (_dot(ckv, wk_ref[:, cols]) + k_rope).astype(BF16)
    v_ref[...] = _dot(ckv, wv_ref[...]).astype(BF16)


def _mla_qkv(c, q_norm_g, kv_norm_g, wq, wq_rot, wk, wv, cos_t, sin_t, batch, *, tl=512):
    m, cw = c.shape
    seq_len = m // batch
    nl = seq_len // tl
    hv = MLA_HEADS * MLA_V
    full = lambda a: pl.BlockSpec(a.shape, lambda bi, i: (0, 0))
    qg = q_norm_g.reshape(1, -1)
    kvg = kv_norm_g.reshape(1, -1)
    head_out = jax.ShapeDtypeStruct((batch, MLA_HEADS, seq_len, LANES), BF16)
    head_spec = pl.BlockSpec((None, MLA_HEADS, tl, LANES), lambda bi, i: (bi, 0, i, 0))
    return pl.pallas_call(
        _mla_qkv_body,
        out_shape=(head_out, head_out, jax.ShapeDtypeStruct((batch, seq_len, hv), BF16)),
        grid=(batch, nl),
        in_specs=[pl.BlockSpec((tl, cw), lambda bi, i: (bi * nl + i, 0)),
                  full(qg), full(kvg), full(wq), full(wq_rot), full(wk), full(wv),
                  pl.BlockSpec((tl, LANES), lambda bi, i: (i, 0)),
                  pl.BlockSpec((tl, LANES), lambda bi, i: (i, 0))],
        out_specs=(head_spec, head_spec, pl.BlockSpec((None, tl, hv), lambda bi, i: (bi, i, 0))),
        compiler_params=_cparams(("parallel", "parallel")),
        name="mla_qkv",
    )(c, qg, kvg, wq, wq_rot, wk, wv, cos_t, sin_t)


def _attention_body(exp2_scale, q_ref, k_ref, v_ref, o_ref):
    pair = LANES // MLA_V
    for g in range(q_ref.shape[0] // pair):
        lanes = slice(g * LANES, (g + 1) * LANES)
        outs = []
        for h in range(g * pair, (g + 1) * pair):
            s = lax.dot_general(q_ref[h], k_ref[h], (((1,), (1,)), ((), ())),
                                preferred_element_type=F32)
            m = jnp.max(s, axis=-1, keepdims=True)
            p = jnp.exp2((s - m) * exp2_scale)
            denom = jnp.sum(p, axis=-1, keepdims=True)
            outs.append(_dot(p.astype(BF16), v_ref[:, lanes]) / denom)
        lane = lax.broadcasted_iota(jnp.int32, outs[0].shape, 1)
        o_ref[:, lanes] = jnp.where(lane < MLA_V, outs[0], outs[1]).astype(o_ref.dtype)


def _attention(q, k, v, *, tq=256, group=4):
    batch, heads, seq_len, _ = q.shape
    width = group * MLA_V
    exp2_scale = (MLA_NOPE + MLA_ROPE) ** -0.5 * math.log2(math.e)
    return pl.pallas_call(
        functools.partial(_attention_body, exp2_scale),
        out_shape=jax.ShapeDtypeStruct((batch, seq_len, heads * MLA_V), BF16),
        grid=(batch, heads // group, seq_len // tq),
        in_specs=[
            pl.BlockSpec((None, group, tq, LANES), lambda bi, p, i: (bi, p, i, 0)),
            pl.BlockSpec((None, group, seq_len, LANES), lambda bi, p, i: (bi, p, 0, 0)),
            pl.BlockSpec((None, seq_len, width), lambda bi, p, i: (bi, 0, p)),
        ],
        out_specs=pl.BlockSpec((None, tq, width), lambda bi, p, i: (bi, i, p)),
        compiler_params=_cparams(("parallel", "parallel", "parallel")),
        name="attention",
    )(q, k, v)


def _rope_rotation():
    half = MLA_ROPE // 2
    r = np.zeros((MLA_ROPE, MLA_ROPE), np.float32)
    for j in range(half):
        r[j + half, j] = -1.0
        r[j, j + half] = 1.0
    return jnp.asarray(r)


def _mla_weights(w_dq, w_uq, w_dkv, w_ukv):
    d = w_dq.shape[0]
    rot = _rope_rotation()
    qh = w_uq.reshape(MLA_Q_LORA, MLA_HEADS, MLA_NOPE + MLA_ROPE)
    zq = jnp.zeros((MLA_Q_LORA, MLA_HEADS, LANES - MLA_NOPE - MLA_ROPE), F32)
    wq = jnp.concatenate([qh, zq], axis=-1).reshape(MLA_Q_LORA, MLA_HEADS * LANES)
    q_rot = jnp.einsum("lhr,rs->lhs", qh[..., MLA_NOPE:], rot)
    wq_rot = jnp.concatenate([jnp.zeros((MLA_Q_LORA, MLA_HEADS, MLA_NOPE), F32), q_rot, zq],
                             axis=-1).reshape(MLA_Q_LORA, MLA_HEADS * LANES)
    kvh = w_ukv.reshape(MLA_KV_LORA, MLA_HEADS, MLA_NOPE + MLA_V)
    wk = jnp.concatenate([kvh[..., :MLA_NOPE],
                          jnp.zeros((MLA_KV_LORA, MLA_HEADS, LANES - MLA_NOPE), F32)],
                         axis=-1).reshape(MLA_KV_LORA, MLA_HEADS * LANES)
    wv = kvh[..., MLA_NOPE:].reshape(MLA_KV_LORA, MLA_HEADS * MLA_V)
    w_kr = w_dkv[:, MLA_KV_LORA:]

    def slab(w):
        return jnp.concatenate([jnp.zeros((d, MLA_NOPE), F32), w,
                                jnp.zeros((d, LANES - MLA_NOPE - MLA_ROPE), F32)], axis=-1)

    w_down = jnp.concatenate([w_dq, w_dkv[:, :MLA_KV_LORA], slab(w_kr), slab(w_kr @ rot)], axis=-1)
    return (w_down.astype(BF16), wq.astype(BF16), wq_rot.astype(BF16), wk.astype(BF16),
            wv.astype(BF16))


def _rope_tables(seq_len):
    inv_freq = ROPE_THETA ** (-jnp.arange(0, MLA_ROPE, 2, dtype=F32) / MLA_ROPE)
    ang = jnp.arange(seq_len, dtype=F32)[:, None] * inv_freq[None, :]
    cos = jnp.cos(ang)
    sin = jnp.sin(ang)
    tail = LANES - MLA_NOPE - MLA_ROPE
    cos_t = jnp.concatenate([jnp.ones((seq_len, MLA_NOPE), F32), cos, cos,
                             jnp.ones((seq_len, tail), F32)], axis=-1)
    sin_t = jnp.concatenate([jnp.zeros((seq_len, MLA_NOPE), F32), sin, sin,
                             jnp.zeros((seq_len, tail), F32)], axis=-1)
    return cos_t, sin_t


def _position_features(seq_len, pos_dim):
    t = jnp.linspace(0.0, 1.0, seq_len, dtype=F32)
    bands = (pos_dim - 1) // 2
    w = 2.0 * math.pi * jnp.arange(seq_len, dtype=F32) / seq_len
    f = jnp.linspace(1e-4, bands - 1, bands, dtype=F32)
    phase = w[:, None] * f[None, :]
    z = jnp.concatenate([t[:, None], jnp.cos(phase), -jnp.sin(phase)], axis=-1)
    return t, z


def kernel(x, norm_g, ffn_w_gate, ffn_w_up, ffn_w_down, mix_w_in, pool_w, pool_scale, hyena_conv_w, hyena_conv_b, hyena_ffn_w1, hyena_ffn_b1, hyena_ffn_w2, hyena_ffn_b2, hyena_ffn_w3, hyena_ffn_b3, hyena_sin_freq, hyena_ffn_w_out, hyena_decay, hyena_bias, mix_w_out, mla_w_dq, mla_q_norm_g, mla_w_uq, mla_w_dkv, mla_kv_norm_g, mla_w_ukv, mla_w_o, final_norm_g):
    batch, seq_len, d = x.shape
    depth = norm_g.shape[0]
    assert 2 * seq_len == DFT_N1 * DFT_N2
    hy_width = hyena_bias.shape[-1]
    t_pos, z_pos = _position_features(seq_len, hyena_ffn_w1.shape[1])
    cos_t, sin_t = _rope_tables(seq_len)

    x = x.reshape(batch * seq_len, d)
    for i in range(depth):
        j = i // 2
        x = _ffn(x, norm_g[i, 0], ffn_w_gate[i, 0].astype(BF16), ffn_w_up[i, 0].astype(BF16),
                 ffn_w_down[i, 0].astype(BF16))
        if i % 2 == 0:
            proj = _norm_proj(x, norm_g[i, 1], mix_w_in[j].astype(BF16))
            y_pool, u = _pool_conv(proj.reshape(batch, seq_len, -1), pool_w[j].astype(BF16),
                                   pool_scale[j], hyena_conv_w[j], hyena_conv_b[j])
            taps, sumsq = _filter_taps(z_pos, t_pos, hyena_ffn_w1[j], hyena_ffn_b1[j],
                                       hyena_ffn_w2[j], hyena_ffn_b2[j], hyena_ffn_w3[j],
                                       hyena_ffn_b3[j], hyena_sin_freq[j], hyena_ffn_w_out[j],
                                       hyena_decay[j])
            spec = _filter_spectrum(taps, sumsq, hy_width)
            y_hyena = _hyena_long_conv(u, spec, hyena_bias[j])
            x = _res_proj(x, [y_pool.reshape(batch * seq_len, -1),
                              y_hyena.reshape(batch * seq_len, -1)], mix_w_out[j].astype(BF16))
        else:
            w_down, wq, wq_rot, wk, wv = _mla_weights(mla_w_dq[j], mla_w_uq[j], mla_w_dkv[j],
                                                      mla_w_ukv[j])
            c = _norm_proj(x, norm_g[i, 1], w_down)
            q, k, v = _mla_qkv(c, mla_q_norm_g[j], mla_kv_norm_g[j], wq, wq_rot, wk, wv,
                               cos_t, sin_t, batch)
            o = _attention(q, k, v)
            x = _res_proj(x, [o.reshape(batch * seq_len, -1)], mla_w_o[j].astype(BF16))
        x = _ffn(x, norm_g[i, 2], ffn_w_gate[i, 1].astype(BF16), ffn_w_up[i, 1].astype(BF16),
                 ffn_w_down[i, 1].astype(BF16),
                 final_g=final_norm_g if i == depth - 1 else None)
    return x.reshape(batch, seq_len, d)
```

```python
import functools
import math

import numpy as np
import jax
import jax.numpy as jnp
from jax import lax
from jax.experimental import pallas as pl
from jax.experimental.pallas import tpu as pltpu

F32 = jnp.float32
BF16 = jnp.bfloat16

RMS_EPS = 1e-6
MACARON_WEIGHT = 0.5
POOL_WINDOWS = (2, 4, 8, 16)
HYENA_ORDER = 2
HYENA_SHORT_CONV = 3
MLA_HEADS = 16
MLA_Q_LORA = 256
MLA_KV_LORA = 128
MLA_NOPE = 64
MLA_ROPE = 32
MLA_V = 64
ROPE_THETA = 10000.0
SOFTMAX_EXP2_SCALE = (MLA_NOPE + MLA_ROPE) ** -0.5 * math.log2(math.e)

LANES = 128
SUBLANES = 8
VMEM_LIMIT = 56 * 2**20

DFT_N1 = 64
DFT_N2 = 128
DFT_K1 = DFT_N1 // 2 + 1
DFT_QP = 40
DFT_PITCH = DFT_N2 + SUBLANES
PLANE_ROWS_PER_DOT = 8
PLANE_UNROLL = 4
K1_UNROLL = 11
HALO = 16


def _cparams(semantics):
    return pltpu.CompilerParams(dimension_semantics=semantics, vmem_limit_bytes=VMEM_LIMIT)


def _rms(x, g):
    return x * lax.rsqrt(jnp.mean(x * x, axis=-1, keepdims=True) + RMS_EPS) * g


def _dot(a, b):
    return jnp.dot(a, b, preferred_element_type=F32)


def _ffn_body(n_mix, has_next, has_final, *refs):
    refs = list(refs)
    x_ref, g_ref, wg_ref, wu_ref, wd_ref = refs[:5]
    del refs[:5]
    mix_refs = [refs.pop(0) for _ in range(n_mix)]
    w_mix_ref = refs.pop(0) if n_mix else None
    next_g_ref, next_w_ref = (refs.pop(0), refs.pop(0)) if has_next else (None, None)
    final_g_ref = refs.pop(0) if has_final else None
    o_ref = refs.pop(0)

    x = x_ref[...]
    row = 0
    for a_ref in mix_refs:
        k = a_ref.shape[1]
        x = x + _dot(a_ref[...].astype(BF16), w_mix_ref[row:row + k, :])
        row += k
    h = _rms(x, g_ref[...]).astype(BF16)
    gate = _dot(h, wg_ref[...])
    up = _dot(h, wu_ref[...])
    act = gate / (1.0 + jnp.exp(-gate)) * up
    y = x + MACARON_WEIGHT * _dot(act.astype(BF16), wd_ref[...])
    if has_final:
        y = _rms(y, final_g_ref[...])
    o_ref[...] = y
    if has_next:
        refs.pop(0)[...] = _dot(_rms(y, next_g_ref[...]).astype(BF16), next_w_ref[...])


def _resident(shape):
    return pl.BlockSpec(shape, lambda *_: (0,) * len(shape), pipeline_mode=pl.Buffered(1))


def _ffn(x, g, wg, wu, wd, *, mix=(), w_mix=None, next_g=None, w_next=None, final_g=None,
         tm=512):
    m, d = x.shape
    row_block = lambda a: pl.BlockSpec((tm, a.shape[1]), lambda i: (i, 0))
    args = [x, g.reshape(1, d), wg, wu, wd]
    in_specs = [row_block(x), _resident((1, d)), _resident(wg.shape), _resident(wu.shape),
                _resident(wd.shape)]
    for a in mix:
        args.append(a)
        in_specs.append(row_block(a))
    if mix:
        args.append(w_mix)
        in_specs.append(_resident(w_mix.shape))
    if w_next is not None:
        args += [next_g.reshape(1, d), w_next]
        in_specs += [_resident((1, d)), _resident(w_next.shape)]
    if final_g is not None:
        args.append(final_g.reshape(1, d))
        in_specs.append(_resident((1, d)))
    out_shape = [jax.ShapeDtypeStruct((m, d), F32)]
    out_specs = [pl.BlockSpec((tm, d), lambda i: (i, 0))]
    if w_next is not None:
        out_shape.append(jax.ShapeDtypeStruct((m, w_next.shape[1]), F32))
        out_specs.append(pl.BlockSpec((tm, w_next.shape[1]), lambda i: (i, 0)))
    outs = pl.pallas_call(
        functools.partial(_ffn_body, len(mix), w_next is not None, final_g is not None),
        out_shape=out_shape,
        grid=(m // tm,),
        in_specs=in_specs,
        out_specs=out_specs,
        compiler_params=_cparams(("parallel",)),
        name="ffn",
    )(*args)
    return outs if w_next is not None else outs[0]


def _pool_conv_body(seq_len, cur_ref, prev_ref, next_ref, pw_ref, ps_ref, cw_ref, cb_ref,
                    yp_ref, u_ref, ext_ref):
    i = pl.program_id(1)
    tl = cur_ref.shape[0]
    pool_width = yp_ref.shape[1]
    group = pool_width // len(POOL_WINDOWS)
    ext_ref[0:HALO, :] = jnp.where(i > 0, prev_ref[...], 0.0)
    ext_ref[HALO:HALO + tl, :] = cur_ref[...]
    ext_ref[HALO + tl:, :] = jnp.where(i < pl.num_programs(1) - 1, next_ref[...], 0.0)

    t = i * tl + lax.broadcasted_iota(jnp.int32, (tl, group), 0)
    for g, w in enumerate(POOL_WINDOWS):
        c0 = g * group
        lo = jnp.clip(t - w // 2, 0, seq_len)
        hi = jnp.clip(t - w // 2 + w, 0, seq_len)
        cnt = (hi - lo).astype(F32)
        s = ext_ref[HALO - w // 2:HALO - w // 2 + tl, c0:c0 + group]
        for d in range(1 - w // 2, w - w // 2):
            s = s + ext_ref[HALO + d:HALO + d + tl, c0:c0 + group]
        p = s / cnt - cur_ref[:, c0:c0 + group]
        y = _dot(p.astype(BF16), pw_ref[g])
        yp_ref[:, c0:c0 + group] = y * ps_ref[:, c0:c0 + group]

    pad = HYENA_SHORT_CONV // 2
    u = cb_ref[...]
    for k in range(HYENA_SHORT_CONV):
        u = u + ext_ref[HALO + k - pad:HALO + k - pad + tl, pool_width:] * cw_ref[k:k + 1, :]
    u_ref[...] = u


def _pool_conv(proj, pool_w, pool_scale, conv_w, conv_b, *, tl=256):
    b, seq_len, width = proj.shape
    pool_width = pool_scale.shape[0]
    hy_width = width - pool_width
    hb = tl // HALO
    last = seq_len // HALO - 1
    return pl.pallas_call(
        functools.partial(_pool_conv_body, seq_len),
        out_shape=(jax.ShapeDtypeStruct((b, seq_len, pool_width), F32),
                   jax.ShapeDtypeStruct((b, seq_len, hy_width), F32)),
        grid=(b, seq_len // tl),
        in_specs=[
            pl.BlockSpec((None, tl, width), lambda bi, i: (bi, i, 0)),
            pl.BlockSpec((None, HALO, width), lambda bi, i: (bi, jnp.maximum(i * hb - 1, 0), 0)),
            pl.BlockSpec((None, HALO, width), lambda bi, i: (bi, jnp.minimum((i + 1) * hb, last), 0)),
            pl.BlockSpec(pool_w.shape, lambda bi, i: (0, 0, 0)),
            pl.BlockSpec((1, pool_width), lambda bi, i: (0, 0)),
            pl.BlockSpec(conv_w.shape, lambda bi, i: (0, 0)),
            pl.BlockSpec((1, hy_width), lambda bi, i: (0, 0)),
        ],
        out_specs=(pl.BlockSpec((None, tl, pool_width), lambda bi, i: (bi, i, 0)),
                   pl.BlockSpec((None, tl, hy_width), lambda bi, i: (bi, i, 0))),
        scratch_shapes=[pltpu.VMEM((tl + 2 * HALO, width), F32)],
        compiler_params=_cparams(("parallel", "parallel")),
        name="pool_conv",
    )(proj, proj, proj, pool_w, pool_scale.reshape(1, pool_width), conv_w,
      conv_b.reshape(1, hy_width))


@functools.lru_cache(maxsize=None)
def _dft_constants():
    n = DFT_N1 * DFT_N2
    half = DFT_N1 // 2
    k1 = np.arange(DFT_K1)
    n1 = np.arange(half)
    ang = 2.0 * np.pi * np.outer(k1, n1) / DFT_N1
    f_fwd = np.zeros((2 * DFT_QP, half))
    f_fwd[:DFT_K1] = np.cos(ang)
    f_fwd[DFT_QP:DFT_QP + DFT_K1] = -np.sin(ang)
    n2 = np.arange(DFT_N2)
    k2 = np.arange(DFT_N2)
    g = np.zeros((DFT_K1, 2 * DFT_N2, 2 * DFT_N2))
    for a in range(DFT_K1):
        ph = 2.0 * np.pi * np.outer(a + DFT_N1 * k2, n2) / n
        gre, gim = np.cos(ph), -np.sin(ph)
        g[a] = np.block([[gre, -gim], [gim, gre]])
    weight = np.full(DFT_K1, 2.0)
    weight[0] = 1.0
    weight[-1] = 1.0
    f_inv = np.zeros((half, 2 * DFT_QP))
    f_inv[:, :DFT_K1] = weight * np.cos(ang.T) / n
    f_inv[:, DFT_QP:DFT_QP + DFT_K1] = -weight * np.sin(ang.T) / n
    return (f_fwd.astype(np.float32), g.astype(np.float32),
            np.transpose(g, (0, 2, 1)).astype(np.float32), f_inv.astype(np.float32))


def _pad_rows(dst_ref, src):
    for n1 in range(DFT_N1 // 2):
        dst_ref[n1 * DFT_PITCH:n1 * DFT_PITCH + DFT_N2, :] = src[n1 * DFT_N2:(n1 + 1) * DFT_N2, :]


def _contract_planes(src_ref, dst_ref, f_ref):
    n_dst, n_src = f_ref.shape

    def step(j, carry):
        r0 = j * PLANE_ROWS_PER_DOT
        x = jnp.concatenate(
            [src_ref[pl.ds(r0 + r, n_src, stride=DFT_PITCH), :].astype(BF16)
             for r in range(PLANE_ROWS_PER_DOT)], axis=1)
        y = _dot(f_ref[...], x)
        for r in range(PLANE_ROWS_PER_DOT):
            dst_ref[pl.ds(r0 + r, n_dst, stride=DFT_PITCH), :] = y[:, r * LANES:(r + 1) * LANES]
        return carry

    lax.fori_loop(0, DFT_N2 // PLANE_ROWS_PER_DOT, step, 0, unroll=PLANE_UNROLL)


def _load_planes(a_ref, k1):
    re = a_ref[pl.ds(pl.multiple_of(k1 * DFT_PITCH, SUBLANES), DFT_N2), :]
    im = a_ref[pl.ds(pl.multiple_of((DFT_QP + k1) * DFT_PITCH, SUBLANES), DFT_N2), :]
    return jnp.concatenate([re, im], axis=0)


def _store_planes(a_ref, k1, z):
    a_ref[pl.ds(pl.multiple_of(k1 * DFT_PITCH, SUBLANES), DFT_N2), :] = z[:DFT_N2]
    a_ref[pl.ds(pl.multiple_of((DFT_QP + k1) * DFT_PITCH, SUBLANES), DFT_N2), :] = z[DFT_N2:]


def _spectrum_body(fw_ref, bw_ref, ssf_ref, ssb_ref, ff_ref, g_ref, h_ref, u_ref, a_ref):
    norm = lax.rsqrt(ssf_ref[...] + ssb_ref[...])
    for part, src_ref in enumerate((fw_ref, bw_ref)):
        _pad_rows(u_ref, src_ref)
        _contract_planes(u_ref, a_ref, ff_ref)
        sign = 1.0 if part == 0 else -1.0

        def step(k1, carry):
            xh = _dot(g_ref[k1], _load_planes(a_ref, k1).astype(BF16))
            re = xh[:DFT_N2] * norm
            im = xh[DFT_N2:] * (sign * norm)
            if part == 0:
                h_ref[k1, :DFT_N2, :] = re
                h_ref[k1, DFT_N2:, :] = im
            else:
                h_ref[k1, :DFT_N2, :] += re
                h_ref[k1, DFT_N2:, :] += im
            return carry

        lax.fori_loop(0, DFT_K1, step, 0, unroll=K1_UNROLL)


def _filter_spectrum(taps, sumsq, width):
    seq_len = taps.shape[0]
    ff, g, _, _ = _dft_constants()
    cb = width // LANES
    half = DFT_N1 // 2
    return pl.pallas_call(
        _spectrum_body,
        out_shape=jax.ShapeDtypeStruct((HYENA_ORDER, DFT_K1, 2 * DFT_N2, width), F32),
        grid=(HYENA_ORDER, cb),
        in_specs=[
            pl.BlockSpec((seq_len, LANES), lambda o, c: (0, (2 * o) * cb + c)),
            pl.BlockSpec((seq_len, LANES), lambda o, c: (0, (2 * o + 1) * cb + c)),
            pl.BlockSpec((1, LANES), lambda o, c: (0, (2 * o) * cb + c)),
            pl.BlockSpec((1, LANES), lambda o, c: (0, (2 * o + 1) * cb + c)),
            pl.BlockSpec((2 * DFT_QP, half), lambda o, c: (0, 0)),
            pl.BlockSpec((DFT_K1, 2 * DFT_N2, 2 * DFT_N2), lambda o, c: (0, 0, 0)),
        ],
        out_specs=pl.BlockSpec((None, DFT_K1, 2 * DFT_N2, LANES), lambda o, c: (o, 0, 0, c)),
        scratch_shapes=[pltpu.VMEM((half * DFT_PITCH, LANES), F32),
                        pltpu.VMEM((2 * DFT_QP * DFT_PITCH, LANES), F32)],
        compiler_params=_cparams(("parallel", "parallel")),
        name="filter_spectrum",
    )(taps, taps, sumsq, sumsq, jnp.asarray(ff).astype(BF16), jnp.asarray(g).astype(BF16))


def _hyena_body(v_ref, gate_ref, h_ref, bias_ref, ff_ref, g_ref, gt_ref, fi_ref, o_ref,
                u_ref, y_ref, a_ref):
    order = pl.program_id(2)
    half = DFT_N1 // 2

    @pl.when(order == 0)
    def _():
        _pad_rows(u_ref, v_ref)

    _contract_planes(u_ref, a_ref, ff_ref)

    def freq_step(k1, carry):
        xh = _dot(g_ref[k1], _load_planes(a_ref, k1).astype(BF16))
        xre, xim = xh[:DFT_N2], xh[DFT_N2:]
        hre, him = h_ref[k1, :DFT_N2, :], h_ref[k1, DFT_N2:, :]
        yh = jnp.concatenate([xre * hre - xim * him, xre * him + xim * hre], axis=0)
        _store_planes(a_ref, k1, _dot(gt_ref[k1], yh.astype(BF16)))
        return carry

    lax.fori_loop(0, DFT_K1, freq_step, 0, unroll=K1_UNROLL)
    _contract_planes(a_ref, y_ref, fi_ref)

    bias = bias_ref[...]
    for n1 in range(half):
        rows = slice(n1 * DFT_PITCH, n1 * DFT_PITCH + DFT_N2)
        z = gate_ref[n1 * DFT_N2:(n1 + 1) * DFT_N2, :] * (y_ref[rows, :] + u_ref[rows, :] * bias)
        u_ref[rows, :] = z
        o_ref[n1 * DFT_N2:(n1 + 1) * DFT_N2, :] = z


def _hyena_long_conv(u, spec, bias):
    b, seq_len, _ = u.shape
    width = bias.shape[1]
    cb = width // LANES
    half = DFT_N1 // 2
    ff, g, gt, fi = _dft_constants()
    const2 = lambda c, bi, o: (0, 0)
    const3 = lambda c, bi, o: (0, 0, 0)
    return pl.pallas_call(
        _hyena_body,
        out_shape=jax.ShapeDtypeStruct((b, seq_len, width), F32),
        grid=(cb, b, HYENA_ORDER),
        in_specs=[
            pl.BlockSpec((None, seq_len, LANES), lambda c, bi, o: (bi, 0, HYENA_ORDER * cb + c)),
            pl.BlockSpec((None, seq_len, LANES), lambda c, bi, o: (bi, 0, o * cb + c)),
            pl.BlockSpec((None, DFT_K1, 2 * DFT_N2, LANES), lambda c, bi, o: (o, 0, 0, c)),
            pl.BlockSpec((None, 1, LANES), lambda c, bi, o: (o, 0, c)),
            pl.BlockSpec((2 * DFT_QP, half), const2),
            pl.BlockSpec((DFT_K1, 2 * DFT_N2, 2 * DFT_N2), const3),
            pl.BlockSpec((DFT_K1, 2 * DFT_N2, 2 * DFT_N2), const3),
            pl.BlockSpec((half, 2 * DFT_QP), const2),
        ],
        out_specs=pl.BlockSpec((None, seq_len, LANES), lambda c, bi, o: (bi, 0, c)),
        scratch_shapes=[pltpu.VMEM((half * DFT_PITCH, LANES), F32),
                        pltpu.VMEM((half * DFT_PITCH, LANES), F32),
                        pltpu.VMEM((2 * DFT_QP * DFT_PITCH, LANES), F32)],
        compiler_params=_cparams(("parallel", "parallel", "arbitrary")),
        name="hyena_long_conv",
    )(u, u, spec, bias.reshape(HYENA_ORDER, 1, width), jnp.asarray(ff).astype(BF16),
      jnp.asarray(g).astype(BF16), jnp.asarray(gt).astype(BF16), jnp.asarray(fi).astype(BF16))


def _hdot(a, b):
    return jnp.dot(a, b, preferred_element_type=F32, precision=lax.Precision.HIGHEST)


def _filter_taps_body(bwd_cols, z_ref, t_ref, w1_ref, b1_ref, w2_ref, b2_ref, w3_ref, b3_ref,
                      sf_ref, wo_ref, decay_ref, taps_ref, ss_ref):
    i = pl.program_id(0)
    h = jnp.sin(sf_ref[0:1, :] * (_hdot(z_ref[...], w1_ref[...]) + b1_ref[...]))
    h = jnp.sin(sf_ref[1:2, :] * (_hdot(h, w2_ref[...]) + b2_ref[...]))
    h = jnp.sin(sf_ref[2:3, :] * (_hdot(h, w3_ref[...]) + b3_ref[...]))
    h = _hdot(h, wo_ref[...])
    h = h * jnp.exp(-t_ref[...] * jnp.abs(decay_ref[...]))
    row = i * h.shape[0] + lax.broadcasted_iota(jnp.int32, h.shape, 0)
    h = jnp.where((row == 0) & (bwd_cols[...] > 0.0), 0.0, h)
    taps_ref[...] = h

    @pl.when(i == 0)
    def _():
        ss_ref[...] = jnp.zeros_like(ss_ref)

    ss_ref[...] += jnp.sum(h * h, axis=0, keepdims=True)


def _filter_taps(z, t, w1, b1, w2, b2, w3, b3, sin_freq, w_out, decay, *, tl=512):
    seq_len = z.shape[0]
    hid = LANES
    n_out = w_out.shape[1]
    width = n_out // (2 * HYENA_ORDER)

    def pad2(a, rows, cols):
        return jnp.pad(a, ((0, rows - a.shape[0]), (0, cols - a.shape[1])))

    is_bwd = np.tile(np.repeat(np.array([0.0, 1.0], np.float32), width), HYENA_ORDER)[None, :]
    full = lambda a: pl.BlockSpec(a.shape, lambda i: (0, 0))
    args = (
        pad2(z, seq_len, hid), t.reshape(seq_len, 1),
        pad2(w1, hid, hid), pad2(b1[None, :], 1, hid),
        pad2(w2, hid, hid), pad2(b2[None, :], 1, hid),
        pad2(w3, hid, hid), pad2(b3[None, :], 1, hid),
        pad2(sin_freq, 3, hid), pad2(w_out, hid, n_out), decay.reshape(1, n_out),
    )
    bwd_cols = jnp.asarray(is_bwd)
    return pl.pallas_call(
        _filter_taps_body,
        out_shape=(jax.ShapeDtypeStruct((seq_len, n_out), F32),
                   jax.ShapeDtypeStruct((1, n_out), F32)),
        grid=(seq_len // tl,),
        in_specs=[full(bwd_cols),
                  pl.BlockSpec((tl, hid), lambda i: (i, 0)),
                  pl.BlockSpec((tl, 1), lambda i: (i, 0))] + [full(a) for a in args[2:]],
        out_specs=(pl.BlockSpec((tl, n_out), lambda i: (i, 0)),
                   pl.BlockSpec((1, n_out), lambda i: (0, 0))),
        compiler_params=_cparams(("arbitrary",)),
        name="filter_taps",
    )(bwd_cols, *args)


def _mla_qkv_body(c_ref, qg_ref, kvg_ref, wq_ref, wqr_ref, wk_ref, wv_ref, cos_ref, sin_ref,
                  q_ref, k_ref, v_ref):
    c = c_ref[...]
    cos = cos_ref[...]
    sin = sin_ref[...]
    cq = _rms(c[:, :MLA_Q_LORA], qg_ref[...]).astype(BF16)
    ckv = _rms(c[:, MLA_Q_LORA:MLA_Q_LORA + MLA_KV_LORA], kvg_ref[...]).astype(BF16)
    kr0 = MLA_Q_LORA + MLA_KV_LORA
    k_rope = c[:, kr0:kr0 + LANES] * cos + c[:, kr0 + LANES:kr0 + 2 * LANES] * sin
    ones_lane = (lax.broadcasted_iota(jnp.int32, (1, LANES), 1) == MLA_V).astype(F32)
    for h in range(MLA_HEADS):
        cols = slice(h * LANES, (h + 1) * LANES)
        q = _dot(cq, wq_ref[:, cols]) * cos + _dot(cq, wqr_ref[:, cols]) * sin
        q_ref[h] = (q * SOFTMAX_EXP2_SCALE).astype(BF16)
        k_ref[h] = (_dot(ckv, wk_ref[:, cols]) + k_rope).astype(BF16)
        v_ref[h] = (_dot(ckv, wv_ref[:, cols]) + ones_lane).astype(BF16)


def _mla_qkv(c, q_norm_g, kv_norm_g, wq, wq_rot, wk, wv, cos_t, sin_t, batch, *, tl=512):
    m, cw = c.shape
    seq_len = m // batch
    nl = seq_len // tl
    full = lambda a: pl.BlockSpec(a.shape, lambda bi, i: (0, 0))
    qg = q_norm_g.reshape(1, -1)
    kvg = kv_norm_g.reshape(1, -1)
    head_out = jax.ShapeDtypeStruct((batch, MLA_HEADS, seq_len, LANES), BF16)
    head_spec = pl.BlockSpec((None, MLA_HEADS, tl, LANES), lambda bi, i: (bi, 0, i, 0))
    return pl.pallas_call(
        _mla_qkv_body,
        out_shape=(head_out, head_out, head_out),
        grid=(batch, nl),
        in_specs=[pl.BlockSpec((tl, cw), lambda bi, i: (bi * nl + i, 0)),
                  full(qg), full(kvg), full(wq), full(wq_rot), full(wk), full(wv),
                  pl.BlockSpec((tl, LANES), lambda bi, i: (i, 0)),
                  pl.BlockSpec((tl, LANES), lambda bi, i: (i, 0))],
        out_specs=(head_spec, head_spec, head_spec),
        compiler_params=_cparams(("parallel", "parallel")),
        name="mla_qkv",
    )(c, qg, kvg, wq, wq_rot, wk, wv, cos_t, sin_t)


def _attention_body(q_ref, k_ref, v_ref, o_ref):
    pair = LANES // MLA_V
    for g in range(q_ref.shape[0] // pair):
        outs = []
        for h in range(g * pair, (g + 1) * pair):
            s = lax.dot_general(q_ref[h], k_ref[h], (((1,), (1,)), ((), ())),
                                preferred_element_type=F32)
            p = jnp.exp2(s - jnp.max(s, axis=-1, keepdims=True))
            r = _dot(p.astype(BF16), v_ref[h])
            outs.append(r / r[:, MLA_V:MLA_V + 1])
        lane = lax.broadcasted_iota(jnp.int32, outs[0].shape, 1)
        both = jnp.where(lane < MLA_V, outs[0], pltpu.roll(outs[1], MLA_V, axis=1))
        o_ref[:, g * LANES:(g + 1) * LANES] = both.astype(o_ref.dtype)


def _attention(q, k, v, *, tq=256, group=8):
    batch, heads, seq_len, _ = q.shape
    width = group * MLA_V
    return pl.pallas_call(
        _attention_body,
        out_shape=jax.ShapeDtypeStruct((batch, seq_len, heads * MLA_V), BF16),
        grid=(batch, heads // group, seq_len // tq),
        in_specs=[
            pl.BlockSpec((None, group, tq, LANES), lambda bi, p, i: (bi, p, i, 0)),
            pl.BlockSpec((None, group, seq_len, LANES), lambda bi, p, i: (bi, p, 0, 0)),
            pl.BlockSpec((None, group, seq_len, LANES), lambda bi, p, i: (bi, p, 0, 0)),
        ],
        out_specs=pl.BlockSpec((None, tq, width), lambda bi, p, i: (bi, i, p)),
        compiler_params=_cparams(("parallel", "parallel", "parallel")),
        name="attention",
    )(q, k, v)


def _rope_rotation():
    half = MLA_ROPE // 2
    r = np.zeros((MLA_ROPE, MLA_ROPE), np.float32)
    for j in range(half):
        r[j + half, j] = -1.0
        r[j, j + half] = 1.0
    return jnp.asarray(r)


def _mla_weights(w_dq, w_uq, w_dkv, w_ukv):
    d = w_dq.shape[0]
    rot = _rope_rotation()
    qh = w_uq.reshape(MLA_Q_LORA, MLA_HEADS, MLA_NOPE + MLA_ROPE)
    zq = jnp.zeros((MLA_Q_LORA, MLA_HEADS, LANES - MLA_NOPE - MLA_ROPE), F32)
    wq = jnp.concatenate([qh, zq], axis=-1).reshape(MLA_Q_LORA, MLA_HEADS * LANES)
    q_rot = jnp.einsum("lhr,rs->lhs", qh[..., MLA_NOPE:], rot)
    wq_rot = jnp.concatenate([jnp.zeros((MLA_Q_LORA, MLA_HEADS, MLA_NOPE), F32), q_rot, zq],
                             axis=-1).reshape(MLA_Q_LORA, MLA_HEADS * LANES)
    kvh = w_ukv.reshape(MLA_KV_LORA, MLA_HEADS, MLA_NOPE + MLA_V)
    wk = jnp.concatenate([kvh[..., :MLA_NOPE],
                          jnp.zeros((MLA_KV_LORA, MLA_HEADS, LANES - MLA_NOPE), F32)],
                         axis=-1).reshape(MLA_KV_LORA, MLA_HEADS * LANES)
    wv = jnp.concatenate([kvh[..., MLA_NOPE:],
                          jnp.zeros((MLA_KV_LORA, MLA_HEADS, LANES - MLA_V), F32)],
                         axis=-1).reshape(MLA_KV_LORA, MLA_HEADS * LANES)
    w_kr = w_dkv[:, MLA_KV_LORA:]

    def slab(w):
        return jnp.concatenate([jnp.zeros((d, MLA_NOPE), F32), w,
                                jnp.zeros((d, LANES - MLA_NOPE - MLA_ROPE), F32)], axis=-1)

    w_down = jnp.concatenate([w_dq, w_dkv[:, :MLA_KV_LORA], slab(w_kr), slab(w_kr @ rot)], axis=-1)
    return (w_down.astype(BF16), wq.astype(BF16), wq_rot.astype(BF16), wk.astype(BF16),
            wv.astype(BF16))


def _rope_tables(seq_len):
    inv_freq = ROPE_THETA ** (-jnp.arange(0, MLA_ROPE, 2, dtype=F32) / MLA_ROPE)
    ang = jnp.arange(seq_len, dtype=F32)[:, None] * inv_freq[None, :]
    cos = jnp.cos(ang)
    sin = jnp.sin(ang)
    tail = LANES - MLA_NOPE - MLA_ROPE
    cos_t = jnp.concatenate([jnp.ones((seq_len, MLA_NOPE), F32), cos, cos,
                             jnp.ones((seq_len, tail), F32)], axis=-1)
    sin_t = jnp.concatenate([jnp.zeros((seq_len, MLA_NOPE), F32), sin, sin,
                             jnp.zeros((seq_len, tail), F32)], axis=-1)
    return cos_t, sin_t


def _position_features(seq_len, pos_dim):
    t = jnp.linspace(0.0, 1.0, seq_len, dtype=F32)
    bands = (pos_dim - 1) // 2
    w = 2.0 * math.pi * jnp.arange(seq_len, dtype=F32) / seq_len
    f = jnp.linspace(1e-4, bands - 1, bands, dtype=F32)
    phase = w[:, None] * f[None, :]
    z = jnp.concatenate([t[:, None], jnp.cos(phase), -jnp.sin(phase)], axis=-1)
    return t, z


def kernel(x, norm_g, ffn_w_gate, ffn_w_up, ffn_w_down, mix_w_in, pool_w, pool_scale, hyena_conv_w, hyena_conv_b, hyena_ffn_w1, hyena_ffn_b1, hyena_ffn_w2, hyena_ffn_b2, hyena_ffn_w3, hyena_ffn_b3, hyena_sin_freq, hyena_ffn_w_out, hyena_decay, hyena_bias, mix_w_out, mla_w_dq, mla_q_norm_g, mla_w_uq, mla_w_dkv, mla_kv_norm_g, mla_w_ukv, mla_w_o, final_norm_g):
    batch, seq_len, d = x.shape
    depth = norm_g.shape[0]
    assert 2 * seq_len == DFT_N1 * DFT_N2
    hy_width = hyena_bias.shape[-1]
    t_pos, z_pos = _position_features(seq_len, hyena_ffn_w1.shape[1])
    cos_t, sin_t = _rope_tables(seq_len)

    x = x.reshape(batch * seq_len, d)
    for i in range(depth):
        j = i // 2
        ffn_w = [[w[i, s].astype(BF16) for w in (ffn_w_gate, ffn_w_up, ffn_w_down)]
                 for s in range(2)]
        if i % 2 == 0:
            x, proj = _ffn(x, norm_g[i, 0], *ffn_w[0], next_g=norm_g[i, 1],
                           w_next=mix_w_in[j].astype(BF16))
            y_pool, u = _pool_conv(proj.reshape(batch, seq_len, -1), pool_w[j].astype(BF16),
                                   pool_scale[j], hyena_conv_w[j], hyena_conv_b[j])
            taps, sumsq = _filter_taps(z_pos, t_pos, hyena_ffn_w1[j], hyena_ffn_b1[j],
                                       hyena_ffn_w2[j], hyena_ffn_b2[j], hyena_ffn_w3[j],
                                       hyena_ffn_b3[j], hyena_sin_freq[j], hyena_ffn_w_out[j],
                                       hyena_decay[j])
            spec = _filter_spectrum(taps, sumsq, hy_width)
            y_hyena = _hyena_long_conv(u, spec, hyena_bias[j])
            mix = (y_pool.reshape(batch * seq_len, -1), y_hyena.reshape(batch * seq_len, -1))
            w_mix = mix_w_out[j].astype(BF16)
        else:
            w_down, wq, wq_rot, wk, wv = _mla_weights(mla_w_dq[j], mla_w_uq[j], mla_w_dkv[j],
                                                      mla_w_ukv[j])
            x, c = _ffn(x, norm_g[i, 0], *ffn_w[0], next_g=norm_g[i, 1], w_next=w_down)
            q, k, v = _mla_qkv(c, mla_q_norm_g[j], mla_kv_norm_g[j], wq, wq_rot, wk, wv,
                               cos_t, sin_t, batch)
            mix = (_attention(q, k, v).reshape(batch * seq_len, -1),)
            w_mix = mla_w_o[j].astype(BF16)
        x = _ffn(x, norm_g[i, 2], *ffn_w[1], mix=mix, w_mix=w_mix,
                 final_g=final_norm_g if i == depth - 1 else None)
    return x.reshape(batch, seq_len, d)
```

```python
import functools
import math

import numpy as np
import jax
import jax.numpy as jnp
from jax import lax
from jax.experimental import pallas as pl
from jax.experimental.pallas import tpu as pltpu

F32 = jnp.float32
BF16 = jnp.bfloat16

RMS_EPS = 1e-6
MACARON_WEIGHT = 0.5
POOL_WINDOWS = (2, 4, 8, 16)
HYENA_ORDER = 2
HYENA_SHORT_CONV = 3
MLA_HEADS = 16
MLA_Q_LORA = 256
MLA_KV_LORA = 128
MLA_NOPE = 64
MLA_ROPE = 32
MLA_V = 64
ROPE_THETA = 10000.0
SOFTMAX_EXP2_SCALE = (MLA_NOPE + MLA_ROPE) ** -0.5 * math.log2(math.e)

LANES = 128
SUBLANES = 8
VMEM_LIMIT = 56 * 2**20

DFT_N1 = 64
DFT_N2 = 128
DFT_K1 = DFT_N1 // 2 + 1
DFT_QP = 40
DFT_PITCH = DFT_N2 + SUBLANES
PLANE_ROWS_PER_DOT = 8
PLANE_UNROLL = 4
K1_UNROLL = 11
HALO = 16


def _cparams(semantics):
    return pltpu.CompilerParams(dimension_semantics=semantics, vmem_limit_bytes=VMEM_LIMIT)


def _rms(x, g):
    return x * lax.rsqrt(jnp.mean(x * x, axis=-1, keepdims=True) + RMS_EPS) * g


def _dot(a, b):
    return jnp.dot(a, b, preferred_element_type=F32)


def _ffn_body(n_mix, has_next, has_final, *refs):
    refs = list(refs)
    x_ref, g_ref, wg_ref, wu_ref, wd_ref = refs[:5]
    del refs[:5]
    mix_refs = [refs.pop(0) for _ in range(n_mix)]
    w_mix_ref = refs.pop(0) if n_mix else None
    next_g_ref, next_w_ref = (refs.pop(0), refs.pop(0)) if has_next else (None, None)
    final_g_ref = refs.pop(0) if has_final else None
    o_ref = refs.pop(0)

    x = x_ref[...]
    row = 0
    for a_ref in mix_refs:
        k = a_ref.shape[1]
        x = x + _dot(a_ref[...].astype(BF16), w_mix_ref[row:row + k, :])
        row += k
    h = _rms(x, g_ref[...]).astype(BF16)
    gate = _dot(h, wg_ref[...])
    up = _dot(h, wu_ref[...])
    act = gate / (1.0 + jnp.exp(-gate)) * up
    y = x + MACARON_WEIGHT * _dot(act.astype(BF16), wd_ref[...])
    if has_final:
        y = _rms(y, final_g_ref[...])
    o_ref[...] = y
    if has_next:
        refs.pop(0)[...] = _dot(_rms(y, next_g_ref[...]).astype(BF16), next_w_ref[...])


def _resident(shape):
    return pl.BlockSpec(shape, lambda *_: (0,) * len(shape), pipeline_mode=pl.Buffered(1))


def _ffn(x, g, wg, wu, wd, *, mix=(), w_mix=None, next_g=None, w_next=None, final_g=None,
         tm=512):
    m, d = x.shape
    row_block = lambda a: pl.BlockSpec((tm, a.shape[1]), lambda i: (i, 0))
    args = [x, g.reshape(1, d), wg, wu, wd]
    in_specs = [row_block(x), _resident((1, d)), _resident(wg.shape), _resident(wu.shape),
                _resident(wd.shape)]
    for a in mix:
        args.append(a)
        in_specs.append(row_block(a))
    if mix:
        args.append(w_mix)
        in_specs.append(_resident(w_mix.shape))
    if w_next is not None:
        args += [next_g.reshape(1, d), w_next]
        in_specs += [_resident((1, d)), _resident(w_next.shape)]
    if final_g is not None:
        args.append(final_g.reshape(1, d))
        in_specs.append(_resident((1, d)))
    out_shape = [jax.ShapeDtypeStruct((m, d), F32)]
    out_specs = [pl.BlockSpec((tm, d), lambda i: (i, 0))]
    if w_next is not None:
        out_shape.append(jax.ShapeDtypeStruct((m, w_next.shape[1]), F32))
        out_specs.append(pl.BlockSpec((tm, w_next.shape[1]), lambda i: (i, 0)))
    outs = pl.pallas_call(
        functools.partial(_ffn_body, len(mix), w_next is not None, final_g is not None),
        out_shape=out_shape,
        grid=(m // tm,),
        in_specs=in_specs,
        out_specs=out_specs,
        compiler_params=_cparams(("parallel",)),
        name="ffn",
    )(*args)
    return outs if w_next is not None else outs[0]


def _pool_body(seq_len, cur_ref, prev_ref, next_ref, pw_ref, ps_ref, yp_ref, ext_ref):
    i = pl.program_id(1)
    tl = cur_ref.shape[0]
    pool_width = yp_ref.shape[1]
    group = pool_width // len(POOL_WINDOWS)
    ext_ref[0:HALO, :] = jnp.where(i > 0, prev_ref[...], 0.0)
    ext_ref[HALO:HALO + tl, :] = cur_ref[...]
    ext_ref[HALO + tl:, :] = jnp.where(i < pl.num_programs(1) - 1, next_ref[...], 0.0)

    t = i * tl + lax.broadcasted_iota(jnp.int32, (tl, group), 0)
    for g, w in enumerate(POOL_WINDOWS):
        c0 = g * group
        lo = jnp.clip(t - w // 2, 0, seq_len)
        hi = jnp.clip(t - w // 2 + w, 0, seq_len)
        cnt = (hi - lo).astype(F32)
        s = ext_ref[HALO - w // 2:HALO - w // 2 + tl, c0:c0 + group]
        for d in range(1 - w // 2, w - w // 2):
            s = s + ext_ref[HALO + d:HALO + d + tl, c0:c0 + group]
        p = s / cnt - cur_ref[:, c0:c0 + group]
        y = _dot(p.astype(BF16), pw_ref[g])
        yp_ref[:, c0:c0 + group] = y * ps_ref[:, c0:c0 + group]


def _pool_mixer(proj, pool_w, pool_scale, *, tl=512):
    b, seq_len, _ = proj.shape
    pool_width = pool_scale.shape[0]
    hb = tl // HALO
    last = seq_len // HALO - 1
    return pl.pallas_call(
        functools.partial(_pool_body, seq_len),
        out_shape=jax.ShapeDtypeStruct((b, seq_len, pool_width), F32),
        grid=(b, seq_len // tl),
        in_specs=[
            pl.BlockSpec((None, tl, pool_width), lambda bi, i: (bi, i, 0)),
            pl.BlockSpec((None, HALO, pool_width),
                         lambda bi, i: (bi, jnp.maximum(i * hb - 1, 0), 0)),
            pl.BlockSpec((None, HALO, pool_width),
                         lambda bi, i: (bi, jnp.minimum((i + 1) * hb, last), 0)),
            pl.BlockSpec(pool_w.shape, lambda bi, i: (0, 0, 0)),
            pl.BlockSpec((1, pool_width), lambda bi, i: (0, 0)),
        ],
        out_specs=pl.BlockSpec((None, tl, pool_width), lambda bi, i: (bi, i, 0)),
        scratch_shapes=[pltpu.VMEM((tl + 2 * HALO, pool_width), F32)],
        compiler_params=_cparams(("parallel", "parallel")),
        name="pool_mixer",
    )(proj, proj, proj, pool_w, pool_scale.reshape(1, pool_width))


@functools.lru_cache(maxsize=None)
def _dft_constants():
    n = DFT_N1 * DFT_N2
    half = DFT_N1 // 2
    k1 = np.arange(DFT_K1)
    n1 = np.arange(half)
    ang = 2.0 * np.pi * np.outer(k1, n1) / DFT_N1
    f_fwd = np.zeros((2 * DFT_QP, half))
    f_fwd[:DFT_K1] = np.cos(ang)
    f_fwd[DFT_QP:DFT_QP + DFT_K1] = -np.sin(ang)
    n2 = np.arange(DFT_N2)
    k2 = np.arange(DFT_N2)
    g = np.zeros((DFT_K1, 2 * DFT_N2, 2 * DFT_N2))
    for a in range(DFT_K1):
        ph = 2.0 * np.pi * np.outer(a + DFT_N1 * k2, n2) / n
        gre, gim = np.cos(ph), -np.sin(ph)
        g[a] = np.block([[gre, -gim], [gim, gre]])
    weight = np.full(DFT_K1, 2.0)
    weight[0] = 1.0
    weight[-1] = 1.0
    f_inv = np.zeros((half, 2 * DFT_QP))
    f_inv[:, :DFT_K1] = weight * np.cos(ang.T) / n
    f_inv[:, DFT_QP:DFT_QP + DFT_K1] = -weight * np.sin(ang.T) / n
    return (f_fwd.astype(np.float32), g.astype(np.float32),
            np.transpose(g, (0, 2, 1)).astype(np.float32), f_inv.astype(np.float32))


def _pad_rows(dst_ref, src):
    for n1 in range(DFT_N1 // 2):
        dst_ref[n1 * DFT_PITCH:n1 * DFT_PITCH + DFT_N2, :] = src[n1 * DFT_N2:(n1 + 1) * DFT_N2, :]


def _contract_planes(src_ref, dst_ref, f_ref):
    n_dst, n_src = f_ref.shape

    def step(j, carry):
        r0 = j * PLANE_ROWS_PER_DOT
        x = jnp.concatenate(
            [src_ref[pl.ds(r0 + r, n_src, stride=DFT_PITCH), :].astype(BF16)
             for r in range(PLANE_ROWS_PER_DOT)], axis=1)
        y = _dot(f_ref[...], x)
        for r in range(PLANE_ROWS_PER_DOT):
            dst_ref[pl.ds(r0 + r, n_dst, stride=DFT_PITCH), :] = y[:, r * LANES:(r + 1) * LANES]
        return carry

    lax.fori_loop(0, DFT_N2 // PLANE_ROWS_PER_DOT, step, 0, unroll=PLANE_UNROLL)


def _load_planes(a_ref, k1):
    re = a_ref[pl.ds(pl.multiple_of(k1 * DFT_PITCH, SUBLANES), DFT_N2), :]
    im = a_ref[pl.ds(pl.multiple_of((DFT_QP + k1) * DFT_PITCH, SUBLANES), DFT_N2), :]
    return jnp.concatenate([re, im], axis=0)


def _store_planes(a_ref, k1, z):
    a_ref[pl.ds(pl.multiple_of(k1 * DFT_PITCH, SUBLANES), DFT_N2), :] = z[:DFT_N2]
    a_ref[pl.ds(pl.multiple_of((DFT_QP + k1) * DFT_PITCH, SUBLANES), DFT_N2), :] = z[DFT_N2:]


def _spectrum_body(fw_ref, bw_ref, ssf_ref, ssb_ref, ff_ref, g_ref, h_ref, u_ref, a_ref):
    norm = lax.rsqrt(ssf_ref[...] + ssb_ref[...])
    for part, src_ref in enumerate((fw_ref, bw_ref)):
        _pad_rows(u_ref, src_ref)
        _contract_planes(u_ref, a_ref, ff_ref)
        sign = 1.0 if part == 0 else -1.0

        def step(k1, carry):
            xh = _dot(g_ref[k1], _load_planes(a_ref, k1).astype(BF16))
            re = xh[:DFT_N2] * norm
            im = xh[DFT_N2:] * (sign * norm)
            if part == 0:
                h_ref[k1, :DFT_N2, :] = re
                h_ref[k1, DFT_N2:, :] = im
            else:
                h_ref[k1, :DFT_N2, :] += re
                h_ref[k1, DFT_N2:, :] += im
            return carry

        lax.fori_loop(0, DFT_K1, step, 0, unroll=K1_UNROLL)


def _filter_spectrum(taps, sumsq, width):
    seq_len = taps.shape[0]
    ff, g, _, _ = _dft_constants()
    cb = width // LANES
    half = DFT_N1 // 2
    return pl.pallas_call(
        _spectrum_body,
        out_shape=jax.ShapeDtypeStruct((HYENA_ORDER, DFT_K1, 2 * DFT_N2, width), F32),
        grid=(HYENA_ORDER, cb),
        in_specs=[
            pl.BlockSpec((seq_len, LANES), lambda o, c: (0, (2 * o) * cb + c)),
            pl.BlockSpec((seq_len, LANES), lambda o, c: (0, (2 * o + 1) * cb + c)),
            pl.BlockSpec((1, LANES), lambda o, c: (0, (2 * o) * cb + c)),
            pl.BlockSpec((1, LANES), lambda o, c: (0, (2 * o + 1) * cb + c)),
            pl.BlockSpec((2 * DFT_QP, half), lambda o, c: (0, 0)),
            pl.BlockSpec((DFT_K1, 2 * DFT_N2, 2 * DFT_N2), lambda o, c: (0, 0, 0)),
        ],
        out_specs=pl.BlockSpec((None, DFT_K1, 2 * DFT_N2, LANES), lambda o, c: (o, 0, 0, c)),
        scratch_shapes=[pltpu.VMEM((half * DFT_PITCH, LANES), F32),
                        pltpu.VMEM((2 * DFT_QP * DFT_PITCH, LANES), F32)],
        compiler_params=_cparams(("parallel", "parallel")),
        name="filter_spectrum",
    )(taps, taps, sumsq, sumsq, jnp.asarray(ff).astype(BF16), jnp.asarray(g).astype(BF16))


def _short_conv_rows(x_ref, n1, w_ref, b_ref):
    r0 = n1 * DFT_N2
    cur = x_ref[r0:r0 + DFT_N2, :]
    row = lax.broadcasted_iota(jnp.int32, cur.shape, 0)
    if r0 == 0:
        prev = jnp.where(row == 0, 0.0, pltpu.roll(cur, 1, axis=0))
    else:
        prev = x_ref[r0 - 1:r0 + DFT_N2 - 1, :]
    if r0 + DFT_N2 == x_ref.shape[0]:
        nxt = jnp.where(row == DFT_N2 - 1, 0.0, pltpu.roll(cur, DFT_N2 - 1, axis=0))
    else:
        nxt = x_ref[r0 + 1:r0 + DFT_N2 + 1, :]
    return b_ref[...] + prev * w_ref[0:1, :] + cur * w_ref[1:2, :] + nxt * w_ref[2:3, :]


def _hyena_body(v_ref, gate_ref, vw_ref, vb_ref, gw_ref, gb_ref, h_ref, bias_ref, ff_ref, g_ref,
                gt_ref, fi_ref, o_ref, u_ref, y_ref, a_ref):
    order = pl.program_id(2)
    half = DFT_N1 // 2

    @pl.when(order == 0)
    def _():
        for n1 in range(half):
            u_ref[n1 * DFT_PITCH:n1 * DFT_PITCH + DFT_N2, :] = _short_conv_rows(
                v_ref, n1, vw_ref, vb_ref)

    _contract_planes(u_ref, a_ref, ff_ref)

    def freq_step(k1, carry):
        xh = _dot(g_ref[k1], _load_planes(a_ref, k1).astype(BF16))
        xre, xim = xh[:DFT_N2], xh[DFT_N2:]
        hre, him = h_ref[k1, :DFT_N2, :], h_ref[k1, DFT_N2:, :]
        yh = jnp.concatenate([xre * hre - xim * him, xre * him + xim * hre], axis=0)
        _store_planes(a_ref, k1, _dot(gt_ref[k1], yh.astype(BF16)))
        return carry

    lax.fori_loop(0, DFT_K1, freq_step, 0, unroll=K1_UNROLL)
    _contract_planes(a_ref, y_ref, fi_ref)

    bias = bias_ref[...]
    for n1 in range(half):
        rows = slice(n1 * DFT_PITCH, n1 * DFT_PITCH + DFT_N2)
        gate = _short_conv_rows(gate_ref, n1, gw_ref, gb_ref)
        z = gate * (y_ref[rows, :] + u_ref[rows, :] * bias)
        u_ref[rows, :] = z
        o_ref[n1 * DFT_N2:(n1 + 1) * DFT_N2, :] = z


def _hyena_mixer(proj, first_col, conv_w, conv_b, spec, bias):
    assert HYENA_SHORT_CONV == 3
    b, seq_len, _ = proj.shape
    width = bias.shape[1]
    cb = width // LANES
    c0 = first_col // LANES
    half = DFT_N1 // 2
    ff, g, gt, fi = _dft_constants()
    value_col = lambda c, bi, o: HYENA_ORDER * cb + c
    gate_col = lambda c, bi, o: o * cb + c
    conv_b = conv_b.reshape(1, -1)
    return pl.pallas_call(
        _hyena_body,
        out_shape=jax.ShapeDtypeStruct((b, seq_len, width), F32),
        grid=(cb, b, HYENA_ORDER),
        in_specs=[
            pl.BlockSpec((None, seq_len, LANES), lambda c, bi, o: (bi, 0, c0 + value_col(c, bi, o))),
            pl.BlockSpec((None, seq_len, LANES), lambda c, bi, o: (bi, 0, c0 + gate_col(c, bi, o))),
            pl.BlockSpec((HYENA_SHORT_CONV, LANES), lambda c, bi, o: (0, value_col(c, bi, o))),
            pl.BlockSpec((1, LANES), lambda c, bi, o: (0, value_col(c, bi, o))),
            pl.BlockSpec((HYENA_SHORT_CONV, LANES), lambda c, bi, o: (0, gate_col(c, bi, o))),
            pl.BlockSpec((1, LANES), lambda c, bi, o: (0, gate_col(c, bi, o))),
            pl.BlockSpec((None, DFT_K1, 2 * DFT_N2, LANES), lambda c, bi, o: (o, 0, 0, c)),
            pl.BlockSpec((None, 1, LANES), lambda c, bi, o: (o, 0, c)),
            _resident((2 * DFT_QP, half)),
            _resident((DFT_K1, 2 * DFT_N2, 2 * DFT_N2)),
            _resident((DFT_K1, 2 * DFT_N2, 2 * DFT_N2)),
            _resident((half, 2 * DFT_QP)),
        ],
        out_specs=pl.BlockSpec((None, seq_len, LANES), lambda c, bi, o: (bi, 0, c)),
        scratch_shapes=[pltpu.VMEM((half * DFT_PITCH, LANES), F32),
                        pltpu.VMEM((half * DFT_PITCH, LANES), F32),
                        pltpu.VMEM((2 * DFT_QP * DFT_PITCH, LANES), F32)],
        compiler_params=_cparams(("parallel", "parallel", "arbitrary")),
        name="hyena_mixer",
    )(proj, proj, conv_w, conv_b, conv_w, conv_b, spec, bias.reshape(HYENA_ORDER, 1, width),
      jnp.asarray(ff).astype(BF16), jnp.asarray(g).astype(BF16), jnp.asarray(gt).astype(BF16),
      jnp.asarray(fi).astype(BF16))


def _hdot(a, b):
    return jnp.dot(a, b, preferred_element_type=F32, precision=lax.Precision.HIGHEST)


def _filter_taps_body(bwd_cols, z_ref, t_ref, w1_ref, b1_ref, w2_ref, b2_ref, w3_ref, b3_ref,
                      sf_ref, wo_ref, decay_ref, taps_ref, ss_ref):
    i = pl.program_id(0)
    h = jnp.sin(sf_ref[0:1, :] * (_hdot(z_ref[...], w1_ref[...]) + b1_ref[...]))
    h = jnp.sin(sf_ref[1:2, :] * (_hdot(h, w2_ref[...]) + b2_ref[...]))
    h = jnp.sin(sf_ref[2:3, :] * (_hdot(h, w3_ref[...]) + b3_ref[...]))
    h = _dot(h.astype(BF16), wo_ref[...])
    h = h * jnp.exp(-t_ref[...] * jnp.abs(decay_ref[...]))
    row = i * h.shape[0] + lax.broadcasted_iota(jnp.int32, h.shape, 0)
    h = jnp.where((row == 0) & (bwd_cols[...] > 0.0), 0.0, h)
    taps_ref[...] = h

    @pl.when(i == 0)
    def _():
        ss_ref[...] = jnp.zeros_like(ss_ref)

    ss_ref[...] += jnp.sum(h * h, axis=0, keepdims=True)


def _filter_taps(z, t, w1, b1, w2, b2, w3, b3, sin_freq, w_out, decay, *, tl=512):
    seq_len = z.shape[0]
    hid = LANES
    n_out = w_out.shape[1]
    width = n_out // (2 * HYENA_ORDER)

    def pad2(a, rows, cols):
        return jnp.pad(a, ((0, rows - a.shape[0]), (0, cols - a.shape[1])))

    is_bwd = np.tile(np.repeat(np.array([0.0, 1.0], np.float32), width), HYENA_ORDER)[None, :]
    full = lambda a: pl.BlockSpec(a.shape, lambda i: (0, 0))
    args = (
        pad2(z, seq_len, hid), t.reshape(seq_len, 1),
        pad2(w1, hid, hid), pad2(b1[None, :], 1, hid),
        pad2(w2, hid, hid), pad2(b2[None, :], 1, hid),
        pad2(w3, hid, hid), pad2(b3[None, :], 1, hid),
        pad2(sin_freq, 3, hid), pad2(w_out, hid, n_out).astype(BF16), decay.reshape(1, n_out),
    )
    bwd_cols = jnp.asarray(is_bwd)
    return pl.pallas_call(
        _filter_taps_body,
        out_shape=(jax.ShapeDtypeStruct((seq_len, n_out), F32),
                   jax.ShapeDtypeStruct((1, n_out), F32)),
        grid=(seq_len // tl,),
        in_specs=[full(bwd_cols),
                  pl.BlockSpec((tl, hid), lambda i: (i, 0)),
                  pl.BlockSpec((tl, 1), lambda i: (i, 0))] + [full(a) for a in args[2:]],
        out_specs=(pl.BlockSpec((tl, n_out), lambda i: (i, 0)),
                   pl.BlockSpec((1, n_out), lambda i: (0, 0))),
        compiler_params=_cparams(("arbitrary",)),
        name="filter_taps",
    )(bwd_cols, *args)


def _mla_qkv_body(c_ref, qg_ref, kvg_ref, wq_ref, wqr_ref, wk_ref, wv_ref, cos_ref, sin_ref,
                  q_ref, k_ref, v_ref):
    c = c_ref[...]
    cos = cos_ref[...]
    sin = sin_ref[...]
    cq = _rms(c[:, :MLA_Q_LORA], qg_ref[...]).astype(BF16)
    ckv = _rms(c[:, MLA_Q_LORA:MLA_Q_LORA + MLA_KV_LORA], kvg_ref[...]).astype(BF16)
    kr0 = MLA_Q_LORA + MLA_KV_LORA
    k_rope = c[:, kr0:kr0 + LANES] * cos + c[:, kr0 + LANES:kr0 + 2 * LANES] * sin
    ones_lane = (lax.broadcasted_iota(jnp.int32, (1, LANES), 1) == MLA_V).astype(F32)
    for h in range(MLA_HEADS):
        cols = slice(h * LANES, (h + 1) * LANES)
        q = _dot(cq, wq_ref[:, cols]) * cos + _dot(cq, wqr_ref[:, cols]) * sin
        q_ref[h] = (q * SOFTMAX_EXP2_SCALE).astype(BF16)
        k_ref[h] = (_dot(ckv, wk_ref[:, cols]) + k_rope).T.astype(BF16)
        v_ref[h] = (_dot(ckv, wv_ref[:, cols]) + ones_lane).astype(BF16)


def _mla_qkv(c, q_norm_g, kv_norm_g, wq, wq_rot, wk, wv, cos_t, sin_t, batch, *, tl=512):
    m, cw = c.shape
    seq_len = m // batch
    nl = seq_len // tl
    full = lambda a: pl.BlockSpec(a.shape, lambda bi, i: (0, 0))
    qg = q_norm_g.reshape(1, -1)
    kvg = kv_norm_g.reshape(1, -1)
    head_out = jax.ShapeDtypeStruct((batch, MLA_HEADS, seq_len, LANES), BF16)
    head_spec = pl.BlockSpec((None, MLA_HEADS, tl, LANES), lambda bi, i: (bi, 0, i, 0))
    return pl.pallas_call(
        _mla_qkv_body,
        out_shape=(head_out, jax.ShapeDtypeStruct((batch, MLA_HEADS, LANES, seq_len), BF16),
                   head_out),
        grid=(batch, nl),
        in_specs=[pl.BlockSpec((tl, cw), lambda bi, i: (bi * nl + i, 0)),
                  full(qg), full(kvg), full(wq), full(wq_rot), full(wk), full(wv),
                  pl.BlockSpec((tl, LANES), lambda bi, i: (i, 0)),
                  pl.BlockSpec((tl, LANES), lambda bi, i: (i, 0))],
        out_specs=(head_spec,
                   pl.BlockSpec((None, MLA_HEADS, LANES, tl), lambda bi, i: (bi, 0, 0, i)),
                   head_spec),
        compiler_params=_cparams(("parallel", "parallel")),
        name="mla_qkv",
    )(c, qg, kvg, wq, wq_rot, wk, wv, cos_t, sin_t)


def _attention_body(q_ref, k_ref, v_ref, o_ref):
    pair = LANES // MLA_V
    for g in range(q_ref.shape[0] // pair):
        outs = []
        for h in range(g * pair, (g + 1) * pair):
            s = _dot(q_ref[h], k_ref[h])
            p = jnp.exp2(s - jnp.max(s, axis=-1, keepdims=True))
            r = _dot(p.astype(BF16), v_ref[h])
            outs.append(r / r[:, MLA_V:MLA_V + 1])
        lane = lax.broadcasted_iota(jnp.int32, outs[0].shape, 1)
        both = jnp.where(lane < MLA_V, outs[0], pltpu.roll(outs[1], MLA_V, axis=1))
        o_ref[:, g * LANES:(g + 1) * LANES] = both.astype(o_ref.dtype)


def _attention(q, k, v, *, tq=256, group=8):
    batch, heads, seq_len, _ = q.shape
    width = group * MLA_V
    return pl.pallas_call(
        _attention_body,
        out_shape=jax.ShapeDtypeStruct((batch, seq_len, heads * MLA_V), BF16),
        grid=(batch, heads // group, seq_len // tq),
        in_specs=[
            pl.BlockSpec((None, group, tq, LANES), lambda bi, p, i: (bi, p, i, 0)),
            pl.BlockSpec((None, group, LANES, seq_len), lambda bi, p, i: (bi, p, 0, 0)),
            pl.BlockSpec((None, group, seq_len, LANES), lambda bi, p, i: (bi, p, 0, 0)),
        ],
        out_specs=pl.BlockSpec((None, tq, width), lambda bi, p, i: (bi, i, p)),
        compiler_params=_cparams(("parallel", "parallel", "parallel")),
        name="attention",
    )(q, k, v)


def _rope_rotation():
    half = MLA_ROPE // 2
    r = np.zeros((MLA_ROPE, MLA_ROPE), np.float32)
    for j in range(half):
        r[j + half, j] = -1.0
        r[j, j + half] = 1.0
    return jnp.asarray(r)


def _mla_weights(w_dq, w_uq, w_dkv, w_ukv):
    d = w_dq.shape[0]
    rot = _rope_rotation()
    qh = w_uq.reshape(MLA_Q_LORA, MLA_HEADS, MLA_NOPE + MLA_ROPE)
    zq = jnp.zeros((MLA_Q_LORA, MLA_HEADS, LANES - MLA_NOPE - MLA_ROPE), F32)
    wq = jnp.concatenate([qh, zq], axis=-1).reshape(MLA_Q_LORA, MLA_HEADS * LANES)
    q_rot = jnp.einsum("lhr,rs->lhs", qh[..., MLA_NOPE:], rot)
    wq_rot = jnp.concatenate([jnp.zeros((MLA_Q_LORA, MLA_HEADS, MLA_NOPE), F32), q_rot, zq],
                             axis=-1).reshape(MLA_Q_LORA, MLA_HEADS * LANES)
    kvh = w_ukv.reshape(MLA_KV_LORA, MLA_HEADS, MLA_NOPE + MLA_V)
    wk = jnp.concatenate([kvh[..., :MLA_NOPE],
                          jnp.zeros((MLA_KV_LORA, MLA_HEADS, LANES - MLA_NOPE), F32)],
                         axis=-1).reshape(MLA_KV_LORA, MLA_HEADS * LANES)
    wv = jnp.concatenate([kvh[..., MLA_NOPE:],
                          jnp.zeros((MLA_KV_LORA, MLA_HEADS, LANES - MLA_V), F32)],
                         axis=-1).reshape(MLA_KV_LORA, MLA_HEADS * LANES)
    w_kr = w_dkv[:, MLA_KV_LORA:]

    def slab(w):
        return jnp.concatenate([jnp.zeros((d, MLA_NOPE), F32), w,
                                jnp.zeros((d, LANES - MLA_NOPE - MLA_ROPE), F32)], axis=-1)

    w_down = jnp.concatenate([w_dq, w_dkv[:, :MLA_KV_LORA], slab(w_kr), slab(w_kr @ rot)], axis=-1)
    return (w_down.astype(BF16), wq.astype(BF16), wq_rot.astype(BF16), wk.astype(BF16),
            wv.astype(BF16))


def _rope_tables(seq_len):
    inv_freq = ROPE_THETA ** (-jnp.arange(0, MLA_ROPE, 2, dtype=F32) / MLA_ROPE)
    ang = jnp.arange(seq_len, dtype=F32)[:, None] * inv_freq[None, :]
    cos = jnp.cos(ang)
    sin = jnp.sin(ang)
    tail = LANES - MLA_NOPE - MLA_ROPE
    cos_t = jnp.concatenate([jnp.ones((seq_len, MLA_NOPE), F32), cos, cos,
                             jnp.ones((seq_len, tail), F32)], axis=-1)
    sin_t = jnp.concatenate([jnp.zeros((seq_len, MLA_NOPE), F32), sin, sin,
                             jnp.zeros((seq_len, tail), F32)], axis=-1)
    return cos_t, sin_t


def _position_features(seq_len, pos_dim):
    t = jnp.linspace(0.0, 1.0, seq_len, dtype=F32)
    bands = (pos_dim - 1) // 2
    w = 2.0 * math.pi * jnp.arange(seq_len, dtype=F32) / seq_len
    f = jnp.linspace(1e-4, bands - 1, bands, dtype=F32)
    phase = w[:, None] * f[None, :]
    z = jnp.concatenate([t[:, None], jnp.cos(phase), -jnp.sin(phase)], axis=-1)
    return t, z


def kernel(x, norm_g, ffn_w_gate, ffn_w_up, ffn_w_down, mix_w_in, pool_w, pool_scale, hyena_conv_w, hyena_conv_b, hyena_ffn_w1, hyena_ffn_b1, hyena_ffn_w2, hyena_ffn_b2, hyena_ffn_w3, hyena_ffn_b3, hyena_sin_freq, hyena_ffn_w_out, hyena_decay, hyena_bias, mix_w_out, mla_w_dq, mla_q_norm_g, mla_w_uq, mla_w_dkv, mla_kv_norm_g, mla_w_ukv, mla_w_o, final_norm_g):
    batch, seq_len, d = x.shape
    depth = norm_g.shape[0]
    assert 2 * seq_len == DFT_N1 * DFT_N2
    hy_width = hyena_bias.shape[-1]
    t_pos, z_pos = _position_features(seq_len, hyena_ffn_w1.shape[1])
    cos_t, sin_t = _rope_tables(seq_len)

    x = x.reshape(batch * seq_len, d)
    for i in range(depth):
        j = i // 2
        ffn_w = [[w[i, s].astype(BF16) for w in (ffn_w_gate, ffn_w_up, ffn_w_down)]
                 for s in range(2)]
        if i % 2 == 0:
            x, proj = _ffn(x, norm_g[i, 0], *ffn_w[0], next_g=norm_g[i, 1],
                           w_next=mix_w_in[j].astype(BF16))
            proj = proj.reshape(batch, seq_len, -1)
            y_pool = _pool_mixer(proj, pool_w[j].astype(BF16), pool_scale[j])
            taps, sumsq = _filter_taps(z_pos, t_pos, hyena_ffn_w1[j], hyena_ffn_b1[j],
                                       hyena_ffn_w2[j], hyena_ffn_b2[j], hyena_ffn_w3[j],
                                       hyena_ffn_b3[j], hyena_sin_freq[j], hyena_ffn_w_out[j],
                                       hyena_decay[j])
            spec = _filter_spectrum(taps, sumsq, hy_width)
            y_hyena = _hyena_mixer(proj, pool_scale.shape[-1], hyena_conv_w[j], hyena_conv_b[j],
                                   spec, hyena_bias[j])
            mix = (y_pool.reshape(batch * seq_len, -1), y_hyena.reshape(batch * seq_len, -1))
            w_mix = mix_w_out[j].astype(BF16)
        else:
            w_down, wq, wq_rot, wk, wv = _mla_weights(mla_w_dq[j], mla_w_uq[j], mla_w_dkv[j],
                                                      mla_w_ukv[j])
            x, c = _ffn(x, norm_g[i, 0], *ffn_w[0], next_g=norm_g[i, 1], w_next=w_down)
            q, k, v = _mla_qkv(c, mla_q_norm_g[j], mla_kv_norm_g[j], wq, wq_rot, wk, wv,
                               cos_t, sin_t, batch)
            mix = (_attention(q, k, v).reshape(batch * seq_len, -1),)
            w_mix = mla_w_o[j].astype(BF16)
        x = _ffn(x, norm_g[i, 2], *ffn_w[1], mix=mix, w_mix=w_mix,
                 final_g=final_norm_g if i == depth - 1 else None)
    return x.reshape(batch, seq_len, d)
```

```python
import functools
import math

import numpy as np
import jax
import jax.numpy as jnp
from jax import lax
from jax.experimental import pallas as pl
from jax.experimental.pallas import tpu as pltpu

F32 = jnp.float32
BF16 = jnp.bfloat16

RMS_EPS = 1e-6
MACARON_WEIGHT = 0.5
POOL_WINDOWS = (2, 4, 8, 16)
HYENA_ORDER = 2
HYENA_SHORT_CONV = 3
MLA_HEADS = 16
MLA_Q_LORA = 256
MLA_KV_LORA = 128
MLA_NOPE = 64
MLA_ROPE = 32
MLA_V = 64
ROPE_THETA = 10000.0
SOFTMAX_EXP2_SCALE = (MLA_NOPE + MLA_ROPE) ** -0.5 * math.log2(math.e)

LANES = 128
SUBLANES = 8
VMEM_LIMIT = 56 * 2**20

DFT_N1 = 64
DFT_N2 = 128
DFT_K1 = DFT_N1 // 2 + 1
DFT_QP = 40
DFT_PITCH = DFT_N2 + SUBLANES
PLANE_ROWS_PER_DOT = 8
PLANE_UNROLL = 4
K1_UNROLL = 11
HALO = 16
SCORE_LEAD = 2


def _cparams(semantics):
    return pltpu.CompilerParams(dimension_semantics=semantics, vmem_limit_bytes=VMEM_LIMIT)


def _rms(x, g):
    return x * lax.rsqrt(jnp.mean(x * x, axis=-1, keepdims=True) + RMS_EPS) * g


def _dot(a, b):
    return jnp.dot(a, b, preferred_element_type=F32)


def _ffn_body(n_mix, has_next, has_final, *refs):
    refs = list(refs)
    x_ref, g_ref, wg_ref, wu_ref, wd_ref = refs[:5]
    del refs[:5]
    mix_refs = [refs.pop(0) for _ in range(n_mix)]
    w_mix_ref = refs.pop(0) if n_mix else None
    next_g_ref, next_w_ref = (refs.pop(0), refs.pop(0)) if has_next else (None, None)
    final_g_ref = refs.pop(0) if has_final else None
    o_ref = refs.pop(0)

    x = x_ref[...]
    row = 0
    for a_ref in mix_refs:
        k = a_ref.shape[1]
        x = x + _dot(a_ref[...].astype(BF16), w_mix_ref[row:row + k, :])
        row += k
    h = _rms(x, g_ref[...]).astype(BF16)
    gate = _dot(h, wg_ref[...])
    up = _dot(h, wu_ref[...])
    act = gate / (1.0 + jnp.exp(-gate)) * up
    y = x + MACARON_WEIGHT * _dot(act.astype(BF16), wd_ref[...])
    if has_final:
        y = _rms(y, final_g_ref[...])
    o_ref[...] = y
    if has_next:
        refs.pop(0)[...] = _dot(_rms(y, next_g_ref[...]).astype(BF16), next_w_ref[...])


def _resident(shape):
    return pl.BlockSpec(shape, lambda *_: (0,) * len(shape), pipeline_mode=pl.Buffered(1))


def _stacked_resident(stack, index):
    lead = len(index)
    return pl.BlockSpec((None,) * lead + stack.shape[lead:],
                        lambda *_: tuple(index) + (0,) * (stack.ndim - lead),
                        pipeline_mode=pl.Buffered(1))


def _ffn(x, g, w_stacks, index, *, mix=(), w_mix=None, next_g=None, w_next=None, final_g=None,
         tm=512):
    m, d = x.shape
    row_block = lambda a: pl.BlockSpec((tm, a.shape[1]), lambda i: (i, 0))
    args = [x, g.reshape(1, d), *w_stacks]
    in_specs = [row_block(x), _resident((1, d))] + [_stacked_resident(w, index) for w in w_stacks]
    for a in mix:
        args.append(a)
        in_specs.append(row_block(a))
    if mix:
        args.append(w_mix)
        in_specs.append(_resident(w_mix.shape))
    if w_next is not None:
        args += [next_g.reshape(1, d), w_next]
        in_specs += [_resident((1, d)), _resident(w_next.shape)]
    if final_g is not None:
        args.append(final_g.reshape(1, d))
        in_specs.append(_resident((1, d)))
    out_shape = [jax.ShapeDtypeStruct((m, d), F32)]
    out_specs = [pl.BlockSpec((tm, d), lambda i: (i, 0))]
    if w_next is not None:
        out_shape.append(jax.ShapeDtypeStruct((m, w_next.shape[1]), F32))
        out_specs.append(pl.BlockSpec((tm, w_next.shape[1]), lambda i: (i, 0)))
    outs = pl.pallas_call(
        functools.partial(_ffn_body, len(mix), w_next is not None, final_g is not None),
        out_shape=out_shape,
        grid=(m // tm,),
        in_specs=in_specs,
        out_specs=out_specs,
        compiler_params=_cparams(("parallel",)),
        name="ffn",
    )(*args)
    return outs if w_next is not None else outs[0]


def _pool_body(seq_len, cur_ref, prev_ref, next_ref, pw_ref, ps_ref, yp_ref, ext_ref):
    i = pl.program_id(1)
    tl = cur_ref.shape[0]
    pool_width = yp_ref.shape[1]
    group = pool_width // len(POOL_WINDOWS)
    ext_ref[0:HALO, :] = jnp.where(i > 0, prev_ref[...], 0.0)
    ext_ref[HALO:HALO + tl, :] = cur_ref[...]
    ext_ref[HALO + tl:, :] = jnp.where(i < pl.num_programs(1) - 1, next_ref[...], 0.0)

    t = i * tl + lax.broadcasted_iota(jnp.int32, (tl, group), 0)
    for g, w in enumerate(POOL_WINDOWS):
        c0 = g * group
        lo = jnp.clip(t - w // 2, 0, seq_len)
        hi = jnp.clip(t - w // 2 + w, 0, seq_len)
        cnt = (hi - lo).astype(F32)
        s = ext_ref[HALO - w // 2:HALO - w // 2 + tl, c0:c0 + group]
        for d in range(1 - w // 2, w - w // 2):
            s = s + ext_ref[HALO + d:HALO + d + tl, c0:c0 + group]
        p = s / cnt - cur_ref[:, c0:c0 + group]
        y = _dot(p.astype(BF16), pw_ref[g])
        yp_ref[:, c0:c0 + group] = y * ps_ref[:, c0:c0 + group]


def _pool_mixer(proj, pool_w, pool_scale, *, tl=512):
    b, seq_len, _ = proj.shape
    pool_width = pool_scale.shape[0]
    hb = tl // HALO
    last = seq_len // HALO - 1
    return pl.pallas_call(
        functools.partial(_pool_body, seq_len),
        out_shape=jax.ShapeDtypeStruct((b, seq_len, pool_width), F32),
        grid=(b, seq_len // tl),
        in_specs=[
            pl.BlockSpec((None, tl, pool_width), lambda bi, i: (bi, i, 0)),
            pl.BlockSpec((None, HALO, pool_width),
                         lambda bi, i: (bi, jnp.maximum(i * hb - 1, 0), 0)),
            pl.BlockSpec((None, HALO, pool_width),
                         lambda bi, i: (bi, jnp.minimum((i + 1) * hb, last), 0)),
            pl.BlockSpec(pool_w.shape, lambda bi, i: (0, 0, 0)),
            pl.BlockSpec((1, pool_width), lambda bi, i: (0, 0)),
        ],
        out_specs=pl.BlockSpec((None, tl, pool_width), lambda bi, i: (bi, i, 0)),
        scratch_shapes=[pltpu.VMEM((tl + 2 * HALO, pool_width), F32)],
        compiler_params=_cparams(("parallel", "parallel")),
        name="pool_mixer",
    )(proj, proj, proj, pool_w, pool_scale.reshape(1, pool_width))


@functools.lru_cache(maxsize=None)
def _dft_constants():
    n = DFT_N1 * DFT_N2
    half = DFT_N1 // 2
    k1 = np.arange(DFT_K1)
    n1 = np.arange(half)
    ang = 2.0 * np.pi * np.outer(k1, n1) / DFT_N1
    f_fwd = np.zeros((2 * DFT_QP, half))
    f_fwd[:DFT_K1] = np.cos(ang)
    f_fwd[DFT_QP:DFT_QP + DFT_K1] = -np.sin(ang)
    n2 = np.arange(DFT_N2)
    k2 = np.arange(DFT_N2)
    g = np.zeros((DFT_K1, 2 * DFT_N2, 2 * DFT_N2))
    for a in range(DFT_K1):
        ph = 2.0 * np.pi * np.outer(a + DFT_N1 * k2, n2) / n
        gre, gim = np.cos(ph), -np.sin(ph)
        g[a] = np.block([[gre, -gim], [gim, gre]])
    weight = np.full(DFT_K1, 2.0)
    weight[0] = 1.0
    weight[-1] = 1.0
    f_inv = np.zeros((half, 2 * DFT_QP))
    f_inv[:, :DFT_K1] = weight * np.cos(ang.T) / n
    f_inv[:, DFT_QP:DFT_QP + DFT_K1] = -weight * np.sin(ang.T) / n
    return (f_fwd.astype(np.float32), g.astype(np.float32),
            np.transpose(g, (0, 2, 1)).astype(np.float32), f_inv.astype(np.float32))


def _pad_rows(dst_ref, src):
    for n1 in range(DFT_N1 // 2):
        dst_ref[n1 * DFT_PITCH:n1 * DFT_PITCH + DFT_N2, :] = src[n1 * DFT_N2:(n1 + 1) * DFT_N2, :]


def _contract_planes(src_ref, dst_ref, f_ref):
    n_dst, n_src = f_ref.shape

    def step(j, carry):
        r0 = j * PLANE_ROWS_PER_DOT
        x = jnp.concatenate(
            [src_ref[pl.ds(r0 + r, n_src, stride=DFT_PITCH), :].astype(BF16)
             for r in range(PLANE_ROWS_PER_DOT)], axis=1)
        y = _dot(f_ref[...], x)
        for r in range(PLANE_ROWS_PER_DOT):
            dst_ref[pl.ds(r0 + r, n_dst, stride=DFT_PITCH), :] = y[:, r * LANES:(r + 1) * LANES]
        return carry

    lax.fori_loop(0, DFT_N2 // PLANE_ROWS_PER_DOT, step, 0, unroll=PLANE_UNROLL)


def _load_planes(a_ref, k1):
    re = a_ref[pl.ds(pl.multiple_of(k1 * DFT_PITCH, SUBLANES), DFT_N2), :]
    im = a_ref[pl.ds(pl.multiple_of((DFT_QP + k1) * DFT_PITCH, SUBLANES), DFT_N2), :]
    return jnp.concatenate([re, im], axis=0)


def _store_planes(a_ref, k1, z):
    a_ref[pl.ds(pl.multiple_of(k1 * DFT_PITCH, SUBLANES), DFT_N2), :] = z[:DFT_N2]
    a_ref[pl.ds(pl.multiple_of((DFT_QP + k1) * DFT_PITCH, SUBLANES), DFT_N2), :] = z[DFT_N2:]


def _spectrum_body(fw_ref, bw_ref, ssf_ref, ssb_ref, ff_ref, g_ref, h_ref, u_ref, a_ref):
    norm = lax.rsqrt(ssf_ref[...] + ssb_ref[...])
    for part, src_ref in enumerate((fw_ref, bw_ref)):
        _pad_rows(u_ref, src_ref)
        _contract_planes(u_ref, a_ref, ff_ref)
        sign = 1.0 if part == 0 else -1.0

        def step(k1, carry):
            xh = _dot(g_ref[k1], _load_planes(a_ref, k1).astype(BF16))
            re = xh[:DFT_N2] * norm
            im = xh[DFT_N2:] * (sign * norm)
            if part == 0:
                h_ref[k1, :DFT_N2, :] = re
                h_ref[k1, DFT_N2:, :] = im
            else:
                h_ref[k1, :DFT_N2, :] += re
                h_ref[k1, DFT_N2:, :] += im
            return carry

        lax.fori_loop(0, DFT_K1, step, 0, unroll=K1_UNROLL)


def _filter_spectrum(taps, sumsq, width):
    seq_len = taps.shape[0]
    ff, g, _, _ = _dft_constants()
    cb = width // LANES
    half = DFT_N1 // 2
    return pl.pallas_call(
        _spectrum_body,
        out_shape=jax.ShapeDtypeStruct((HYENA_ORDER, DFT_K1, 2 * DFT_N2, width), F32),
        grid=(HYENA_ORDER, cb),
        in_specs=[
            pl.BlockSpec((seq_len, LANES), lambda o, c: (0, (2 * o) * cb + c)),
            pl.BlockSpec((seq_len, LANES), lambda o, c: (0, (2 * o + 1) * cb + c)),
            pl.BlockSpec((1, LANES), lambda o, c: (0, (2 * o) * cb + c)),
            pl.BlockSpec((1, LANES), lambda o, c: (0, (2 * o + 1) * cb + c)),
            pl.BlockSpec((2 * DFT_QP, half), lambda o, c: (0, 0)),
            pl.BlockSpec((DFT_K1, 2 * DFT_N2, 2 * DFT_N2), lambda o, c: (0, 0, 0)),
        ],
        out_specs=pl.BlockSpec((None, DFT_K1, 2 * DFT_N2, LANES), lambda o, c: (o, 0, 0, c)),
        scratch_shapes=[pltpu.VMEM((half * DFT_PITCH, LANES), F32),
                        pltpu.VMEM((2 * DFT_QP * DFT_PITCH, LANES), F32)],
        compiler_params=_cparams(("parallel", "parallel")),
        name="filter_spectrum",
    )(taps, taps, sumsq, sumsq, jnp.asarray(ff).astype(BF16), jnp.asarray(g).astype(BF16))


def _short_conv_rows(x_ref, n1, w_ref, b_ref):
    r0 = n1 * DFT_N2
    cur = x_ref[r0:r0 + DFT_N2, :]
    row = lax.broadcasted_iota(jnp.int32, cur.shape, 0)
    if r0 == 0:
        prev = jnp.where(row == 0, 0.0, pltpu.roll(cur, 1, axis=0))
    else:
        prev = x_ref[r0 - 1:r0 + DFT_N2 - 1, :]
    if r0 + DFT_N2 == x_ref.shape[0]:
        nxt = jnp.where(row == DFT_N2 - 1, 0.0, pltpu.roll(cur, DFT_N2 - 1, axis=0))
    else:
        nxt = x_ref[r0 + 1:r0 + DFT_N2 + 1, :]
    return b_ref[...] + prev * w_ref[0:1, :] + cur * w_ref[1:2, :] + nxt * w_ref[2:3, :]


def _hyena_body(v_ref, gate_ref, vw_ref, vb_ref, gw_ref, gb_ref, h_ref, bias_ref, ff_ref, g_ref,
                gt_ref, fi_ref, o_ref, u_ref, y_ref, a_ref):
    order = pl.program_id(2)
    half = DFT_N1 // 2

    @pl.when(order == 0)
    def _():
        for n1 in range(half):
            u_ref[n1 * DFT_PITCH:n1 * DFT_PITCH + DFT_N2, :] = _short_conv_rows(
                v_ref, n1, vw_ref, vb_ref)

    _contract_planes(u_ref, a_ref, ff_ref)

    def freq_step(k1, carry):
        xh = _dot(g_ref[k1], _load_planes(a_ref, k1).astype(BF16))
        xre, xim = xh[:DFT_N2], xh[DFT_N2:]
        hre, him = h_ref[k1, :DFT_N2, :], h_ref[k1, DFT_N2:, :]
        yh = jnp.concatenate([xre * hre - xim * him, xre * him + xim * hre], axis=0)
        _store_planes(a_ref, k1, _dot(gt_ref[k1], yh.astype(BF16)))
        return carry

    lax.fori_loop(0, DFT_K1, freq_step, 0, unroll=K1_UNROLL)
    _contract_planes(a_ref, y_ref, fi_ref)

    bias = bias_ref[...]
    for n1 in range(half):
        rows = slice(n1 * DFT_PITCH, n1 * DFT_PITCH + DFT_N2)
        gate = _short_conv_rows(gate_ref, n1, gw_ref, gb_ref)
        z = gate * (y_ref[rows, :] + u_ref[rows, :] * bias)
        u_ref[rows, :] = z
        o_ref[n1 * DFT_N2:(n1 + 1) * DFT_N2, :] = z


def _hyena_mixer(proj, first_col, conv_w, conv_b, spec, bias):
    assert HYENA_SHORT_CONV == 3
    b, seq_len, _ = proj.shape
    width = bias.shape[1]
    cb = width // LANES
    c0 = first_col // LANES
    half = DFT_N1 // 2
    ff, g, gt, fi = _dft_constants()
    value_col = lambda c, bi, o: HYENA_ORDER * cb + c
    gate_col = lambda c, bi, o: o * cb + c
    conv_b = conv_b.reshape(1, -1)
    return pl.pallas_call(
        _hyena_body,
        out_shape=jax.ShapeDtypeStruct((b, seq_len, width), F32),
        grid=(cb, b, HYENA_ORDER),
        in_specs=[
            pl.BlockSpec((None, seq_len, LANES), lambda c, bi, o: (bi, 0, c0 + value_col(c, bi, o))),
            pl.BlockSpec((None, seq_len, LANES), lambda c, bi, o: (bi, 0, c0 + gate_col(c, bi, o))),
            pl.BlockSpec((HYENA_SHORT_CONV, LANES), lambda c, bi, o: (0, value_col(c, bi, o))),
            pl.BlockSpec((1, LANES), lambda c, bi, o: (0, value_col(c, bi, o))),
            pl.BlockSpec((HYENA_SHORT_CONV, LANES), lambda c, bi, o: (0, gate_col(c, bi, o))),
            pl.BlockSpec((1, LANES), lambda c, bi, o: (0, gate_col(c, bi, o))),
            pl.BlockSpec((None, DFT_K1, 2 * DFT_N2, LANES), lambda c, bi, o: (o, 0, 0, c)),
            pl.BlockSpec((None, 1, LANES), lambda c, bi, o: (o, 0, c)),
            _resident((2 * DFT_QP, half)),
            _resident((DFT_K1, 2 * DFT_N2, 2 * DFT_N2)),
            _resident((DFT_K1, 2 * DFT_N2, 2 * DFT_N2)),
            _resident((half, 2 * DFT_QP)),
        ],
        out_specs=pl.BlockSpec((None, seq_len, LANES), lambda c, bi, o: (bi, 0, c)),
        scratch_shapes=[pltpu.VMEM((half * DFT_PITCH, LANES), F32),
                        pltpu.VMEM((half * DFT_PITCH, LANES), F32),
                        pltpu.VMEM((2 * DFT_QP * DFT_PITCH, LANES), F32)],
        compiler_params=_cparams(("parallel", "parallel", "arbitrary")),
        name="hyena_mixer",
    )(proj, proj, conv_w, conv_b, conv_w, conv_b, spec, bias.reshape(HYENA_ORDER, 1, width),
      jnp.asarray(ff).astype(BF16), jnp.asarray(g).astype(BF16), jnp.asarray(gt).astype(BF16),
      jnp.asarray(fi).astype(BF16))


def _hdot(a, b):
    return jnp.dot(a, b, preferred_element_type=F32, precision=lax.Precision.HIGHEST)


def _filter_taps_body(bwd_cols, z_ref, t_ref, w1_ref, b1_ref, w2_ref, b2_ref, w3_ref, b3_ref,
                      sf_ref, wo_ref, decay_ref, taps_ref, ss_ref):
    i = pl.program_id(0)
    h = jnp.sin(sf_ref[0:1, :] * (_hdot(z_ref[...], w1_ref[...]) + b1_ref[...]))
    h = jnp.sin(sf_ref[1:2, :] * (_hdot(h, w2_ref[...]) + b2_ref[...]))
    h = jnp.sin(sf_ref[2:3, :] * (_hdot(h, w3_ref[...]) + b3_ref[...]))
    h = _dot(h.astype(BF16), wo_ref[...])
    h = h * jnp.exp(-t_ref[...] * jnp.abs(decay_ref[...]))
    row = i * h.shape[0] + lax.broadcasted_iota(jnp.int32, h.shape, 0)
    h = jnp.where((row == 0) & (bwd_cols[...] > 0.0), 0.0, h)
    taps_ref[...] = h

    @pl.when(i == 0)
    def _():
        ss_ref[...] = jnp.zeros_like(ss_ref)

    ss_ref[...] += jnp.sum(h * h, axis=0, keepdims=True)


def _filter_taps(z, t, w1, b1, w2, b2, w3, b3, sin_freq, w_out, decay, *, tl=512):
    seq_len = z.shape[0]
    hid = LANES
    n_out = w_out.shape[1]
    width = n_out // (2 * HYENA_ORDER)

    def pad2(a, rows, cols):
        return jnp.pad(a, ((0, rows - a.shape[0]), (0, cols - a.shape[1])))

    is_bwd = np.tile(np.repeat(np.array([0.0, 1.0], np.float32), width), HYENA_ORDER)[None, :]
    full = lambda a: pl.BlockSpec(a.shape, lambda i: (0, 0))
    args = (
        pad2(z, seq_len, hid), t.reshape(seq_len, 1),
        pad2(w1, hid, hid), pad2(b1[None, :], 1, hid),
        pad2(w2, hid, hid), pad2(b2[None, :], 1, hid),
        pad2(w3, hid, hid), pad2(b3[None, :], 1, hid),
        pad2(sin_freq, 3, hid), pad2(w_out, hid, n_out).astype(BF16), decay.reshape(1, n_out),
    )
    bwd_cols = jnp.asarray(is_bwd)
    return pl.pallas_call(
        _filter_taps_body,
        out_shape=(jax.ShapeDtypeStruct((seq_len, n_out), F32),
                   jax.ShapeDtypeStruct((1, n_out), F32)),
        grid=(seq_len // tl,),
        in_specs=[full(bwd_cols),
                  pl.BlockSpec((tl, hid), lambda i: (i, 0)),
                  pl.BlockSpec((tl, 1), lambda i: (i, 0))] + [full(a) for a in args[2:]],
        out_specs=(pl.BlockSpec((tl, n_out), lambda i: (i, 0)),
                   pl.BlockSpec((1, n_out), lambda i: (0, 0))),
        compiler_params=_cparams(("arbitrary",)),
        name="filter_taps",
    )(bwd_cols, *args)


def _mla_qkv_body(c_ref, qg_ref, kvg_ref, wq_ref, wqr_ref, wk_ref, wv_ref, cos_ref, sin_ref,
                  q_ref, k_ref, v_ref):
    c = c_ref[...]
    cos = cos_ref[...]
    sin = sin_ref[...]
    cq = _rms(c[:, :MLA_Q_LORA], qg_ref[...]).astype(BF16)
    ckv = _rms(c[:, MLA_Q_LORA:MLA_Q_LORA + MLA_KV_LORA], kvg_ref[...]).astype(BF16)
    kr0 = MLA_Q_LORA + MLA_KV_LORA
    k_rope = c[:, kr0:kr0 + LANES] * cos + c[:, kr0 + LANES:kr0 + 2 * LANES] * sin
    ones_lane = (lax.broadcasted_iota(jnp.int32, (1, LANES), 1) == MLA_V).astype(F32)
    for h in range(MLA_HEADS):
        cols = slice(h * LANES, (h + 1) * LANES)
        q = _dot(cq, wq_ref[:, cols]) * cos + _dot(cq, wqr_ref[:, cols]) * sin
        q_ref[h] = (q * SOFTMAX_EXP2_SCALE).astype(BF16)
        k_ref[h] = (_dot(ckv, wk_ref[:, cols]) + k_rope).T.astype(BF16)
        v_ref[h] = (_dot(ckv, wv_ref[:, cols]) + ones_lane).astype(BF16)


def _mla_qkv(c, q_norm_g, kv_norm_g, wq, wq_rot, wk, wv, cos_t, sin_t, batch, *, tl=512):
    m, cw = c.shape
    seq_len = m // batch
    nl = seq_len // tl
    full = lambda a: pl.BlockSpec(a.shape, lambda bi, i: (0, 0))
    qg = q_norm_g.reshape(1, -1)
    kvg = kv_norm_g.reshape(1, -1)
    head_out = jax.ShapeDtypeStruct((batch, MLA_HEADS, seq_len, LANES), BF16)
    head_spec = pl.BlockSpec((None, MLA_HEADS, tl, LANES), lambda bi, i: (bi, 0, i, 0))
    return pl.pallas_call(
        _mla_qkv_body,
        out_shape=(head_out, jax.ShapeDtypeStruct((batch, MLA_HEADS, LANES, seq_len), BF16),
                   head_out),
        grid=(batch, nl),
        in_specs=[pl.BlockSpec((tl, cw), lambda bi, i: (bi * nl + i, 0)),
                  full(qg), full(kvg), full(wq), full(wq_rot), full(wk), full(wv),
                  pl.BlockSpec((tl, LANES), lambda bi, i: (i, 0)),
                  pl.BlockSpec((tl, LANES), lambda bi, i: (i, 0))],
        out_specs=(head_spec,
                   pl.BlockSpec((None, MLA_HEADS, LANES, tl), lambda bi, i: (bi, 0, 0, i)),
                   head_spec),
        compiler_params=_cparams(("parallel", "parallel")),
        name="mla_qkv",
    )(c, qg, kvg, wq, wq_rot, wk, wv, cos_t, sin_t)


def _attention_body(q_ref, qn_ref, k_ref, v_ref, o_ref, *s_refs):
    pair = LANES // MLA_V
    n_heads = q_ref.shape[0]

    @pl.when(pl.program_id(2) == 0)
    def _():
        for h, s_ref in enumerate(s_refs):
            s_ref[...] = _dot(q_ref[h], k_ref[h])

    ahead = {h: s_ref[...] for h, s_ref in enumerate(s_refs)}
    for g in range(n_heads // pair):
        outs = []
        for h in range(g * pair, (g + 1) * pair):
            lead = h + SCORE_LEAD
            if lead < n_heads:
                ahead[lead] = _dot(q_ref[lead], k_ref[lead])
            else:
                s_refs[lead - n_heads][...] = _dot(qn_ref[lead - n_heads], k_ref[lead - n_heads])
            s = ahead.pop(h)
            p = jnp.exp2(s - jnp.max(s, axis=-1, keepdims=True))
            r = _dot(p.astype(BF16), v_ref[h])
            outs.append(r / r[:, MLA_V:MLA_V + 1])
        lane = lax.broadcasted_iota(jnp.int32, outs[0].shape, 1)
        both = jnp.where(lane < MLA_V, outs[0], pltpu.roll(outs[1], MLA_V, axis=1))
        o_ref[:, g * LANES:(g + 1) * LANES] = both.astype(o_ref.dtype)


def _attention(q, k, v, *, tq=256, group=8):
    batch, heads, seq_len, _ = q.shape
    width = group * MLA_V
    n_tiles = seq_len // tq
    return pl.pallas_call(
        _attention_body,
        out_shape=jax.ShapeDtypeStruct((batch, seq_len, heads * MLA_V), BF16),
        grid=(batch, heads // group, n_tiles),
        in_specs=[
            pl.BlockSpec((None, group, tq, LANES), lambda bi, p, i: (bi, p, i, 0)),
            pl.BlockSpec((None, SCORE_LEAD, tq, LANES),
                         lambda bi, p, i: (bi, p * (group // SCORE_LEAD),
                                           jnp.minimum(i + 1, n_tiles - 1), 0)),
            pl.BlockSpec((None, group, LANES, seq_len), lambda bi, p, i: (bi, p, 0, 0)),
            pl.BlockSpec((None, group, seq_len, LANES), lambda bi, p, i: (bi, p, 0, 0)),
        ],
        out_specs=pl.BlockSpec((None, tq, width), lambda bi, p, i: (bi, i, p)),
        scratch_shapes=[pltpu.VMEM((tq, seq_len), F32)] * SCORE_LEAD,
        compiler_params=_cparams(("parallel", "parallel", "arbitrary")),
        name="attention",
    )(q, q, k, v)


def _rope_rotation():
    half = MLA_ROPE // 2
    r = np.zeros((MLA_ROPE, MLA_ROPE), np.float32)
    for j in range(half):
        r[j + half, j] = -1.0
        r[j, j + half] = 1.0
    return jnp.asarray(r)


def _mla_weights(w_dq, w_uq, w_dkv, w_ukv):
    d = w_dq.shape[0]
    rot = _rope_rotation()
    qh = w_uq.reshape(MLA_Q_LORA, MLA_HEADS, MLA_NOPE + MLA_ROPE)
    zq = jnp.zeros((MLA_Q_LORA, MLA_HEADS, LANES - MLA_NOPE - MLA_ROPE), F32)
    wq = jnp.concatenate([qh, zq], axis=-1).reshape(MLA_Q_LORA, MLA_HEADS * LANES)
    q_rot = jnp.einsum("lhr,rs->lhs", qh[..., MLA_NOPE:], rot)
    wq_rot = jnp.concatenate([jnp.zeros((MLA_Q_LORA, MLA_HEADS, MLA_NOPE), F32), q_rot, zq],
                             axis=-1).reshape(MLA_Q_LORA, MLA_HEADS * LANES)
    kvh = w_ukv.reshape(MLA_KV_LORA, MLA_HEADS, MLA_NOPE + MLA_V)
    wk = jnp.concatenate([kvh[..., :MLA_NOPE],
                          jnp.zeros((MLA_KV_LORA, MLA_HEADS, LANES - MLA_NOPE), F32)],
                         axis=-1).reshape(MLA_KV_LORA, MLA_HEADS * LANES)
    wv = jnp.concatenate([kvh[..., MLA_NOPE:],
                          jnp.zeros((MLA_KV_LORA, MLA_HEADS, LANES - MLA_V), F32)],
                         axis=-1).reshape(MLA_KV_LORA, MLA_HEADS * LANES)
    w_kr = w_dkv[:, MLA_KV_LORA:]

    def slab(w):
        return jnp.concatenate([jnp.zeros((d, MLA_NOPE), F32), w,
                                jnp.zeros((d, LANES - MLA_NOPE - MLA_ROPE), F32)], axis=-1)

    w_down = jnp.concatenate([w_dq, w_dkv[:, :MLA_KV_LORA], slab(w_kr), slab(w_kr @ rot)], axis=-1)
    return (w_down.astype(BF16), wq.astype(BF16), wq_rot.astype(BF16), wk.astype(BF16),
            wv.astype(BF16))


def _rope_tables(seq_len):
    inv_freq = ROPE_THETA ** (-jnp.arange(0, MLA_ROPE, 2, dtype=F32) / MLA_ROPE)
    ang = jnp.arange(seq_len, dtype=F32)[:, None] * inv_freq[None, :]
    cos = jnp.cos(ang)
    sin = jnp.sin(ang)
    tail = LANES - MLA_NOPE - MLA_ROPE
    cos_t = jnp.concatenate([jnp.ones((seq_len, MLA_NOPE), F32), cos, cos,
                             jnp.ones((seq_len, tail), F32)], axis=-1)
    sin_t = jnp.concatenate([jnp.zeros((seq_len, MLA_NOPE), F32), sin, sin,
                             jnp.zeros((seq_len, tail), F32)], axis=-1)
    return cos_t, sin_t


def _position_features(seq_len, pos_dim):
    t = jnp.linspace(0.0, 1.0, seq_len, dtype=F32)
    bands = (pos_dim - 1) // 2
    w = 2.0 * math.pi * jnp.arange(seq_len, dtype=F32) / seq_len
    f = jnp.linspace(1e-4, bands - 1, bands, dtype=F32)
    phase = w[:, None] * f[None, :]
    z = jnp.concatenate([t[:, None], jnp.cos(phase), -jnp.sin(phase)], axis=-1)
    return t, z


def kernel(x, norm_g, ffn_w_gate, ffn_w_up, ffn_w_down, mix_w_in, pool_w, pool_scale, hyena_conv_w, hyena_conv_b, hyena_ffn_w1, hyena_ffn_b1, hyena_ffn_w2, hyena_ffn_b2, hyena_ffn_w3, hyena_ffn_b3, hyena_sin_freq, hyena_ffn_w_out, hyena_decay, hyena_bias, mix_w_out, mla_w_dq, mla_q_norm_g, mla_w_uq, mla_w_dkv, mla_kv_norm_g, mla_w_ukv, mla_w_o, final_norm_g):
    batch, seq_len, d = x.shape
    depth = norm_g.shape[0]
    assert 2 * seq_len == DFT_N1 * DFT_N2
    hy_width = hyena_bias.shape[-1]
    t_pos, z_pos = _position_features(seq_len, hyena_ffn_w1.shape[1])
    cos_t, sin_t = _rope_tables(seq_len)

    x = x.reshape(batch * seq_len, d)
    ffn_w = [w.astype(BF16) for w in (ffn_w_gate, ffn_w_up, ffn_w_down)]
    for i in range(depth):
        j = i // 2
        if i % 2 == 0:
            x, proj = _ffn(x, norm_g[i, 0], ffn_w, (i, 0), next_g=norm_g[i, 1],
                           w_next=mix_w_in[j].astype(BF16))
            proj = proj.reshape(batch, seq_len, -1)
            y_pool = _pool_mixer(proj, pool_w[j].astype(BF16), pool_scale[j])
            taps, sumsq = _filter_taps(z_pos, t_pos, hyena_ffn_w1[j], hyena_ffn_b1[j],
                                       hyena_ffn_w2[j], hyena_ffn_b2[j], hyena_ffn_w3[j],
                                       hyena_ffn_b3[j], hyena_sin_freq[j], hyena_ffn_w_out[j],
                                       hyena_decay[j])
            spec = _filter_spectrum(taps, sumsq, hy_width)
            y_hyena = _hyena_mixer(proj, pool_scale.shape[-1], hyena_conv_w[j], hyena_conv_b[j],
                                   spec, hyena_bias[j])
            mix = (y_pool.reshape(batch * seq_len, -1), y_hyena.reshape(batch * seq_len, -1))
            w_mix = mix_w_out[j].astype(BF16)
        else:
            w_down, wq, wq_rot, wk, wv = _mla_weights(mla_w_dq[j], mla_w_uq[j], mla_w_dkv[j],
                                                      mla_w_ukv[j])
            x, c = _ffn(x, norm_g[i, 0], ffn_w, (i, 0), next_g=norm_g[i, 1], w_next=w_down)
            q, k, v = _mla_qkv(c, mla_q_norm_g[j], mla_kv_norm_g[j], wq, wq_rot, wk, wv,
                               cos_t, sin_t, batch)
            mix = (_attention(q, k, v).reshape(batch * seq_len, -1),)
            w_mix = mla_w_o[j].astype(BF16)
        x = _ffn(x, norm_g[i, 2], ffn_w, (i, 1), mix=mix, w_mix=w_mix,
                 final_g=final_norm_g if i == depth - 1 else None)
    return x.reshape(batch, seq_len, d)
```

```python
import functools
import math

import numpy as np
import jax
import jax.numpy as jnp
from jax import lax
from jax.experimental import pallas as pl
from jax.experimental.pallas import tpu as pltpu

F32 = jnp.float32
BF16 = jnp.bfloat16

RMS_EPS = 1e-6
MACARON_WEIGHT = 0.5
POOL_WINDOWS = (2, 4, 8, 16)
HYENA_ORDER = 2
HYENA_SHORT_CONV = 3
MLA_HEADS = 16
MLA_Q_LORA = 256
MLA_KV_LORA = 128
MLA_NOPE = 64
MLA_ROPE = 32
MLA_V = 64
ROPE_THETA = 10000.0
SOFTMAX_EXP2_SCALE = (MLA_NOPE + MLA_ROPE) ** -0.5 * math.log2(math.e)

LANES = 128
SUBLANES = 8
VMEM_LIMIT = 56 * 2**20

DFT_N1 = 64
DFT_N2 = 128
DFT_K1 = DFT_N1 // 2 + 1
DFT_QP = 40
DFT_PITCH = DFT_N2 + SUBLANES
PLANE_ROWS_PER_DOT = 8
PLANE_UNROLL = 4
K1_UNROLL = 11
HALO = 16
FFN_LOAD_STEPS = 8
SCORE_LEAD = 2


def _cparams(semantics):
    return pltpu.CompilerParams(dimension_semantics=semantics, vmem_limit_bytes=VMEM_LIMIT)


def _rms(x, g):
    return x * lax.rsqrt(jnp.mean(x * x, axis=-1, keepdims=True) + RMS_EPS) * g


def _dot(a, b):
    return jnp.dot(a, b, preferred_element_type=F32)


def _ffn_body(n_mix, has_next, has_final, *refs):
    refs = list(refs)
    x_ref, g_ref, wg_ref, wu_ref, wd_ref = refs[:5]
    del refs[:5]
    mix_refs = [refs.pop(0) for _ in range(n_mix)]
    w_mix_ref = refs.pop(0) if n_mix else None
    next_g_ref, next_w_ref = (refs.pop(0), refs.pop(0)) if has_next else (None, None)
    final_g_ref = refs.pop(0) if has_final else None
    o_ref = refs.pop(0)
    next_o_ref = refs.pop(0) if has_next else None
    wg_bf, wu_bf, wd_bf = refs
    step = pl.program_id(0)

    @pl.when(step < FFN_LOAD_STEPS)
    def _():
        for src_ref, dst_ref in ((wg_ref, wg_bf), (wu_ref, wu_bf), (wd_ref, wd_bf)):
            rows = src_ref.shape[0]
            dst_ref[pl.ds(pl.multiple_of(step * rows, rows), rows), :] = src_ref[...].astype(BF16)

    @pl.when(step >= FFN_LOAD_STEPS)
    def _():
        x = x_ref[...]
        row = 0
        for a_ref in mix_refs:
            k = a_ref.shape[1]
            x = x + _dot(a_ref[...].astype(BF16), w_mix_ref[row:row + k, :])
            row += k
        h = _rms(x, g_ref[...]).astype(BF16)
        gate = _dot(h, wg_bf[...])
        up = _dot(h, wu_bf[...])
        act = gate / (1.0 + jnp.exp(-gate)) * up
        y = x + MACARON_WEIGHT * _dot(act.astype(BF16), wd_bf[...])
        if has_final:
            y = _rms(y, final_g_ref[...])
        o_ref[...] = y
        if has_next:
            next_o_ref[...] = _dot(_rms(y, next_g_ref[...]).astype(BF16), next_w_ref[...])


def _resident(shape):
    return pl.BlockSpec(shape, lambda *_: (0,) * len(shape), pipeline_mode=pl.Buffered(1))


def _streamed_rows(stack, index):
    lead = len(index)
    rows, cols = stack.shape[lead:]
    return pl.BlockSpec(
        (None,) * lead + (rows // FFN_LOAD_STEPS, cols),
        lambda i: tuple(index) + (jnp.minimum(i, FFN_LOAD_STEPS - 1), 0))


def _ffn(x, g, w_stacks, index, *, mix=(), w_mix=None, next_g=None, w_next=None, final_g=None,
         tm=512):
    m, d = x.shape
    tile = lambda i: jnp.maximum(i - FFN_LOAD_STEPS, 0)
    row_block = lambda a: pl.BlockSpec((tm, a.shape[1]), lambda i: (tile(i), 0))
    args = [x, g.reshape(1, d), *w_stacks]
    in_specs = [row_block(x), _resident((1, d))] + [_streamed_rows(w, index) for w in w_stacks]
    for a in mix:
        args.append(a)
        in_specs.append(row_block(a))
    if mix:
        args.append(w_mix)
        in_specs.append(_resident(w_mix.shape))
    if w_next is not None:
        args += [next_g.reshape(1, d), w_next]
        in_specs += [_resident((1, d)), _resident(w_next.shape)]
    if final_g is not None:
        args.append(final_g.reshape(1, d))
        in_specs.append(_resident((1, d)))
    out_shape = [jax.ShapeDtypeStruct((m, d), F32)]
    out_specs = [pl.BlockSpec((tm, d), lambda i: (tile(i), 0))]
    if w_next is not None:
        out_shape.append(jax.ShapeDtypeStruct((m, w_next.shape[1]), F32))
        out_specs.append(pl.BlockSpec((tm, w_next.shape[1]), lambda i: (tile(i), 0)))
    lead = len(index)
    outs = pl.pallas_call(
        functools.partial(_ffn_body, len(mix), w_next is not None, final_g is not None),
        out_shape=out_shape,
        grid=(FFN_LOAD_STEPS + m // tm,),
        in_specs=in_specs,
        out_specs=out_specs,
        scratch_shapes=[pltpu.VMEM(w.shape[lead:], BF16) for w in w_stacks],
        compiler_params=_cparams(("arbitrary",)),
        name="ffn",
    )(*args)
    return outs if w_next is not None else outs[0]


def _pool_body(seq_len, cur_ref, prev_ref, next_ref, pw_ref, ps_ref, yp_ref, ext_ref):
    i = pl.program_id(1)
    tl = cur_ref.shape[0]
    pool_width = yp_ref.shape[1]
    group = pool_width // len(POOL_WINDOWS)
    ext_ref[0:HALO, :] = jnp.where(i > 0, prev_ref[...], 0.0)
    ext_ref[HALO:HALO + tl, :] = cur_ref[...]
    ext_ref[HALO + tl:, :] = jnp.where(i < pl.num_programs(1) - 1, next_ref[...], 0.0)

    t = i * tl + lax.broadcasted_iota(jnp.int32, (tl, group), 0)
    for g, w in enumerate(POOL_WINDOWS):
        c0 = g * group
        lo = jnp.clip(t - w // 2, 0, seq_len)
        hi = jnp.clip(t - w // 2 + w, 0, seq_len)
        cnt = (hi - lo).astype(F32)
        s = ext_ref[HALO - w // 2:HALO - w // 2 + tl, c0:c0 + group]
        for d in range(1 - w // 2, w - w // 2):
            s = s + ext_ref[HALO + d:HALO + d + tl, c0:c0 + group]
        p = s / cnt - cur_ref[:, c0:c0 + group]
        y = _dot(p.astype(BF16), pw_ref[g])
        yp_ref[:, c0:c0 + group] = y * ps_ref[:, c0:c0 + group]


def _pool_mixer(proj, pool_w, pool_scale, *, tl=512):
    b, seq_len, _ = proj.shape
    pool_width = pool_scale.shape[0]
    hb = tl // HALO
    last = seq_len // HALO - 1
    return pl.pallas_call(
        functools.partial(_pool_body, seq_len),
        out_shape=jax.ShapeDtypeStruct((b, seq_len, pool_width), F32),
        grid=(b, seq_len // tl),
        in_specs=[
            pl.BlockSpec((None, tl, pool_width), lambda bi, i: (bi, i, 0)),
            pl.BlockSpec((None, HALO, pool_width),
                         lambda bi, i: (bi, jnp.maximum(i * hb - 1, 0), 0)),
            pl.BlockSpec((None, HALO, pool_width),
                         lambda bi, i: (bi, jnp.minimum((i + 1) * hb, last), 0)),
            pl.BlockSpec(pool_w.shape, lambda bi, i: (0, 0, 0)),
            pl.BlockSpec((1, pool_width), lambda bi, i: (0, 0)),
        ],
        out_specs=pl.BlockSpec((None, tl, pool_width), lambda bi, i: (bi, i, 0)),
        scratch_shapes=[pltpu.VMEM((tl + 2 * HALO, pool_width), F32)],
        compiler_params=_cparams(("parallel", "parallel")),
        name="pool_mixer",
    )(proj, proj, proj, pool_w, pool_scale.reshape(1, pool_width))


@functools.lru_cache(maxsize=None)
def _dft_constants():
    n = DFT_N1 * DFT_N2
    half = DFT_N1 // 2
    k1 = np.arange(DFT_K1)
    n1 = np.arange(half)
    ang = 2.0 * np.pi * np.outer(k1, n1) / DFT_N1
    f_fwd = np.zeros((2 * DFT_QP, half))
    f_fwd[:DFT_K1] = np.cos(ang)
    f_fwd[DFT_QP:DFT_QP + DFT_K1] = -np.sin(ang)
    n2 = np.arange(DFT_N2)
    k2 = np.arange(DFT_N2)
    g = np.zeros((DFT_K1, 2 * DFT_N2, 2 * DFT_N2))
    for a in range(DFT_K1):
        ph = 2.0 * np.pi * np.outer(a + DFT_N1 * k2, n2) / n
        gre, gim = np.cos(ph), -np.sin(ph)
        g[a] = np.block([[gre, -gim], [gim, gre]])
    weight = np.full(DFT_K1, 2.0)
    weight[0] = 1.0
    weight[-1] = 1.0
    f_inv = np.zeros((half, 2 * DFT_QP))
    f_inv[:, :DFT_K1] = weight * np.cos(ang.T) / n
    f_inv[:, DFT_QP:DFT_QP + DFT_K1] = -weight * np.sin(ang.T) / n
    return (f_fwd.astype(np.float32), g.astype(np.float32),
            np.transpose(g, (0, 2, 1)).astype(np.float32), f_inv.astype(np.float32))


def _pad_rows(dst_ref, src):
    for n1 in range(DFT_N1 // 2):
        dst_ref[n1 * DFT_PITCH:n1 * DFT_PITCH + DFT_N2, :] = src[n1 * DFT_N2:(n1 + 1) * DFT_N2, :]


def _contract_planes(src_ref, dst_ref, f_ref):
    n_dst, n_src = f_ref.shape

    def step(j, carry):
        r0 = j * PLANE_ROWS_PER_DOT
        x = jnp.concatenate(
            [src_ref[pl.ds(r0 + r, n_src, stride=DFT_PITCH), :].astype(BF16)
             for r in range(PLANE_ROWS_PER_DOT)], axis=1)
        y = _dot(f_ref[...], x)
        for r in range(PLANE_ROWS_PER_DOT):
            dst_ref[pl.ds(r0 + r, n_dst, stride=DFT_PITCH), :] = y[:, r * LANES:(r + 1) * LANES]
        return carry

    lax.fori_loop(0, DFT_N2 // PLANE_ROWS_PER_DOT, step, 0, unroll=PLANE_UNROLL)


def _load_planes(a_ref, k1):
    re = a_ref[pl.ds(pl.multiple_of(k1 * DFT_PITCH, SUBLANES), DFT_N2), :]
    im = a_ref[pl.ds(pl.multiple_of((DFT_QP + k1) * DFT_PITCH, SUBLANES), DFT_N2), :]
    return jnp.concatenate([re, im], axis=0)


def _store_planes(a_ref, k1, z):
    a_ref[pl.ds(pl.multiple_of(k1 * DFT_PITCH, SUBLANES), DFT_N2), :] = z[:DFT_N2]
    a_ref[pl.ds(pl.multiple_of((DFT_QP + k1) * DFT_PITCH, SUBLANES), DFT_N2), :] = z[DFT_N2:]


def _spectrum_body(fw_ref, bw_ref, ssf_ref, ssb_ref, ff_ref, g_ref, h_ref, u_ref, a_ref):
    norm = lax.rsqrt(ssf_ref[...] + ssb_ref[...])
    for part, src_ref in enumerate((fw_ref, bw_ref)):
        _pad_rows(u_ref, src_ref)
        _contract_planes(u_ref, a_ref, ff_ref)
        sign = 1.0 if part == 0 else -1.0

        def step(k1, carry):
            xh = _dot(g_ref[k1], _load_planes(a_ref, k1).astype(BF16))
            re = xh[:DFT_N2] * norm
            im = xh[DFT_N2:] * (sign * norm)
            if part == 0:
                h_ref[k1, :DFT_N2, :] = re
                h_ref[k1, DFT_N2:, :] = im
            else:
                h_ref[k1, :DFT_N2, :] += re
                h_ref[k1, DFT_N2:, :] += im
            return carry

        lax.fori_loop(0, DFT_K1, step, 0, unroll=K1_UNROLL)


def _filter_spectrum(taps, sumsq, width):
    seq_len = taps.shape[0]
    ff, g, _, _ = _dft_constants()
    cb = width // LANES
    half = DFT_N1 // 2
    return pl.pallas_call(
        _spectrum_body,
        out_shape=jax.ShapeDtypeStruct((HYENA_ORDER, DFT_K1, 2 * DFT_N2, width), F32),
        grid=(HYENA_ORDER, cb),
        in_specs=[
            pl.BlockSpec((seq_len, LANES), lambda o, c: (0, (2 * o) * cb + c)),
            pl.BlockSpec((seq_len, LANES), lambda o, c: (0, (2 * o + 1) * cb + c)),
            pl.BlockSpec((1, LANES), lambda o, c: (0, (2 * o) * cb + c)),
            pl.BlockSpec((1, LANES), lambda o, c: (0, (2 * o + 1) * cb + c)),
            pl.BlockSpec((2 * DFT_QP, half), lambda o, c: (0, 0)),
            pl.BlockSpec((DFT_K1, 2 * DFT_N2, 2 * DFT_N2), lambda o, c: (0, 0, 0)),
        ],
        out_specs=pl.BlockSpec((None, DFT_K1, 2 * DFT_N2, LANES), lambda o, c: (o, 0, 0, c)),
        scratch_shapes=[pltpu.VMEM((half * DFT_PITCH, LANES), F32),
                        pltpu.VMEM((2 * DFT_QP * DFT_PITCH, LANES), F32)],
        compiler_params=_cparams(("parallel", "parallel")),
        name="filter_spectrum",
    )(taps, taps, sumsq, sumsq, jnp.asarray(ff).astype(BF16), jnp.asarray(g).astype(BF16))


def _short_conv_rows(x_ref, n1, w_ref, b_ref):
    r0 = n1 * DFT_N2
    cur = x_ref[r0:r0 + DFT_N2, :]
    row = lax.broadcasted_iota(jnp.int32, cur.shape, 0)
    if r0 == 0:
        prev = jnp.where(row == 0, 0.0, pltpu.roll(cur, 1, axis=0))
    else:
        prev = x_ref[r0 - 1:r0 + DFT_N2 - 1, :]
    if r0 + DFT_N2 == x_ref.shape[0]:
        nxt = jnp.where(row == DFT_N2 - 1, 0.0, pltpu.roll(cur, DFT_N2 - 1, axis=0))
    else:
        nxt = x_ref[r0 + 1:r0 + DFT_N2 + 1, :]
    return b_ref[...] + prev * w_ref[0:1, :] + cur * w_ref[1:2, :] + nxt * w_ref[2:3, :]


def _hyena_body(v_ref, gate_ref, vw_ref, vb_ref, gw_ref, gb_ref, h_ref, bias_ref, ff_ref, g_ref,
                gt_ref, fi_ref, o_ref, u_ref, y_ref, a_ref):
    order = pl.program_id(2)
    half = DFT_N1 // 2

    @pl.when(order == 0)
    def _():
        for n1 in range(half):
            u_ref[n1 * DFT_PITCH:n1 * DFT_PITCH + DFT_N2, :] = _short_conv_rows(
                v_ref, n1, vw_ref, vb_ref)

    _contract_planes(u_ref, a_ref, ff_ref)

    def freq_step(k1, carry):
        xh = _dot(g_ref[k1], _load_planes(a_ref, k1).astype(BF16))
        xre, xim = xh[:DFT_N2], xh[DFT_N2:]
        hre, him = h_ref[k1, :DFT_N2, :], h_ref[k1, DFT_N2:, :]
        yh = jnp.concatenate([xre * hre - xim * him, xre * him + xim * hre], axis=0)
        _store_planes(a_ref, k1, _dot(gt_ref[k1], yh.astype(BF16)))
        return carry

    lax.fori_loop(0, DFT_K1, freq_step, 0, unroll=K1_UNROLL)
    _contract_planes(a_ref, y_ref, fi_ref)

    bias = bias_ref[...]
    for n1 in range(half):
        rows = slice(n1 * DFT_PITCH, n1 * DFT_PITCH + DFT_N2)
        gate = _short_conv_rows(gate_ref, n1, gw_ref, gb_ref)
        z = gate * (y_ref[rows, :] + u_ref[rows, :] * bias)
        u_ref[rows, :] = z
        o_ref[n1 * DFT_N2:(n1 + 1) * DFT_N2, :] = z


def _hyena_mixer(proj, first_col, conv_w, conv_b, spec, bias):
    assert HYENA_SHORT_CONV == 3
    b, seq_len, _ = proj.shape
    width = bias.shape[1]
    cb = width // LANES
    c0 = first_col // LANES
    half = DFT_N1 // 2
    ff, g, gt, fi = _dft_constants()
    value_col = lambda c, bi, o: HYENA_ORDER * cb + c
    gate_col = lambda c, bi, o: o * cb + c
    conv_b = conv_b.reshape(1, -1)
    return pl.pallas_call(
        _hyena_body,
        out_shape=jax.ShapeDtypeStruct((b, seq_len, width), F32),
        grid=(cb, b, HYENA_ORDER),
        in_specs=[
            pl.BlockSpec((None, seq_len, LANES), lambda c, bi, o: (bi, 0, c0 + value_col(c, bi, o))),
            pl.BlockSpec((None, seq_len, LANES), lambda c, bi, o: (bi, 0, c0 + gate_col(c, bi, o))),
            pl.BlockSpec((HYENA_SHORT_CONV, LANES), lambda c, bi, o: (0, value_col(c, bi, o))),
            pl.BlockSpec((1, LANES), lambda c, bi, o: (0, value_col(c, bi, o))),
            pl.BlockSpec((HYENA_SHORT_CONV, LANES), lambda c, bi, o: (0, gate_col(c, bi, o))),
            pl.BlockSpec((1, LANES), lambda c, bi, o: (0, gate_col(c, bi, o))),
            pl.BlockSpec((None, DFT_K1, 2 * DFT_N2, LANES), lambda c, bi, o: (o, 0, 0, c)),
            pl.BlockSpec((None, 1, LANES), lambda c, bi, o: (o, 0, c)),
            _resident((2 * DFT_QP, half)),
            _resident((DFT_K1, 2 * DFT_N2, 2 * DFT_N2)),
            _resident((DFT_K1, 2 * DFT_N2, 2 * DFT_N2)),
            _resident((half, 2 * DFT_QP)),
        ],
        out_specs=pl.BlockSpec((None, seq_len, LANES), lambda c, bi, o: (bi, 0, c)),
        scratch_shapes=[pltpu.VMEM((half * DFT_PITCH, LANES), F32),
                        pltpu.VMEM((half * DFT_PITCH, LANES), F32),
                        pltpu.VMEM((2 * DFT_QP * DFT_PITCH, LANES), F32)],
        compiler_params=_cparams(("parallel", "parallel", "arbitrary")),
        name="hyena_mixer",
    )(proj, proj, conv_w, conv_b, conv_w, conv_b, spec, bias.reshape(HYENA_ORDER, 1, width),
      jnp.asarray(ff).astype(BF16), jnp.asarray(g).astype(BF16), jnp.asarray(gt).astype(BF16),
      jnp.asarray(fi).astype(BF16))


def _hdot(a, b):
    return jnp.dot(a, b, preferred_element_type=F32, precision=lax.Precision.HIGHEST)


def _filter_taps_body(bwd_cols, z_ref, t_ref, w1_ref, b1_ref, w2_ref, b2_ref, w3_ref, b3_ref,
                      sf_ref, wo_ref, decay_ref, taps_ref, ss_ref):
    i = pl.program_id(0)
    h = jnp.sin(sf_ref[0:1, :] * (_hdot(z_ref[...], w1_ref[...]) + b1_ref[...]))
    h = jnp.sin(sf_ref[1:2, :] * (_hdot(h, w2_ref[...]) + b2_ref[...]))
    h = jnp.sin(sf_ref[2:3, :] * (_hdot(h, w3_ref[...]) + b3_ref[...]))
    h = _dot(h.astype(BF16), wo_ref[...])
    h = h * jnp.exp(-t_ref[...] * jnp.abs(decay_ref[...]))
    row = i * h.shape[0] + lax.broadcasted_iota(jnp.int32, h.shape, 0)
    h = jnp.where((row == 0) & (bwd_cols[...] > 0.0), 0.0, h)
    taps_ref[...] = h

    @pl.when(i == 0)
    def _():
        ss_ref[...] = jnp.zeros_like(ss_ref)

    ss_ref[...] += jnp.sum(h * h, axis=0, keepdims=True)


def _filter_taps(z, t, w1, b1, w2, b2, w3, b3, sin_freq, w_out, decay, *, tl=512):
    seq_len = z.shape[0]
    hid = LANES
    n_out = w_out.shape[1]
    width = n_out // (2 * HYENA_ORDER)

    def pad2(a, rows, cols):
        return jnp.pad(a, ((0, rows - a.shape[0]), (0, cols - a.shape[1])))

    is_bwd = np.tile(np.repeat(np.array([0.0, 1.0], np.float32), width), HYENA_ORDER)[None, :]
    full = lambda a: pl.BlockSpec(a.shape, lambda i: (0, 0))
    args = (
        pad2(z, seq_len, hid), t.reshape(seq_len, 1),
        pad2(w1, hid, hid), pad2(b1[None, :], 1, hid),
        pad2(w2, hid, hid), pad2(b2[None, :], 1, hid),
        pad2(w3, hid, hid), pad2(b3[None, :], 1, hid),
        pad2(sin_freq, 3, hid), pad2(w_out, hid, n_out).astype(BF16), decay.reshape(1, n_out),
    )
    bwd_cols = jnp.asarray(is_bwd)
    return pl.pallas_call(
        _filter_taps_body,
        out_shape=(jax.ShapeDtypeStruct((seq_len, n_out), F32),
                   jax.ShapeDtypeStruct((1, n_out), F32)),
        grid=(seq_len // tl,),
        in_specs=[full(bwd_cols),
                  pl.BlockSpec((tl, hid), lambda i: (i, 0)),
                  pl.BlockSpec((tl, 1), lambda i: (i, 0))] + [full(a) for a in args[2:]],
        out_specs=(pl.BlockSpec((tl, n_out), lambda i: (i, 0)),
                   pl.BlockSpec((1, n_out), lambda i: (0, 0))),
        compiler_params=_cparams(("arbitrary",)),
        name="filter_taps",
    )(bwd_cols, *args)


def _mla_qkv_body(c_ref, qg_ref, kvg_ref, wq_ref, wqr_ref, wk_ref, wv_ref, cos_ref, sin_ref,
                  q_ref, k_ref, v_ref):
    c = c_ref[...]
    cos = cos_ref[...]
    sin = sin_ref[...]
    cq = _rms(c[:, :MLA_Q_LORA], qg_ref[...]).astype(BF16)
    ckv = _rms(c[:, MLA_Q_LORA:MLA_Q_LORA + MLA_KV_LORA], kvg_ref[...]).astype(BF16)
    kr0 = MLA_Q_LORA + MLA_KV_LORA
    k_rope = c[:, kr0:kr0 + LANES] * cos + c[:, kr0 + LANES:kr0 + 2 * LANES] * sin
    ones_lane = (lax.broadcasted_iota(jnp.int32, (1, LANES), 1) == MLA_V).astype(F32)
    for h in range(MLA_HEADS):
        cols = slice(h * LANES, (h + 1) * LANES)
        q = _dot(cq, wq_ref[:, cols]) * cos + _dot(cq, wqr_ref[:, cols]) * sin
        q_ref[h] = (q * SOFTMAX_EXP2_SCALE).astype(BF16)
        k_ref[h] = (_dot(ckv, wk_ref[:, cols]) + k_rope).T.astype(BF16)
        v_ref[h] = (_dot(ckv, wv_ref[:, cols]) + ones_lane).astype(BF16)


def _mla_qkv(c, q_norm_g, kv_norm_g, wq, wq_rot, wk, wv, cos_t, sin_t, batch, *, tl=512):
    m, cw = c.shape
    seq_len = m // batch
    nl = seq_len // tl
    full = lambda a: pl.BlockSpec(a.shape, lambda bi, i: (0, 0))
    qg = q_norm_g.reshape(1, -1)
    kvg = kv_norm_g.reshape(1, -1)
    head_out = jax.ShapeDtypeStruct((batch, MLA_HEADS, seq_len, LANES), BF16)
    head_spec = pl.BlockSpec((None, MLA_HEADS, tl, LANES), lambda bi, i: (bi, 0, i, 0))
    return pl.pallas_call(
        _mla_qkv_body,
        out_shape=(head_out, jax.ShapeDtypeStruct((batch, MLA_HEADS, LANES, seq_len), BF16),
                   head_out),
        grid=(batch, nl),
        in_specs=[pl.BlockSpec((tl, cw), lambda bi, i: (bi * nl + i, 0)),
                  full(qg), full(kvg), full(wq), full(wq_rot), full(wk), full(wv),
                  pl.BlockSpec((tl, LANES), lambda bi, i: (i, 0)),
                  pl.BlockSpec((tl, LANES), lambda bi, i: (i, 0))],
        out_specs=(head_spec,
                   pl.BlockSpec((None, MLA_HEADS, LANES, tl), lambda bi, i: (bi, 0, 0, i)),
                   head_spec),
        compiler_params=_cparams(("parallel", "parallel")),
        name="mla_qkv",
    )(c, qg, kvg, wq, wq_rot, wk, wv, cos_t, sin_t)


def _attention_body(q_ref, qn_ref, k_ref, v_ref, o_ref, *s_refs):
    pair = LANES // MLA_V
    n_heads = q_ref.shape[0]

    @pl.when(pl.program_id(2) == 0)
    def _():
        for h, s_ref in enumerate(s_refs):
            s_ref[...] = _dot(q_ref[h], k_ref[h])

    ahead = {h: s_ref[...] for h, s_ref in enumerate(s_refs)}
    for g in range(n_heads // pair):
        outs = []
        for h in range(g * pair, (g + 1) * pair):
            lead = h + SCORE_LEAD
            if lead < n_heads:
                ahead[lead] = _dot(q_ref[lead], k_ref[lead])
            else:
                s_refs[lead - n_heads][...] = _dot(qn_ref[lead - n_heads], k_ref[lead - n_heads])
            s = ahead.pop(h)
            p = jnp.exp2(s - jnp.max(s, axis=-1, keepdims=True))
            r = _dot(p.astype(BF16), v_ref[h])
            outs.append(r / r[:, MLA_V:MLA_V + 1])
        lane = lax.broadcasted_iota(jnp.int32, outs[0].shape, 1)
        both = jnp.where(lane < MLA_V, outs[0], pltpu.roll(outs[1], MLA_V, axis=1))
        o_ref[:, g * LANES:(g + 1) * LANES] = both.astype(o_ref.dtype)


def _attention(q, k, v, *, tq=256, group=8):
    batch, heads, seq_len, _ = q.shape
    width = group * MLA_V
    n_tiles = seq_len // tq
    return pl.pallas_call(
        _attention_body,
        out_shape=jax.ShapeDtypeStruct((batch, seq_len, heads * MLA_V), BF16),
        grid=(batch, heads // group, n_tiles),
        in_specs=[
            pl.BlockSpec((None, group, tq, LANES), lambda bi, p, i: (bi, p, i, 0)),
            pl.BlockSpec((None, SCORE_LEAD, tq, LANES),
                         lambda bi, p, i: (bi, p * (group // SCORE_LEAD),
                                           jnp.minimum(i + 1, n_tiles - 1), 0)),
            pl.BlockSpec((None, group, LANES, seq_len), lambda bi, p, i: (bi, p, 0, 0)),
            pl.BlockSpec((None, group, seq_len, LANES), lambda bi, p, i: (bi, p, 0, 0)),
        ],
        out_specs=pl.BlockSpec((None, tq, width), lambda bi, p, i: (bi, i, p)),
        scratch_shapes=[pltpu.VMEM((tq, seq_len), F32)] * SCORE_LEAD,
        compiler_params=_cparams(("parallel", "parallel", "arbitrary")),
        name="attention",
    )(q, q, k, v)


def _rope_rotation():
    half = MLA_ROPE // 2
    r = np.zeros((MLA_ROPE, MLA_ROPE), np.float32)
    for j in range(half):
        r[j + half, j] = -1.0
        r[j, j + half] = 1.0
    return jnp.asarray(r)


def _mla_weights(w_dq, w_uq, w_dkv, w_ukv):
    d = w_dq.shape[0]
    rot = _rope_rotation()
    qh = w_uq.reshape(MLA_Q_LORA, MLA_HEADS, MLA_NOPE + MLA_ROPE)
    zq = jnp.zeros((MLA_Q_LORA, MLA_HEADS, LANES - MLA_NOPE - MLA_ROPE), F32)
    wq = jnp.concatenate([qh, zq], axis=-1).reshape(MLA_Q_LORA, MLA_HEADS * LANES)
    q_rot = jnp.einsum("lhr,rs->lhs", qh[..., MLA_NOPE:], rot)
    wq_rot = jnp.concatenate([jnp.zeros((MLA_Q_LORA, MLA_HEADS, MLA_NOPE), F32), q_rot, zq],
                             axis=-1).reshape(MLA_Q_LORA, MLA_HEADS * LANES)
    kvh = w_ukv.reshape(MLA_KV_LORA, MLA_HEADS, MLA_NOPE + MLA_V)
    wk = jnp.concatenate([kvh[..., :MLA_NOPE],
                          jnp.zeros((MLA_KV_LORA, MLA_HEADS, LANES - MLA_NOPE), F32)],
                         axis=-1).reshape(MLA_KV_LORA, MLA_HEADS * LANES)
    wv = jnp.concatenate([kvh[..., MLA_NOPE:],
                          jnp.zeros((MLA_KV_LORA, MLA_HEADS, LANES - MLA_V), F32)],
                         axis=-1).reshape(MLA_KV_LORA, MLA_HEADS * LANES)
    w_kr = w_dkv[:, MLA_KV_LORA:]

    def slab(w):
        return jnp.concatenate([jnp.zeros((d, MLA_NOPE), F32), w,
                                jnp.zeros((d, LANES - MLA_NOPE - MLA_ROPE), F32)], axis=-1)

    w_down = jnp.concatenate([w_dq, w_dkv[:, :MLA_KV_LORA], slab(w_kr), slab(w_kr @ rot)], axis=-1)
    return (w_down.astype(BF16), wq.astype(BF16), wq_rot.astype(BF16), wk.astype(BF16),
            wv.astype(BF16))


def _rope_tables(seq_len):
    inv_freq = ROPE_THETA ** (-jnp.arange(0, MLA_ROPE, 2, dtype=F32) / MLA_ROPE)
    ang = jnp.arange(seq_len, dtype=F32)[:, None] * inv_freq[None, :]
    cos = jnp.cos(ang)
    sin = jnp.sin(ang)
    tail = LANES - MLA_NOPE - MLA_ROPE
    cos_t = jnp.concatenate([jnp.ones((seq_len, MLA_NOPE), F32), cos, cos,
                             jnp.ones((seq_len, tail), F32)], axis=-1)
    sin_t = jnp.concatenate([jnp.zeros((seq_len, MLA_NOPE), F32), sin, sin,
                             jnp.zeros((seq_len, tail), F32)], axis=-1)
    return cos_t, sin_t


def _position_features(seq_len, pos_dim):
    t = jnp.linspace(0.0, 1.0, seq_len, dtype=F32)
    bands = (pos_dim - 1) // 2
    w = 2.0 * math.pi * jnp.arange(seq_len, dtype=F32) / seq_len
    f = jnp.linspace(1e-4, bands - 1, bands, dtype=F32)
    phase = w[:, None] * f[None, :]
    z = jnp.concatenate([t[:, None], jnp.cos(phase), -jnp.sin(phase)], axis=-1)
    return t, z


def kernel(x, norm_g, ffn_w_gate, ffn_w_up, ffn_w_down, mix_w_in, pool_w, pool_scale, hyena_conv_w, hyena_conv_b, hyena_ffn_w1, hyena_ffn_b1, hyena_ffn_w2, hyena_ffn_b2, hyena_ffn_w3, hyena_ffn_b3, hyena_sin_freq, hyena_ffn_w_out, hyena_decay, hyena_bias, mix_w_out, mla_w_dq, mla_q_norm_g, mla_w_uq, mla_w_dkv, mla_kv_norm_g, mla_w_ukv, mla_w_o, final_norm_g):
    batch, seq_len, d = x.shape
    depth = norm_g.shape[0]
    assert 2 * seq_len == DFT_N1 * DFT_N2
    hy_width = hyena_bias.shape[-1]
    t_pos, z_pos = _position_features(seq_len, hyena_ffn_w1.shape[1])
    cos_t, sin_t = _rope_tables(seq_len)

    x = x.reshape(batch * seq_len, d)
    ffn_w = (ffn_w_gate, ffn_w_up, ffn_w_down)
    for i in range(depth):
        j = i // 2
        if i % 2 == 0:
            x, proj = _ffn(x, norm_g[i, 0], ffn_w, (i, 0), next_g=norm_g[i, 1],
                           w_next=mix_w_in[j].astype(BF16))
            proj = proj.reshape(batch, seq_len, -1)
            y_pool = _pool_mixer(proj, pool_w[j].astype(BF16), pool_scale[j])
            taps, sumsq = _filter_taps(z_pos, t_pos, hyena_ffn_w1[j], hyena_ffn_b1[j],
                                       hyena_ffn_w2[j], hyena_ffn_b2[j], hyena_ffn_w3[j],
                                       hyena_ffn_b3[j], hyena_sin_freq[j], hyena_ffn_w_out[j],
                                       hyena_decay[j])
            spec = _filter_spectrum(taps, sumsq, hy_width)
            y_hyena = _hyena_mixer(proj, pool_scale.shape[-1], hyena_conv_w[j], hyena_conv_b[j],
                                   spec, hyena_bias[j])
            mix = (y_pool.reshape(batch * seq_len, -1), y_hyena.reshape(batch * seq_len, -1))
            w_mix = mix_w_out[j].astype(BF16)
        else:
            w_down, wq, wq_rot, wk, wv = _mla_weights(mla_w_dq[j], mla_w_uq[j], mla_w_dkv[j],
                                                      mla_w_ukv[j])
            x, c = _ffn(x, norm_g[i, 0], ffn_w, (i, 0), next_g=norm_g[i, 1], w_next=w_down)
            q, k, v = _mla_qkv(c, mla_q_norm_g[j], mla_kv_norm_g[j], wq, wq_rot, wk, wv,
                               cos_t, sin_t, batch)
            mix = (_attention(q, k, v).reshape(batch * seq_len, -1),)
            w_mix = mla_w_o[j].astype(BF16)
        x = _ffn(x, norm_g[i, 2], ffn_w, (i, 1), mix=mix, w_mix=w_mix,
                 final_g=final_norm_g if i == depth - 1 else None)
    return x.reshape(batch, seq_len, d)
```

```python
import functools
import math

import numpy as np
import jax
import jax.numpy as jnp
from jax import lax
from jax.experimental import pallas as pl
from jax.experimental.pallas import tpu as pltpu

F32 = jnp.float32
BF16 = jnp.bfloat16

RMS_EPS = 1e-6
MACARON_WEIGHT = 0.5
POOL_WINDOWS = (2, 4, 8, 16)
HYENA_ORDER = 2
HYENA_SHORT_CONV = 3
MLA_HEADS = 16
MLA_Q_LORA = 256
MLA_KV_LORA = 128
MLA_NOPE = 64
MLA_ROPE = 32
MLA_V = 64
ROPE_THETA = 10000.0
SOFTMAX_EXP2_SCALE = (MLA_NOPE + MLA_ROPE) ** -0.5 * math.log2(math.e)

LANES = 128
SUBLANES = 8
VMEM_LIMIT = 56 * 2**20

DFT_N1 = 64
DFT_N2 = 128
DFT_K1 = DFT_N1 // 2 + 1
DFT_QP = 40
DFT_PITCH = DFT_N2 + SUBLANES
PLANE_ROWS_PER_DOT = 8
PLANE_UNROLL = 4
K1_UNROLL = 11
HALO = 16
FFN_LOAD_STEPS = 8
SCORE_LEAD = 2


def _cparams(semantics):
    return pltpu.CompilerParams(dimension_semantics=semantics, vmem_limit_bytes=VMEM_LIMIT)


def _rms(x, g):
    return x * lax.rsqrt(jnp.mean(x * x, axis=-1, keepdims=True) + RMS_EPS) * g


def _dot(a, b):
    return jnp.dot(a, b, preferred_element_type=F32)


def _ffn_body(n_mix, has_next, has_final, *refs):
    refs = list(refs)
    x_ref, g_ref, wg_ref, wu_ref, wd_ref = refs[:5]
    del refs[:5]
    mix_refs = [refs.pop(0) for _ in range(n_mix)]
    w_mix_ref = refs.pop(0) if n_mix else None
    next_g_ref, next_w_ref = (refs.pop(0), refs.pop(0)) if has_next else (None, None)
    final_g_ref = refs.pop(0) if has_final else None
    o_ref = refs.pop(0)
    next_o_ref = refs.pop(0) if has_next else None
    wg_bf, wu_bf, wd_bf = refs
    step = pl.program_id(0)

    @pl.when(step < FFN_LOAD_STEPS)
    def _():
        for src_ref, dst_ref in ((wg_ref, wg_bf), (wu_ref, wu_bf), (wd_ref, wd_bf)):
            rows = src_ref.shape[0]
            dst_ref[pl.ds(pl.multiple_of(step * rows, rows), rows), :] = src_ref[...].astype(BF16)

    @pl.when(step >= FFN_LOAD_STEPS)
    def _():
        x = x_ref[...]
        row = 0
        for a_ref in mix_refs:
            k = a_ref.shape[1]
            x = x + _dot(a_ref[...].astype(BF16), w_mix_ref[row:row + k, :])
            row += k
        h = _rms(x, g_ref[...]).astype(BF16)
        gate = _dot(h, wg_bf[...])
        up = _dot(h, wu_bf[...])
        act = gate / (1.0 + jnp.exp(-gate)) * up
        y = x + MACARON_WEIGHT * _dot(act.astype(BF16), wd_bf[...])
        if has_final:
            y = _rms(y, final_g_ref[...])
        o_ref[...] = y
        if has_next:
            next_o_ref[...] = _dot(_rms(y, next_g_ref[...]).astype(BF16), next_w_ref[...])


def _resident(shape):
    return pl.BlockSpec(shape, lambda *_: (0,) * len(shape), pipeline_mode=pl.Buffered(1))


def _streamed_rows(stack, index):
    lead = len(index)
    rows, cols = stack.shape[lead:]
    return pl.BlockSpec(
        (None,) * lead + (rows // FFN_LOAD_STEPS, cols),
        lambda i: tuple(index) + (jnp.minimum(i, FFN_LOAD_STEPS - 1), 0))


def _ffn(x, g, w_stacks, index, *, mix=(), w_mix=None, next_g=None, w_next=None, final_g=None,
         tm=512):
    m, d = x.shape
    tile = lambda i: jnp.maximum(i - FFN_LOAD_STEPS, 0)
    row_block = lambda a: pl.BlockSpec((tm, a.shape[1]), lambda i: (tile(i), 0))
    args = [x, g.reshape(1, d), *w_stacks]
    in_specs = [row_block(x), _resident((1, d))] + [_streamed_rows(w, index) for w in w_stacks]
    for a in mix:
        args.append(a)
        in_specs.append(row_block(a))
    if mix:
        args.append(w_mix)
        in_specs.append(_resident(w_mix.shape))
    if w_next is not None:
        args += [next_g.reshape(1, d), w_next]
        in_specs += [_resident((1, d)), _resident(w_next.shape)]
    if final_g is not None:
        args.append(final_g.reshape(1, d))
        in_specs.append(_resident((1, d)))
    out_shape = [jax.ShapeDtypeStruct((m, d), F32)]
    out_specs = [pl.BlockSpec((tm, d), lambda i: (tile(i), 0))]
    if w_next is not None:
        out_shape.append(jax.ShapeDtypeStruct((m, w_next.shape[1]), F32))
        out_specs.append(pl.BlockSpec((tm, w_next.shape[1]), lambda i: (tile(i), 0)))
    lead = len(index)
    outs = pl.pallas_call(
        functools.partial(_ffn_body, len(mix), w_next is not None, final_g is not None),
        out_shape=out_shape,
        grid=(FFN_LOAD_STEPS + m // tm,),
        in_specs=in_specs,
        out_specs=out_specs,
        scratch_shapes=[pltpu.VMEM(w.shape[lead:], BF16) for w in w_stacks],
        compiler_params=_cparams(("arbitrary",)),
        name="ffn",
    )(*args)
    return outs if w_next is not None else outs[0]


def _pool_body(seq_len, cur_ref, prev_ref, next_ref, pw_ref, ps_ref, yp_ref, ext_ref):
    i = pl.program_id(1)
    tl = cur_ref.shape[0]
    pool_width = yp_ref.shape[1]
    group = pool_width // len(POOL_WINDOWS)
    ext_ref[0:HALO, :] = jnp.where(i > 0, prev_ref[...], 0.0)
    ext_ref[HALO:HALO + tl, :] = cur_ref[...]
    ext_ref[HALO + tl:, :] = jnp.where(i < pl.num_programs(1) - 1, next_ref[...], 0.0)

    t = i * tl + lax.broadcasted_iota(jnp.int32, (tl, group), 0)
    for g, w in enumerate(POOL_WINDOWS):
        c0 = g * group
        lo = jnp.clip(t - w // 2, 0, seq_len)
        hi = jnp.clip(t - w // 2 + w, 0, seq_len)
        cnt = (hi - lo).astype(F32)
        s = ext_ref[HALO - w // 2:HALO - w // 2 + tl, c0:c0 + group]
        for d in range(1 - w // 2, w - w // 2):
            s = s + ext_ref[HALO + d:HALO + d + tl, c0:c0 + group]
        p = s / cnt - cur_ref[:, c0:c0 + group]
        y = _dot(p.astype(BF16), pw_ref[g])
        yp_ref[:, c0:c0 + group] = y * ps_ref[:, c0:c0 + group]


def _pool_mixer(proj, pool_w, pool_scale, *, tl=512):
    b, seq_len, _ = proj.shape
    pool_width = pool_scale.shape[0]
    hb = tl // HALO
    last = seq_len // HALO - 1
    return pl.pallas_call(
        functools.partial(_pool_body, seq_len),
        out_shape=jax.ShapeDtypeStruct((b, seq_len, pool_width), F32),
        grid=(b, seq_len // tl),
        in_specs=[
            pl.BlockSpec((None, tl, pool_width), lambda bi, i: (bi, i, 0)),
            pl.BlockSpec((None, HALO, pool_width),
                         lambda bi, i: (bi, jnp.maximum(i * hb - 1, 0), 0)),
            pl.BlockSpec((None, HALO, pool_width),
                         lambda bi, i: (bi, jnp.minimum((i + 1) * hb, last), 0)),
            pl.BlockSpec(pool_w.shape, lambda bi, i: (0, 0, 0)),
            pl.BlockSpec((1, pool_width), lambda bi, i: (0, 0)),
        ],
        out_specs=pl.BlockSpec((None, tl, pool_width), lambda bi, i: (bi, i, 0)),
        scratch_shapes=[pltpu.VMEM((tl + 2 * HALO, pool_width), F32)],
        compiler_params=_cparams(("parallel", "parallel")),
        name="pool_mixer",
    )(proj, proj, proj, pool_w, pool_scale.reshape(1, pool_width))


@functools.lru_cache(maxsize=None)
def _dft_constants():
    n = DFT_N1 * DFT_N2
    half = DFT_N1 // 2
    k1 = np.arange(DFT_K1)
    n1 = np.arange(half)
    ang = 2.0 * np.pi * np.outer(k1, n1) / DFT_N1
    f_fwd = np.zeros((2 * DFT_QP, half))
    f_fwd[:DFT_K1] = np.cos(ang)
    f_fwd[DFT_QP:DFT_QP + DFT_K1] = -np.sin(ang)
    n2 = np.arange(DFT_N2)
    k2 = np.arange(DFT_N2)
    g = np.zeros((DFT_K1, 2 * DFT_N2, 2 * DFT_N2))
    for a in range(DFT_K1):
        ph = 2.0 * np.pi * np.outer(a + DFT_N1 * k2, n2) / n
        gre, gim = np.cos(ph), -np.sin(ph)
        g[a] = np.block([[gre, -gim], [gim, gre]])
    weight = np.full(DFT_K1, 2.0)
    weight[0] = 1.0
    weight[-1] = 1.0
    f_inv = np.zeros((half, 2 * DFT_QP))
    f_inv[:, :DFT_K1] = weight * np.cos(ang.T) / n
    f_inv[:, DFT_QP:DFT_QP + DFT_K1] = -weight * np.sin(ang.T) / n
    return (f_fwd.astype(np.float32), g.astype(np.float32),
            np.transpose(g, (0, 2, 1)).astype(np.float32), f_inv.astype(np.float32))


def _pad_rows(dst_ref, src):
    for n1 in range(DFT_N1 // 2):
        dst_ref[n1 * DFT_PITCH:n1 * DFT_PITCH + DFT_N2, :] = src[n1 * DFT_N2:(n1 + 1) * DFT_N2, :]


def _contract_planes(src_ref, dst_ref, f_ref):
    n_dst, n_src = f_ref.shape

    def step(j, carry):
        r0 = j * PLANE_ROWS_PER_DOT
        x = jnp.concatenate(
            [src_ref[pl.ds(r0 + r, n_src, stride=DFT_PITCH), :].astype(BF16)
             for r in range(PLANE_ROWS_PER_DOT)], axis=1)
        y = _dot(f_ref[...], x)
        for r in range(PLANE_ROWS_PER_DOT):
            dst_ref[pl.ds(r0 + r, n_dst, stride=DFT_PITCH), :] = y[:, r * LANES:(r + 1) * LANES]
        return carry

    lax.fori_loop(0, DFT_N2 // PLANE_ROWS_PER_DOT, step, 0, unroll=PLANE_UNROLL)


def _load_planes(a_ref, k1):
    re = a_ref[pl.ds(pl.multiple_of(k1 * DFT_PITCH, SUBLANES), DFT_N2), :]
    im = a_ref[pl.ds(pl.multiple_of((DFT_QP + k1) * DFT_PITCH, SUBLANES), DFT_N2), :]
    return jnp.concatenate([re, im], axis=0)


def _store_planes(a_ref, k1, z):
    a_ref[pl.ds(pl.multiple_of(k1 * DFT_PITCH, SUBLANES), DFT_N2), :] = z[:DFT_N2]
    a_ref[pl.ds(pl.multiple_of((DFT_QP + k1) * DFT_PITCH, SUBLANES), DFT_N2), :] = z[DFT_N2:]


def _spectrum_body(fw_ref, bw_ref, ssf_ref, ssb_ref, ff_ref, g_ref, h_ref, u_ref, a_ref):
    norm = lax.rsqrt(ssf_ref[...] + ssb_ref[...])
    for part, src_ref in enumerate((fw_ref, bw_ref)):
        _pad_rows(u_ref, src_ref)
        _contract_planes(u_ref, a_ref, ff_ref)
        sign = 1.0 if part == 0 else -1.0

        def step(k1, carry):
            xh = _dot(g_ref[k1], _load_planes(a_ref, k1).astype(BF16))
            re = xh[:DFT_N2] * norm
            im = xh[DFT_N2:] * (sign * norm)
            if part == 0:
                h_ref[k1, :DFT_N2, :] = re
                h_ref[k1, DFT_N2:, :] = im
            else:
                h_ref[k1, :DFT_N2, :] += re
                h_ref[k1, DFT_N2:, :] += im
            return carry

        lax.fori_loop(0, DFT_K1, step, 0, unroll=K1_UNROLL)


def _filter_spectrum(taps, sumsq, width):
    seq_len = taps.shape[0]
    ff, g, _, _ = _dft_constants()
    cb = width // LANES
    half = DFT_N1 // 2
    return pl.pallas_call(
        _spectrum_body,
        out_shape=jax.ShapeDtypeStruct((HYENA_ORDER, DFT_K1, 2 * DFT_N2, width), F32),
        grid=(HYENA_ORDER, cb),
        in_specs=[
            pl.BlockSpec((seq_len, LANES), lambda o, c: (0, (2 * o) * cb + c)),
            pl.BlockSpec((seq_len, LANES), lambda o, c: (0, (2 * o + 1) * cb + c)),
            pl.BlockSpec((1, LANES), lambda o, c: (0, (2 * o) * cb + c)),
            pl.BlockSpec((1, LANES), lambda o, c: (0, (2 * o + 1) * cb + c)),
            pl.BlockSpec((2 * DFT_QP, half), lambda o, c: (0, 0)),
            pl.BlockSpec((DFT_K1, 2 * DFT_N2, 2 * DFT_N2), lambda o, c: (0, 0, 0)),
        ],
        out_specs=pl.BlockSpec((None, DFT_K1, 2 * DFT_N2, LANES), lambda o, c: (o, 0, 0, c)),
        scratch_shapes=[pltpu.VMEM((half * DFT_PITCH, LANES), F32),
                        pltpu.VMEM((2 * DFT_QP * DFT_PITCH, LANES), F32)],
        compiler_params=_cparams(("parallel", "parallel")),
        name="filter_spectrum",
    )(taps, taps, sumsq, sumsq, jnp.asarray(ff).astype(BF16), jnp.asarray(g).astype(BF16))


def _short_conv_rows(x_ref, n1, w_ref, b_ref):
    r0 = n1 * DFT_N2
    cur = x_ref[r0:r0 + DFT_N2, :]
    row = lax.broadcasted_iota(jnp.int32, cur.shape, 0)
    if r0 == 0:
        prev = jnp.where(row == 0, 0.0, pltpu.roll(cur, 1, axis=0))
    else:
        prev = x_ref[r0 - 1:r0 + DFT_N2 - 1, :]
    if r0 + DFT_N2 == x_ref.shape[0]:
        nxt = jnp.where(row == DFT_N2 - 1, 0.0, pltpu.roll(cur, DFT_N2 - 1, axis=0))
    else:
        nxt = x_ref[r0 + 1:r0 + DFT_N2 + 1, :]
    return b_ref[...] + prev * w_ref[0:1, :] + cur * w_ref[1:2, :] + nxt * w_ref[2:3, :]


def _hyena_body(v_ref, gate_ref, vw_ref, vb_ref, gw_ref, gb_ref, h_ref, bias_ref, ff_ref, g_ref,
                gt_ref, fi_ref, o_ref, u_ref, y_ref, a_ref):
    order = pl.program_id(2)
    half = DFT_N1 // 2

    @pl.when(order == 0)
    def _():
        for n1 in range(half):
            u_ref[n1 * DFT_PITCH:n1 * DFT_PITCH + DFT_N2, :] = _short_conv_rows(
                v_ref, n1, vw_ref, vb_ref)

    _contract_planes(u_ref, a_ref, ff_ref)

    def freq_step(k1, carry):
        xh = _dot(g_ref[k1], _load_planes(a_ref, k1).astype(BF16))
        xre, xim = xh[:DFT_N2], xh[DFT_N2:]
        hre, him = h_ref[k1, :DFT_N2, :], h_ref[k1, DFT_N2:, :]
        yh = jnp.concatenate([xre * hre - xim * him, xre * him + xim * hre], axis=0)
        _store_planes(a_ref, k1, _dot(gt_ref[k1], yh.astype(BF16)))
        return carry

    lax.fori_loop(0, DFT_K1, freq_step, 0, unroll=K1_UNROLL)
    _contract_planes(a_ref, y_ref, fi_ref)

    bias = bias_ref[...]
    for n1 in range(half):
        rows = slice(n1 * DFT_PITCH, n1 * DFT_PITCH + DFT_N2)
        gate = _short_conv_rows(gate_ref, n1, gw_ref, gb_ref)
        z = gate * (y_ref[rows, :] + u_ref[rows, :] * bias)
        u_ref[rows, :] = z
        o_ref[n1 * DFT_N2:(n1 + 1) * DFT_N2, :] = z


def _hyena_mixer(proj, first_col, conv_w, conv_b, spec, bias):
    assert HYENA_SHORT_CONV == 3
    b, seq_len, _ = proj.shape
    width = bias.shape[1]
    cb = width // LANES
    c0 = first_col // LANES
    half = DFT_N1 // 2
    ff, g, gt, fi = _dft_constants()
    value_col = lambda c, bi, o: HYENA_ORDER * cb + c
    gate_col = lambda c, bi, o: o * cb + c
    conv_b = conv_b.reshape(1, -1)
    return pl.pallas_call(
        _hyena_body,
        out_shape=jax.ShapeDtypeStruct((b, seq_len, width), F32),
        grid=(cb, b, HYENA_ORDER),
        in_specs=[
            pl.BlockSpec((None, seq_len, LANES), lambda c, bi, o: (bi, 0, c0 + value_col(c, bi, o))),
            pl.BlockSpec((None, seq_len, LANES), lambda c, bi, o: (bi, 0, c0 + gate_col(c, bi, o))),
            pl.BlockSpec((HYENA_SHORT_CONV, LANES), lambda c, bi, o: (0, value_col(c, bi, o))),
            pl.BlockSpec((1, LANES), lambda c, bi, o: (0, value_col(c, bi, o))),
            pl.BlockSpec((HYENA_SHORT_CONV, LANES), lambda c, bi, o: (0, gate_col(c, bi, o))),
            pl.BlockSpec((1, LANES), lambda c, bi, o: (0, gate_col(c, bi, o))),
            pl.BlockSpec((None, DFT_K1, 2 * DFT_N2, LANES), lambda c, bi, o: (o, 0, 0, c)),
            pl.BlockSpec((None, 1, LANES), lambda c, bi, o: (o, 0, c)),
            _resident((2 * DFT_QP, half)),
            _resident((DFT_K1, 2 * DFT_N2, 2 * DFT_N2)),
            _resident((DFT_K1, 2 * DFT_N2, 2 * DFT_N2)),
            _resident((half, 2 * DFT_QP)),
        ],
        out_specs=pl.BlockSpec((None, seq_len, LANES), lambda c, bi, o: (bi, 0, c)),
        scratch_shapes=[pltpu.VMEM((half * DFT_PITCH, LANES), F32),
                        pltpu.VMEM((half * DFT_PITCH, LANES), F32),
                        pltpu.VMEM((2 * DFT_QP * DFT_PITCH, LANES), F32)],
        compiler_params=_cparams(("parallel", "parallel", "arbitrary")),
        name="hyena_mixer",
    )(proj, proj, conv_w, conv_b, conv_w, conv_b, spec, bias.reshape(HYENA_ORDER, 1, width),
      jnp.asarray(ff).astype(BF16), jnp.asarray(g).astype(BF16), jnp.asarray(gt).astype(BF16),
      jnp.asarray(fi).astype(BF16))


def _hdot(a, b):
    return jnp.dot(a, b, preferred_element_type=F32, precision=lax.Precision.HIGHEST)


def _filter_taps_body(bwd_cols, z_ref, t_ref, w1_ref, b1_ref, w2_ref, b2_ref, w3_ref, b3_ref,
                      sf_ref, wo_ref, decay_ref, taps_ref, ss_ref):
    i = pl.program_id(0)
    h = jnp.sin(sf_ref[0:1, :] * (_hdot(z_ref[...], w1_ref[...]) + b1_ref[...]))
    h = jnp.sin(sf_ref[1:2, :] * (_hdot(h, w2_ref[...]) + b2_ref[...]))
    h = jnp.sin(sf_ref[2:3, :] * (_hdot(h, w3_ref[...]) + b3_ref[...]))
    h = _dot(h.astype(BF16), wo_ref[...])
    h = h * jnp.exp(-t_ref[...] * jnp.abs(decay_ref[...]))
    row = i * h.shape[0] + lax.broadcasted_iota(jnp.int32, h.shape, 0)
    h = jnp.where((row == 0) & (bwd_cols[...] > 0.0), 0.0, h)
    taps_ref[...] = h

    @pl.when(i == 0)
    def _():
        ss_ref[...] = jnp.zeros_like(ss_ref)

    ss_ref[...] += jnp.sum(h * h, axis=0, keepdims=True)


def _filter_taps(z, t, w1, b1, w2, b2, w3, b3, sin_freq, w_out, decay, *, tl=512):
    seq_len = z.shape[0]
    hid = LANES
    n_out = w_out.shape[1]
    width = n_out // (2 * HYENA_ORDER)

    def pad2(a, rows, cols):
        return jnp.pad(a, ((0, rows - a.shape[0]), (0, cols - a.shape[1])))

    is_bwd = np.tile(np.repeat(np.array([0.0, 1.0], np.float32), width), HYENA_ORDER)[None, :]
    full = lambda a: pl.BlockSpec(a.shape, lambda i: (0, 0))
    args = (
        pad2(z, seq_len, hid), t.reshape(seq_len, 1),
        pad2(w1, hid, hid), pad2(b1[None, :], 1, hid),
        pad2(w2, hid, hid), pad2(b2[None, :], 1, hid),
        pad2(w3, hid, hid), pad2(b3[None, :], 1, hid),
        pad2(sin_freq, 3, hid), pad2(w_out, hid, n_out).astype(BF16), decay.reshape(1, n_out),
    )
    bwd_cols = jnp.asarray(is_bwd)
    return pl.pallas_call(
        _filter_taps_body,
        out_shape=(jax.ShapeDtypeStruct((seq_len, n_out), F32),
                   jax.ShapeDtypeStruct((1, n_out), F32)),
        grid=(seq_len // tl,),
        in_specs=[full(bwd_cols),
                  pl.BlockSpec((tl, hid), lambda i: (i, 0)),
                  pl.BlockSpec((tl, 1), lambda i: (i, 0))] + [full(a) for a in args[2:]],
        out_specs=(pl.BlockSpec((tl, n_out), lambda i: (i, 0)),
                   pl.BlockSpec((1, n_out), lambda i: (0, 0))),
        compiler_params=_cparams(("arbitrary",)),
        name="filter_taps",
    )(bwd_cols, *args)


def _rope_slab(x, cos, sin):
    half = MLA_ROPE // 2
    lane = lax.broadcasted_iota(jnp.int32, x.shape, 1)
    swapped = jnp.where(lane < MLA_NOPE + half, -pltpu.roll(x, LANES - half, axis=1),
                        pltpu.roll(x, half, axis=1))
    return x * cos + swapped * sin


def _mla_qkv_body(c_ref, qg_ref, kvg_ref, wq_ref, wk_ref, wv_ref, cos_ref, sin_ref,
                  q_ref, k_ref, v_ref):
    c = c_ref[...]
    cos = cos_ref[...]
    sin = sin_ref[...]
    cq = _rms(c[:, :MLA_Q_LORA], qg_ref[...]).astype(BF16)
    ckv = _rms(c[:, MLA_Q_LORA:MLA_Q_LORA + MLA_KV_LORA], kvg_ref[...]).astype(BF16)
    kr0 = MLA_Q_LORA + MLA_KV_LORA
    k_rope = _rope_slab(c[:, kr0:kr0 + LANES], cos, sin)
    ones_lane = (lax.broadcasted_iota(jnp.int32, (1, LANES), 1) == MLA_V).astype(F32)
    for h0 in range(0, MLA_HEADS, 2):
        cols = slice(h0 * LANES, (h0 + 2) * LANES)
        q2 = _dot(cq, wq_ref[:, cols])
        k2 = _dot(ckv, wk_ref[:, cols])
        v2 = _dot(ckv, wv_ref[:, cols])
        for j in range(2):
            lanes = slice(j * LANES, (j + 1) * LANES)
            q = _rope_slab(q2[:, lanes], cos, sin)
            q_ref[h0 + j] = (q * SOFTMAX_EXP2_SCALE).astype(BF16)
            k_ref[h0 + j] = (k2[:, lanes] + k_rope).astype(BF16)
            v_ref[h0 + j] = (v2[:, lanes] + ones_lane).astype(BF16)


def _mla_qkv(c, q_norm_g, kv_norm_g, wq, wk, wv, cos_t, sin_t, batch, *, tl=512):
    m, cw = c.shape
    seq_len = m // batch
    nl = seq_len // tl
    full = lambda a: pl.BlockSpec(a.shape, lambda bi, i: (0, 0))
    qg = q_norm_g.reshape(1, -1)
    kvg = kv_norm_g.reshape(1, -1)
    head_out = jax.ShapeDtypeStruct((batch, MLA_HEADS, seq_len, LANES), BF16)
    head_spec = pl.BlockSpec((None, MLA_HEADS, tl, LANES), lambda bi, i: (bi, 0, i, 0))
    return pl.pallas_call(
        _mla_qkv_body,
        out_shape=(head_out, head_out, head_out),
        grid=(batch, nl),
        in_specs=[pl.BlockSpec((tl, cw), lambda bi, i: (bi * nl + i, 0)),
                  full(qg), full(kvg), full(wq), full(wk), full(wv),
                  pl.BlockSpec((tl, LANES), lambda bi, i: (i, 0)),
                  pl.BlockSpec((tl, LANES), lambda bi, i: (i, 0))],
        out_specs=(head_spec, head_spec, head_spec),
        compiler_params=_cparams(("parallel", "parallel")),
        name="mla_qkv",
    )(c, qg, kvg, wq, wk, wv, cos_t, sin_t)


def _attention_body(q_ref, qn_ref, k_ref, v_ref, o_ref, *s_refs):
    pair = LANES // MLA_V
    n_heads = q_ref.shape[0]

    def scores(q, h):
        return lax.dot_general(q, k_ref[h], (((1,), (1,)), ((), ())), preferred_element_type=F32)

    @pl.when(pl.program_id(2) == 0)
    def _():
        for h, s_ref in enumerate(s_refs):
            s_ref[...] = scores(q_ref[h], h)

    ahead = {h: s_ref[...] for h, s_ref in enumerate(s_refs)}
    for g in range(n_heads // pair):
        outs = []
        for h in range(g * pair, (g + 1) * pair):
            lead = h + SCORE_LEAD
            if lead < n_heads:
                ahead[lead] = scores(q_ref[lead], lead)
            else:
                s_refs[lead - n_heads][...] = scores(qn_ref[lead - n_heads], lead - n_heads)
            s = ahead.pop(h)
            p = jnp.exp2(s - jnp.max(s, axis=-1, keepdims=True))
            r = _dot(p.astype(BF16), v_ref[h])
            outs.append(r / r[:, MLA_V:MLA_V + 1])
        lane = lax.broadcasted_iota(jnp.int32, outs[0].shape, 1)
        both = jnp.where(lane < MLA_V, outs[0], pltpu.roll(outs[1], MLA_V, axis=1))
        o_ref[:, g * LANES:(g + 1) * LANES] = both.astype(o_ref.dtype)


def _attention(q, k, v, *, tq=256, group=8):
    batch, heads, seq_len, _ = q.shape
    width = group * MLA_V
    n_tiles = seq_len // tq
    return pl.pallas_call(
        _attention_body,
        out_shape=jax.ShapeDtypeStruct((batch, seq_len, heads * MLA_V), BF16),
        grid=(batch, heads // group, n_tiles),
        in_specs=[
            pl.BlockSpec((None, group, tq, LANES), lambda bi, p, i: (bi, p, i, 0)),
            pl.BlockSpec((None, SCORE_LEAD, tq, LANES),
                         lambda bi, p, i: (bi, p * (group // SCORE_LEAD),
                                           jnp.minimum(i + 1, n_tiles - 1), 0)),
            pl.BlockSpec((None, group, seq_len, LANES), lambda bi, p, i: (bi, p, 0, 0)),
            pl.BlockSpec((None, group, seq_len, LANES), lambda bi, p, i: (bi, p, 0, 0)),
        ],
        out_specs=pl.BlockSpec((None, tq, width), lambda bi, p, i: (bi, i, p)),
        scratch_shapes=[pltpu.VMEM((tq, seq_len), F32)] * SCORE_LEAD,
        compiler_params=_cparams(("parallel", "parallel", "arbitrary")),
        name="attention",
    )(q, q, k, v)


def _mla_weights(w_dq, w_uq, w_dkv, w_ukv):
    d = w_dq.shape[0]
    qh = w_uq.reshape(MLA_Q_LORA, MLA_HEADS, MLA_NOPE + MLA_ROPE)
    zq = jnp.zeros((MLA_Q_LORA, MLA_HEADS, LANES - MLA_NOPE - MLA_ROPE), F32)
    wq = jnp.concatenate([qh, zq], axis=-1).reshape(MLA_Q_LORA, MLA_HEADS * LANES)
    kvh = w_ukv.reshape(MLA_KV_LORA, MLA_HEADS, MLA_NOPE + MLA_V)
    wk = jnp.concatenate([kvh[..., :MLA_NOPE],
                          jnp.zeros((MLA_KV_LORA, MLA_HEADS, LANES - MLA_NOPE), F32)],
                         axis=-1).reshape(MLA_KV_LORA, MLA_HEADS * LANES)
    wv = jnp.concatenate([kvh[..., MLA_NOPE:],
                          jnp.zeros((MLA_KV_LORA, MLA_HEADS, LANES - MLA_V), F32)],
                         axis=-1).reshape(MLA_KV_LORA, MLA_HEADS * LANES)
    w_kr = jnp.concatenate([jnp.zeros((d, MLA_NOPE), F32), w_dkv[:, MLA_KV_LORA:],
                            jnp.zeros((d, LANES - MLA_NOPE - MLA_ROPE), F32)], axis=-1)
    w_down = jnp.concatenate([w_dq, w_dkv[:, :MLA_KV_LORA], w_kr], axis=-1)
    return w_down.astype(BF16), wq.astype(BF16), wk.astype(BF16), wv.astype(BF16)


def _rope_tables(seq_len):
    inv_freq = ROPE_THETA ** (-jnp.arange(0, MLA_ROPE, 2, dtype=F32) / MLA_ROPE)
    ang = jnp.arange(seq_len, dtype=F32)[:, None] * inv_freq[None, :]
    cos = jnp.cos(ang)
    sin = jnp.sin(ang)
    tail = LANES - MLA_NOPE - MLA_ROPE
    cos_t = jnp.concatenate([jnp.ones((seq_len, MLA_NOPE), F32), cos, cos,
                             jnp.ones((seq_len, tail), F32)], axis=-1)
    sin_t = jnp.concatenate([jnp.zeros((seq_len, MLA_NOPE), F32), sin, sin,
                             jnp.zeros((seq_len, tail), F32)], axis=-1)
    return cos_t, sin_t


def _position_features(seq_len, pos_dim):
    t = jnp.linspace(0.0, 1.0, seq_len, dtype=F32)
    bands = (pos_dim - 1) // 2
    w = 2.0 * math.pi * jnp.arange(seq_len, dtype=F32) / seq_len
    f = jnp.linspace(1e-4, bands - 1, bands, dtype=F32)
    phase = w[:, None] * f[None, :]
    z = jnp.concatenate([t[:, None], jnp.cos(phase), -jnp.sin(phase)], axis=-1)
    return t, z


def kernel(x, norm_g, ffn_w_gate, ffn_w_up, ffn_w_down, mix_w_in, pool_w, pool_scale, hyena_conv_w, hyena_conv_b, hyena_ffn_w1, hyena_ffn_b1, hyena_ffn_w2, hyena_ffn_b2, hyena_ffn_w3, hyena_ffn_b3, hyena_sin_freq, hyena_ffn_w_out, hyena_decay, hyena_bias, mix_w_out, mla_w_dq, mla_q_norm_g, mla_w_uq, mla_w_dkv, mla_kv_norm_g, mla_w_ukv, mla_w_o, final_norm_g):
    batch, seq_len, d = x.shape
    depth = norm_g.shape[0]
    assert 2 * seq_len == DFT_N1 * DFT_N2
    hy_width = hyena_bias.shape[-1]
    t_pos, z_pos = _position_features(seq_len, hyena_ffn_w1.shape[1])
    cos_t, sin_t = _rope_tables(seq_len)

    x = x.reshape(batch * seq_len, d)
    ffn_w = (ffn_w_gate, ffn_w_up, ffn_w_down)
    for i in range(depth):
        j = i // 2
        if i % 2 == 0:
            x, proj = _ffn(x, norm_g[i, 0], ffn_w, (i, 0), next_g=norm_g[i, 1],
                           w_next=mix_w_in[j].astype(BF16))
            proj = proj.reshape(batch, seq_len, -1)
            y_pool = _pool_mixer(proj, pool_w[j].astype(BF16), pool_scale[j])
            taps, sumsq = _filter_taps(z_pos, t_pos, hyena_ffn_w1[j], hyena_ffn_b1[j],
                                       hyena_ffn_w2[j], hyena_ffn_b2[j], hyena_ffn_w3[j],
                                       hyena_ffn_b3[j], hyena_sin_freq[j], hyena_ffn_w_out[j],
                                       hyena_decay[j])
            spec = _filter_spectrum(taps, sumsq, hy_width)
            y_hyena = _hyena_mixer(proj, pool_scale.shape[-1], hyena_conv_w[j], hyena_conv_b[j],
                                   spec, hyena_bias[j])
            mix = (y_pool.reshape(batch * seq_len, -1), y_hyena.reshape(batch * seq_len, -1))
            w_mix = mix_w_out[j].astype(BF16)
        else:
            w_down, wq, wk, wv = _mla_weights(mla_w_dq[j], mla_w_uq[j], mla_w_dkv[j],
                                              mla_w_ukv[j])
            x, c = _ffn(x, norm_g[i, 0], ffn_w, (i, 0), next_g=norm_g[i, 1], w_next=w_down)
            q, k, v = _mla_qkv(c, mla_q_norm_g[j], mla_kv_norm_g[j], wq, wk, wv,
                               cos_t, sin_t, batch)
            mix = (_attention(q, k, v).reshape(batch * seq_len, -1),)
            w_mix = mla_w_o[j].astype(BF16)
        x = _ffn(x, norm_g[i, 2], ffn_w, (i, 1), mix=mix, w_mix=w_mix,
                 final_g=final_norm_g if i == depth - 1 else None)
    return x.reshape(batch, seq_len, d)
```

```python
import functools
import math

import numpy as np
import jax
import jax.numpy as jnp
from jax import lax
from jax.experimental import pallas as pl
from jax.experimental.pallas import tpu as pltpu

F32 = jnp.float32
BF16 = jnp.bfloat16

RMS_EPS = 1e-6
MACARON_WEIGHT = 0.5
POOL_WINDOWS = (2, 4, 8, 16)
HYENA_ORDER = 2
HYENA_SHORT_CONV = 3
MLA_HEADS = 16
MLA_Q_LORA = 256
MLA_KV_LORA = 128
MLA_NOPE = 64
MLA_ROPE = 32
MLA_V = 64
ROPE_THETA = 10000.0
SOFTMAX_EXP2_SCALE = (MLA_NOPE + MLA_ROPE) ** -0.5 * math.log2(math.e)

LANES = 128
SUBLANES = 8
VMEM_LIMIT = 56 * 2**20

DFT_N1 = 64
DFT_N2 = 128
DFT_K1 = DFT_N1 // 2 + 1
DFT_QP = 40
DFT_PITCH = DFT_N2 + SUBLANES
PLANE_ROWS_PER_DOT = 8
PLANE_UNROLL = 4
K1_UNROLL = 11
HALO = 16
FFN_LOAD_STEPS = 8
SCORE_LEAD = 2


def _cparams(semantics):
    return pltpu.CompilerParams(dimension_semantics=semantics, vmem_limit_bytes=VMEM_LIMIT)


def _rms(x, g):
    return x * lax.rsqrt(jnp.mean(x * x, axis=-1, keepdims=True) + RMS_EPS) * g


def _dot(a, b):
    return jnp.dot(a, b, preferred_element_type=F32)


def _ffn_body(n_mix, has_next, has_final, *refs):
    refs = list(refs)
    x_ref, g_ref, wg_ref, wu_ref, wd_ref = refs[:5]
    del refs[:5]
    mix_refs = [refs.pop(0) for _ in range(n_mix)]
    w_mix_ref = refs.pop(0) if n_mix else None
    next_g_ref, next_w_ref = (refs.pop(0), refs.pop(0)) if has_next else (None, None)
    final_g_ref = refs.pop(0) if has_final else None
    o_ref = refs.pop(0)
    next_o_ref = refs.pop(0) if has_next else None
    wg_bf, wu_bf, wd_bf = refs
    step = pl.program_id(0)

    @pl.when(step < FFN_LOAD_STEPS)
    def _():
        for src_ref, dst_ref in ((wg_ref, wg_bf), (wu_ref, wu_bf), (wd_ref, wd_bf)):
            rows = src_ref.shape[0]
            dst_ref[pl.ds(pl.multiple_of(step * rows, rows), rows), :] = src_ref[...].astype(BF16)

    @pl.when(step >= FFN_LOAD_STEPS)
    def _():
        x = x_ref[...]
        row = 0
        for a_ref in mix_refs:
            k = a_ref.shape[1]
            x = x + _dot(a_ref[...].astype(BF16), w_mix_ref[row:row + k, :])
            row += k
        h = _rms(x, g_ref[...]).astype(BF16)
        gate = _dot(h, wg_bf[...])
        up = _dot(h, wu_bf[...])
        act = gate / (1.0 + jnp.exp(-gate)) * up
        y = x + MACARON_WEIGHT * _dot(act.astype(BF16), wd_bf[...])
        if has_final:
            y = _rms(y, final_g_ref[...])
        o_ref[...] = y
        if has_next:
            next_o_ref[...] = _dot(_rms(y, next_g_ref[...]).astype(BF16), next_w_ref[...])


def _resident(shape):
    return pl.BlockSpec(shape, lambda *_: (0,) * len(shape), pipeline_mode=pl.Buffered(1))


def _streamed_rows(stack, index):
    lead = len(index)
    rows, cols = stack.shape[lead:]
    return pl.BlockSpec(
        (None,) * lead + (rows // FFN_LOAD_STEPS, cols),
        lambda i: tuple(index) + (jnp.minimum(i, FFN_LOAD_STEPS - 1), 0))


def _ffn(x, g, w_stacks, index, *, mix=(), w_mix=None, next_g=None, w_next=None, final_g=None,
         tm=512):
    m, d = x.shape
    tile = lambda i: jnp.maximum(i - FFN_LOAD_STEPS, 0)
    row_block = lambda a: pl.BlockSpec((tm, a.shape[1]), lambda i: (tile(i), 0))
    args = [x, g.reshape(1, d), *w_stacks]
    in_specs = [row_block(x), _resident((1, d))] + [_streamed_rows(w, index) for w in w_stacks]
    for a in mix:
        args.append(a)
        in_specs.append(row_block(a))
    if mix:
        args.append(w_mix)
        in_specs.append(_resident(w_mix.shape))
    if w_next is not None:
        args += [next_g.reshape(1, d), w_next]
        in_specs += [_resident((1, d)), _resident(w_next.shape)]
    if final_g is not None:
        args.append(final_g.reshape(1, d))
        in_specs.append(_resident((1, d)))
    out_shape = [jax.ShapeDtypeStruct((m, d), F32)]
    out_specs = [pl.BlockSpec((tm, d), lambda i: (tile(i), 0))]
    if w_next is not None:
        out_shape.append(jax.ShapeDtypeStruct((m, w_next.shape[1]), F32))
        out_specs.append(pl.BlockSpec((tm, w_next.shape[1]), lambda i: (tile(i), 0)))
    lead = len(index)
    outs = pl.pallas_call(
        functools.partial(_ffn_body, len(mix), w_next is not None, final_g is not None),
        out_shape=out_shape,
        grid=(FFN_LOAD_STEPS + m // tm,),
        in_specs=in_specs,
        out_specs=out_specs,
        scratch_shapes=[pltpu.VMEM(w.shape[lead:], BF16) for w in w_stacks],
        compiler_params=_cparams(("arbitrary",)),
        name="ffn",
    )(*args)
    return outs if w_next is not None else outs[0]


def _pool_body(seq_len, cur_ref, prev_ref, next_ref, pw_ref, ps_ref, yp_ref, ext_ref):
    i = pl.program_id(1)
    tl = cur_ref.shape[0]
    pool_width = yp_ref.shape[1]
    group = pool_width // len(POOL_WINDOWS)
    ext_ref[0:HALO, :] = jnp.where(i > 0, prev_ref[...], 0.0)
    ext_ref[HALO:HALO + tl, :] = cur_ref[...]
    ext_ref[HALO + tl:, :] = jnp.where(i < pl.num_programs(1) - 1, next_ref[...], 0.0)

    t = i * tl + lax.broadcasted_iota(jnp.int32, (tl, group), 0)
    for g, w in enumerate(POOL_WINDOWS):
        c0 = g * group
        lo = jnp.clip(t - w // 2, 0, seq_len)
        hi = jnp.clip(t - w // 2 + w, 0, seq_len)
        cnt = (hi - lo).astype(F32)
        s = ext_ref[HALO - w // 2:HALO - w // 2 + tl, c0:c0 + group]
        for d in range(1 - w // 2, w - w // 2):
            s = s + ext_ref[HALO + d:HALO + d + tl, c0:c0 + group]
        p = s / cnt - cur_ref[:, c0:c0 + group]
        y = _dot(p.astype(BF16), pw_ref[g])
        yp_ref[:, c0:c0 + group] = y * ps_ref[:, c0:c0 + group]


def _pool_mixer(proj, pool_w, pool_scale, *, tl=512):
    b, seq_len, _ = proj.shape
    pool_width = pool_scale.shape[0]
    hb = tl // HALO
    last = seq_len // HALO - 1
    return pl.pallas_call(
        functools.partial(_pool_body, seq_len),
        out_shape=jax.ShapeDtypeStruct((b, seq_len, pool_width), F32),
        grid=(b, seq_len // tl),
        in_specs=[
            pl.BlockSpec((None, tl, pool_width), lambda bi, i: (bi, i, 0)),
            pl.BlockSpec((None, HALO, pool_width),
                         lambda bi, i: (bi, jnp.maximum(i * hb - 1, 0), 0)),
            pl.BlockSpec((None, HALO, pool_width),
                         lambda bi, i: (bi, jnp.minimum((i + 1) * hb, last), 0)),
            pl.BlockSpec(pool_w.shape, lambda bi, i: (0, 0, 0)),
            pl.BlockSpec((1, pool_width), lambda bi, i: (0, 0)),
        ],
        out_specs=pl.BlockSpec((None, tl, pool_width), lambda bi, i: (bi, i, 0)),
        scratch_shapes=[pltpu.VMEM((tl + 2 * HALO, pool_width), F32)],
        compiler_params=_cparams(("parallel", "parallel")),
        name="pool_mixer",
    )(proj, proj, proj, pool_w, pool_scale.reshape(1, pool_width))


@functools.lru_cache(maxsize=None)
def _dft_constants():
    n = DFT_N1 * DFT_N2
    half = DFT_N1 // 2
    k1 = np.arange(DFT_K1)
    n1 = np.arange(half)
    ang = 2.0 * np.pi * np.outer(k1, n1) / DFT_N1
    f_fwd = np.zeros((2 * DFT_QP, half))
    f_fwd[:DFT_K1] = np.cos(ang)
    f_fwd[DFT_QP:DFT_QP + DFT_K1] = -np.sin(ang)
    n2 = np.arange(DFT_N2)
    k2 = np.arange(DFT_N2)
    g = np.zeros((DFT_K1, 2 * DFT_N2, 2 * DFT_N2))
    for a in range(DFT_K1):
        ph = 2.0 * np.pi * np.outer(a + DFT_N1 * k2, n2) / n
        gre, gim = np.cos(ph), -np.sin(ph)
        g[a] = np.block([[gre, -gim], [gim, gre]])
    weight = np.full(DFT_K1, 2.0)
    weight[0] = 1.0
    weight[-1] = 1.0
    f_inv = np.zeros((half, 2 * DFT_QP))
    f_inv[:, :DFT_K1] = weight * np.cos(ang.T) / n
    f_inv[:, DFT_QP:DFT_QP + DFT_K1] = -weight * np.sin(ang.T) / n
    return (f_fwd.astype(np.float32), g.astype(np.float32),
            np.transpose(g, (0, 2, 1)).astype(np.float32), f_inv.astype(np.float32))


def _pad_rows(dst_ref, src):
    for n1 in range(DFT_N1 // 2):
        dst_ref[n1 * DFT_PITCH:n1 * DFT_PITCH + DFT_N2, :] = src[n1 * DFT_N2:(n1 + 1) * DFT_N2, :]


def _contract_planes(src_ref, dst_ref, f_ref):
    n_dst, n_src = f_ref.shape

    def step(j, carry):
        r0 = j * PLANE_ROWS_PER_DOT
        x = jnp.concatenate(
            [src_ref[pl.ds(r0 + r, n_src, stride=DFT_PITCH), :].astype(BF16)
             for r in range(PLANE_ROWS_PER_DOT)], axis=1)
        y = _dot(f_ref[...], x)
        for r in range(PLANE_ROWS_PER_DOT):
            dst_ref[pl.ds(r0 + r, n_dst, stride=DFT_PITCH), :] = y[:, r * LANES:(r + 1) * LANES]
        return carry

    lax.fori_loop(0, DFT_N2 // PLANE_ROWS_PER_DOT, step, 0, unroll=PLANE_UNROLL)


def _load_planes(a_ref, k1):
    re = a_ref[pl.ds(pl.multiple_of(k1 * DFT_PITCH, SUBLANES), DFT_N2), :]
    im = a_ref[pl.ds(pl.multiple_of((DFT_QP + k1) * DFT_PITCH, SUBLANES), DFT_N2), :]
    return jnp.concatenate([re, im], axis=0)


def _store_planes(a_ref, k1, z):
    a_ref[pl.ds(pl.multiple_of(k1 * DFT_PITCH, SUBLANES), DFT_N2), :] = z[:DFT_N2]
    a_ref[pl.ds(pl.multiple_of((DFT_QP + k1) * DFT_PITCH, SUBLANES), DFT_N2), :] = z[DFT_N2:]


def _spectrum_body(fw_ref, bw_ref, ssf_ref, ssb_ref, ff_ref, gt_ref, h_ref, u_ref, a_ref):
    norm = lax.rsqrt(ssf_ref[...] + ssb_ref[...]).T
    for part, src_ref in enumerate((fw_ref, bw_ref)):
        _pad_rows(u_ref, src_ref)
        _contract_planes(u_ref, a_ref, ff_ref)
        sign = 1.0 if part == 0 else -1.0

        def step(k1, carry):
            xt = _dot(_load_planes(a_ref, k1).T.astype(BF16), gt_ref[k1])
            re = xt[:, :DFT_N2] * norm
            im = xt[:, DFT_N2:] * (sign * norm)
            if part == 0:
                h_ref[k1, :, :DFT_N2] = re
                h_ref[k1, :, DFT_N2:] = im
            else:
                h_ref[k1, :, :DFT_N2] += re
                h_ref[k1, :, DFT_N2:] += im
            return carry

        lax.fori_loop(0, DFT_K1, step, 0, unroll=K1_UNROLL)


def _filter_spectrum(taps, sumsq, width):
    seq_len = taps.shape[0]
    ff, _, gt, _ = _dft_constants()
    cb = width // LANES
    half = DFT_N1 // 2
    return pl.pallas_call(
        _spectrum_body,
        out_shape=jax.ShapeDtypeStruct((HYENA_ORDER, DFT_K1, width, 2 * DFT_N2), F32),
        grid=(HYENA_ORDER, cb),
        in_specs=[
            pl.BlockSpec((seq_len, LANES), lambda o, c: (0, (2 * o) * cb + c)),
            pl.BlockSpec((seq_len, LANES), lambda o, c: (0, (2 * o + 1) * cb + c)),
            pl.BlockSpec((1, LANES), lambda o, c: (0, (2 * o) * cb + c)),
            pl.BlockSpec((1, LANES), lambda o, c: (0, (2 * o + 1) * cb + c)),
            pl.BlockSpec((2 * DFT_QP, half), lambda o, c: (0, 0)),
            pl.BlockSpec((DFT_K1, 2 * DFT_N2, 2 * DFT_N2), lambda o, c: (0, 0, 0)),
        ],
        out_specs=pl.BlockSpec((None, DFT_K1, LANES, 2 * DFT_N2), lambda o, c: (o, 0, c, 0)),
        scratch_shapes=[pltpu.VMEM((half * DFT_PITCH, LANES), F32),
                        pltpu.VMEM((2 * DFT_QP * DFT_PITCH, LANES), F32)],
        compiler_params=_cparams(("parallel", "parallel")),
        name="filter_spectrum",
    )(taps, taps, sumsq, sumsq, jnp.asarray(ff).astype(BF16), jnp.asarray(gt).astype(BF16))


def _short_conv_rows(x_ref, n1, w_ref, b_ref):
    r0 = n1 * DFT_N2
    cur = x_ref[r0:r0 + DFT_N2, :]
    row = lax.broadcasted_iota(jnp.int32, cur.shape, 0)
    if r0 == 0:
        prev = jnp.where(row == 0, 0.0, pltpu.roll(cur, 1, axis=0))
    else:
        prev = x_ref[r0 - 1:r0 + DFT_N2 - 1, :]
    if r0 + DFT_N2 == x_ref.shape[0]:
        nxt = jnp.where(row == DFT_N2 - 1, 0.0, pltpu.roll(cur, DFT_N2 - 1, axis=0))
    else:
        nxt = x_ref[r0 + 1:r0 + DFT_N2 + 1, :]
    return b_ref[...] + prev * w_ref[0:1, :] + cur * w_ref[1:2, :] + nxt * w_ref[2:3, :]


def _hyena_body(v_ref, gate_ref, vw_ref, vb_ref, gw_ref, gb_ref, h_ref, bias_ref, ff_ref, g_ref,
                gt_ref, fi_ref, o_ref, u_ref, y_ref, a_ref):
    order = pl.program_id(2)
    half = DFT_N1 // 2

    @pl.when(order == 0)
    def _():
        for n1 in range(half):
            u_ref[n1 * DFT_PITCH:n1 * DFT_PITCH + DFT_N2, :] = _short_conv_rows(
                v_ref, n1, vw_ref, vb_ref)

    _contract_planes(u_ref, a_ref, ff_ref)

    def freq_step(k1, carry):
        xt = _dot(_load_planes(a_ref, k1).T.astype(BF16), gt_ref[k1])
        xre, xim = xt[:, :DFT_N2], xt[:, DFT_N2:]
        hre, him = h_ref[k1, :, :DFT_N2], h_ref[k1, :, DFT_N2:]
        yt = jnp.concatenate([xre * hre - xim * him, xre * him + xim * hre], axis=1)
        _store_planes(a_ref, k1, _dot(yt.astype(BF16), g_ref[k1]).T)
        return carry

    lax.fori_loop(0, DFT_K1, freq_step, 0, unroll=K1_UNROLL)
    _contract_planes(a_ref, y_ref, fi_ref)

    bias = bias_ref[...]
    for n1 in range(half):
        rows = slice(n1 * DFT_PITCH, n1 * DFT_PITCH + DFT_N2)
        gate = _short_conv_rows(gate_ref, n1, gw_ref, gb_ref)
        z = gate * (y_ref[rows, :] + u_ref[rows, :] * bias)
        u_ref[rows, :] = z
        o_ref[n1 * DFT_N2:(n1 + 1) * DFT_N2, :] = z


def _hyena_mixer(proj, first_col, conv_w, conv_b, spec, bias):
    assert HYENA_SHORT_CONV == 3
    b, seq_len, _ = proj.shape
    width = bias.shape[1]
    cb = width // LANES
    c0 = first_col // LANES
    half = DFT_N1 // 2
    ff, g, gt, fi = _dft_constants()
    value_col = lambda c, bi, o: HYENA_ORDER * cb + c
    gate_col = lambda c, bi, o: o * cb + c
    conv_b = conv_b.reshape(1, -1)
    return pl.pallas_call(
        _hyena_body,
        out_shape=jax.ShapeDtypeStruct((b, seq_len, width), F32),
        grid=(cb, b, HYENA_ORDER),
        in_specs=[
            pl.BlockSpec((None, seq_len, LANES), lambda c, bi, o: (bi, 0, c0 + value_col(c, bi, o))),
            pl.BlockSpec((None, seq_len, LANES), lambda c, bi, o: (bi, 0, c0 + gate_col(c, bi, o))),
            pl.BlockSpec((HYENA_SHORT_CONV, LANES), lambda c, bi, o: (0, value_col(c, bi, o))),
            pl.BlockSpec((1, LANES), lambda c, bi, o: (0, value_col(c, bi, o))),
            pl.BlockSpec((HYENA_SHORT_CONV, LANES), lambda c, bi, o: (0, gate_col(c, bi, o))),
            pl.BlockSpec((1, LANES), lambda c, bi, o: (0, gate_col(c, bi, o))),
            pl.BlockSpec((None, DFT_K1, LANES, 2 * DFT_N2), lambda c, bi, o: (o, 0, c, 0)),
            pl.BlockSpec((None, 1, LANES), lambda c, bi, o: (o, 0, c)),
            _resident((2 * DFT_QP, half)),
            _resident((DFT_K1, 2 * DFT_N2, 2 * DFT_N2)),
            _resident((DFT_K1, 2 * DFT_N2, 2 * DFT_N2)),
            _resident((half, 2 * DFT_QP)),
        ],
        out_specs=pl.BlockSpec((None, seq_len, LANES), lambda c, bi, o: (bi, 0, c)),
        scratch_shapes=[pltpu.VMEM((half * DFT_PITCH, LANES), F32),
                        pltpu.VMEM((half * DFT_PITCH, LANES), F32),
                        pltpu.VMEM((2 * DFT_QP * DFT_PITCH, LANES), F32)],
        compiler_params=_cparams(("parallel", "parallel", "arbitrary")),
        name="hyena_mixer",
    )(proj, proj, conv_w, conv_b, conv_w, conv_b, spec, bias.reshape(HYENA_ORDER, 1, width),
      jnp.asarray(ff).astype(BF16), jnp.asarray(g).astype(BF16), jnp.asarray(gt).astype(BF16),
      jnp.asarray(fi).astype(BF16))


def _hdot(a, b):
    return jnp.dot(a, b, preferred_element_type=F32, precision=lax.Precision.HIGHEST)


def _filter_taps_body(bwd_cols, z_ref, t_ref, w1_ref, b1_ref, w2_ref, b2_ref, w3_ref, b3_ref,
                      sf_ref, wo_ref, decay_ref, taps_ref, ss_ref):
    i = pl.program_id(0)
    h = jnp.sin(sf_ref[0:1, :] * (_hdot(z_ref[...], w1_ref[...]) + b1_ref[...]))
    h = jnp.sin(sf_ref[1:2, :] * (_hdot(h, w2_ref[...]) + b2_ref[...]))
    h = jnp.sin(sf_ref[2:3, :] * (_hdot(h, w3_ref[...]) + b3_ref[...]))
    h = _dot(h.astype(BF16), wo_ref[...])
    h = h * jnp.exp(-t_ref[...] * jnp.abs(decay_ref[...]))
    row = i * h.shape[0] + lax.broadcasted_iota(jnp.int32, h.shape, 0)
    h = jnp.where((row == 0) & (bwd_cols[...] > 0.0), 0.0, h)
    taps_ref[...] = h

    @pl.when(i == 0)
    def _():
        ss_ref[...] = jnp.zeros_like(ss_ref)

    ss_ref[...] += jnp.sum(h * h, axis=0, keepdims=True)


def _filter_taps(z, t, w1, b1, w2, b2, w3, b3, sin_freq, w_out, decay, *, tl=512):
    seq_len = z.shape[0]
    hid = LANES
    n_out = w_out.shape[1]
    width = n_out // (2 * HYENA_ORDER)

    def pad2(a, rows, cols):
        return jnp.pad(a, ((0, rows - a.shape[0]), (0, cols - a.shape[1])))

    is_bwd = np.tile(np.repeat(np.array([0.0, 1.0], np.float32), width), HYENA_ORDER)[None, :]
    full = lambda a: pl.BlockSpec(a.shape, lambda i: (0, 0))
    args = (
        pad2(z, seq_len, hid), t.reshape(seq_len, 1),
        pad2(w1, hid, hid), pad2(b1[None, :], 1, hid),
        pad2(w2, hid, hid), pad2(b2[None, :], 1, hid),
        pad2(w3, hid, hid), pad2(b3[None, :], 1, hid),
        pad2(sin_freq, 3, hid), pad2(w_out, hid, n_out).astype(BF16), decay.reshape(1, n_out),
    )
    bwd_cols = jnp.asarray(is_bwd)
    return pl.pallas_call(
        _filter_taps_body,
        out_shape=(jax.ShapeDtypeStruct((seq_len, n_out), F32),
                   jax.ShapeDtypeStruct((1, n_out), F32)),
        grid=(seq_len // tl,),
        in_specs=[full(bwd_cols),
                  pl.BlockSpec((tl, hid), lambda i: (i, 0)),
                  pl.BlockSpec((tl, 1), lambda i: (i, 0))] + [full(a) for a in args[2:]],
        out_specs=(pl.BlockSpec((tl, n_out), lambda i: (i, 0)),
                   pl.BlockSpec((1, n_out), lambda i: (0, 0))),
        compiler_params=_cparams(("arbitrary",)),
        name="filter_taps",
    )(bwd_cols, *args)


def _rope_slab(x, cos, sin):
    half = MLA_ROPE // 2
    lane = lax.broadcasted_iota(jnp.int32, x.shape, 1)
    swapped = jnp.where(lane < MLA_NOPE + half, -pltpu.roll(x, LANES - half, axis=1),
                        pltpu.roll(x, half, axis=1))
    return x * cos + swapped * sin


def _mla_qkv_body(c_ref, qg_ref, kvg_ref, wq_ref, wk_ref, wv_ref, cos_ref, sin_ref,
                  q_ref, k_ref, v_ref):
    c = c_ref[...]
    cos = cos_ref[...]
    sin = sin_ref[...]
    cq = _rms(c[:, :MLA_Q_LORA], qg_ref[...]).astype(BF16)
    ckv = _rms(c[:, MLA_Q_LORA:MLA_Q_LORA + MLA_KV_LORA], kvg_ref[...]).astype(BF16)
    kr0 = MLA_Q_LORA + MLA_KV_LORA
    k_rope = _rope_slab(c[:, kr0:kr0 + LANES], cos, sin)
    ones_lane = (lax.broadcasted_iota(jnp.int32, (1, LANES), 1) == MLA_V).astype(F32)
    for h0 in range(0, MLA_HEADS, 2):
        cols = slice(h0 * LANES, (h0 + 2) * LANES)
        q2 = _dot(cq, wq_ref[:, cols])
        k2 = _dot(ckv, wk_ref[:, cols])
        v2 = _dot(ckv, wv_ref[:, cols])
        for j in range(2):
            lanes = slice(j * LANES, (j + 1) * LANES)
            q = _rope_slab(q2[:, lanes], cos, sin)
            q_ref[h0 + j] = (q * SOFTMAX_EXP2_SCALE).astype(BF16)
            k_ref[h0 + j] = (k2[:, lanes] + k_rope).astype(BF16)
            v_ref[h0 + j] = (v2[:, lanes] + ones_lane).astype(BF16)


def _mla_qkv(c, q_norm_g, kv_norm_g, wq, wk, wv, cos_t, sin_t, batch, *, tl=512):
    m, cw = c.shape
    seq_len = m // batch
    nl = seq_len // tl
    full = lambda a: pl.BlockSpec(a.shape, lambda bi, i: (0, 0))
    qg = q_norm_g.reshape(1, -1)
    kvg = kv_norm_g.reshape(1, -1)
    head_out = jax.ShapeDtypeStruct((batch, MLA_HEADS, seq_len, LANES), BF16)
    head_spec = pl.BlockSpec((None, MLA_HEADS, tl, LANES), lambda bi, i: (bi, 0, i, 0))
    return pl.pallas_call(
        _mla_qkv_body,
        out_shape=(head_out, head_out, head_out),
        grid=(batch, nl),
        in_specs=[pl.BlockSpec((tl, cw), lambda bi, i: (bi * nl + i, 0)),
                  full(qg), full(kvg), full(wq), full(wk), full(wv),
                  pl.BlockSpec((tl, LANES), lambda bi, i: (i, 0)),
                  pl.BlockSpec((tl, LANES), lambda bi, i: (i, 0))],
        out_specs=(head_spec, head_spec, head_spec),
        compiler_params=_cparams(("parallel", "parallel")),
        name="mla_qkv",
    )(c, qg, kvg, wq, wk, wv, cos_t, sin_t)


def _attention_body(q_ref, qn_ref, k_ref, v_ref, o_ref, *s_refs):
    pair = LANES // MLA_V
    n_heads = q_ref.shape[0]

    def scores(q, h):
        return lax.dot_general(q, k_ref[h], (((1,), (1,)), ((), ())), preferred_element_type=F32)

    @pl.when(pl.program_id(2) == 0)
    def _():
        for h, s_ref in enumerate(s_refs):
            s_ref[...] = scores(q_ref[h], h)

    ahead = {h: s_ref[...] for h, s_ref in enumerate(s_refs)}
    for g in range(n_heads // pair):
        outs = []
        for h in range(g * pair, (g + 1) * pair):
            lead = h + SCORE_LEAD
            if lead < n_heads:
                ahead[lead] = scores(q_ref[lead], lead)
            else:
                s_refs[lead - n_heads][...] = scores(qn_ref[lead - n_heads], lead - n_heads)
            s = ahead.pop(h)
            p = jnp.exp2(s - jnp.max(s, axis=-1, keepdims=True))
            r = _dot(p.astype(BF16), v_ref[h])
            outs.append(r / r[:, MLA_V:MLA_V + 1])
        lane = lax.broadcasted_iota(jnp.int32, outs[0].shape, 1)
        both = jnp.where(lane < MLA_V, outs[0], pltpu.roll(outs[1], MLA_V, axis=1))
        o_ref[:, g * LANES:(g + 1) * LANES] = both.astype(o_ref.dtype)


def _attention(q, k, v, *, tq=256, group=8):
    batch, heads, seq_len, _ = q.shape
    width = group * MLA_V
    n_tiles = seq_len // tq
    return pl.pallas_call(
        _attention_body,
        out_shape=jax.ShapeDtypeStruct((batch, seq_len, heads * MLA_V), BF16),
        grid=(batch, heads // group, n_tiles),
        in_specs=[
            pl.BlockSpec((None, group, tq, LANES), lambda bi, p, i: (bi, p, i, 0)),
            pl.BlockSpec((None, SCORE_LEAD, tq, LANES),
                         lambda bi, p, i: (bi, p * (group // SCORE_LEAD),
                                           jnp.minimum(i + 1, n_tiles - 1), 0)),
            pl.BlockSpec((None, group, seq_len, LANES), lambda bi, p, i: (bi, p, 0, 0)),
            pl.BlockSpec((None, group, seq_len, LANES), lambda bi, p, i: (bi, p, 0, 0)),
        ],
        out_specs=pl.BlockSpec((None, tq, width), lambda bi, p, i: (bi, i, p)),
        scratch_shapes=[pltpu.VMEM((tq, seq_len), F32)] * SCORE_LEAD,
        compiler_params=_cparams(("parallel", "parallel", "arbitrary")),
        name="attention",
    )(q, q, k, v)


def _mla_weights(w_dq, w_uq, w_dkv, w_ukv):
    d = w_dq.shape[0]
    qh = w_uq.reshape(MLA_Q_LORA, MLA_HEADS, MLA_NOPE + MLA_ROPE)
    zq = jnp.zeros((MLA_Q_LORA, MLA_HEADS, LANES - MLA_NOPE - MLA_ROPE), F32)
    wq = jnp.concatenate([qh, zq], axis=-1).reshape(MLA_Q_LORA, MLA_HEADS * LANES)
    kvh = w_ukv.reshape(MLA_KV_LORA, MLA_HEADS, MLA_NOPE + MLA_V)
    wk = jnp.concatenate([kvh[..., :MLA_NOPE],
                          jnp.zeros((MLA_KV_LORA, MLA_HEADS, LANES - MLA_NOPE), F32)],
                         axis=-1).reshape(MLA_KV_LORA, MLA_HEADS * LANES)
    wv = jnp.concatenate([kvh[..., MLA_NOPE:],
                          jnp.zeros((MLA_KV_LORA, MLA_HEADS, LANES - MLA_V), F32)],
                         axis=-1).reshape(MLA_KV_LORA, MLA_HEADS * LANES)
    w_kr = jnp.concatenate([jnp.zeros((d, MLA_NOPE), F32), w_dkv[:, MLA_KV_LORA:],
                            jnp.zeros((d, LANES - MLA_NOPE - MLA_ROPE), F32)], axis=-1)
    w_down = jnp.concatenate([w_dq, w_dkv[:, :MLA_KV_LORA], w_kr], axis=-1)
    return w_down.astype(BF16), wq.astype(BF16), wk.astype(BF16), wv.astype(BF16)


def _rope_tables(seq_len):
    inv_freq = ROPE_THETA ** (-jnp.arange(0, MLA_ROPE, 2, dtype=F32) / MLA_ROPE)
    ang = jnp.arange(seq_len, dtype=F32)[:, None] * inv_freq[None, :]
    cos = jnp.cos(ang)
    sin = jnp.sin(ang)
    tail = LANES - MLA_NOPE - MLA_ROPE
    cos_t = jnp.concatenate([jnp.ones((seq_len, MLA_NOPE), F32), cos, cos,
                             jnp.ones((seq_len, tail), F32)], axis=-1)
    sin_t = jnp.concatenate([jnp.zeros((seq_len, MLA_NOPE), F32), sin, sin,
                             jnp.zeros((seq_len, tail), F32)], axis=-1)
    return cos_t, sin_t


def _position_features(seq_len, pos_dim):
    t = jnp.linspace(0.0, 1.0, seq_len, dtype=F32)
    bands = (pos_dim - 1) // 2
    w = 2.0 * math.pi * jnp.arange(seq_len, dtype=F32) / seq_len
    f = jnp.linspace(1e-4, bands - 1, bands, dtype=F32)
    phase = w[:, None] * f[None, :]
    z = jnp.concatenate([t[:, None], jnp.cos(phase), -jnp.sin(phase)], axis=-1)
    return t, z


def kernel(x, norm_g, ffn_w_gate, ffn_w_up, ffn_w_down, mix_w_in, pool_w, pool_scale, hyena_conv_w, hyena_conv_b, hyena_ffn_w1, hyena_ffn_b1, hyena_ffn_w2, hyena_ffn_b2, hyena_ffn_w3, hyena_ffn_b3, hyena_sin_freq, hyena_ffn_w_out, hyena_decay, hyena_bias, mix_w_out, mla_w_dq, mla_q_norm_g, mla_w_uq, mla_w_dkv, mla_kv_norm_g, mla_w_ukv, mla_w_o, final_norm_g):
    batch, seq_len, d = x.shape
    depth = norm_g.shape[0]
    assert 2 * seq_len == DFT_N1 * DFT_N2
    hy_width = hyena_bias.shape[-1]
    t_pos, z_pos = _position_features(seq_len, hyena_ffn_w1.shape[1])
    cos_t, sin_t = _rope_tables(seq_len)

    x = x.reshape(batch * seq_len, d)
    ffn_w = (ffn_w_gate, ffn_w_up, ffn_w_down)
    for i in range(depth):
        j = i // 2
        if i % 2 == 0:
            x, proj = _ffn(x, norm_g[i, 0], ffn_w, (i, 0), next_g=norm_g[i, 1],
                           w_next=mix_w_in[j].astype(BF16))
            proj = proj.reshape(batch, seq_len, -1)
            y_pool = _pool_mixer(proj, pool_w[j].astype(BF16), pool_scale[j])
            taps, sumsq = _filter_taps(z_pos, t_pos, hyena_ffn_w1[j], hyena_ffn_b1[j],
                                       hyena_ffn_w2[j], hyena_ffn_b2[j], hyena_ffn_w3[j],
                                       hyena_ffn_b3[j], hyena_sin_freq[j], hyena_ffn_w_out[j],
                                       hyena_decay[j])
            spec = _filter_spectrum(taps, sumsq, hy_width)
            y_hyena = _hyena_mixer(proj, pool_scale.shape[-1], hyena_conv_w[j], hyena_conv_b[j],
                                   spec, hyena_bias[j])
            mix = (y_pool.reshape(batch * seq_len, -1), y_hyena.reshape(batch * seq_len, -1))
            w_mix = mix_w_out[j].astype(BF16)
        else:
            w_down, wq, wk, wv = _mla_weights(mla_w_dq[j], mla_w_uq[j], mla_w_dkv[j],
                                              mla_w_ukv[j])
            x, c = _ffn(x, norm_g[i, 0], ffn_w, (i, 0), next_g=norm_g[i, 1], w_next=w_down)
            q, k, v = _mla_qkv(c, mla_q_norm_g[j], mla_kv_norm_g[j], wq, wk, wv,
                               cos_t, sin_t, batch)
            mix = (_attention(q, k, v).reshape(batch * seq_len, -1),)
            w_mix = mla_w_o[j].astype(BF16)
        x = _ffn(x, norm_g[i, 2], ffn_w, (i, 1), mix=mix, w_mix=w_mix,
                 final_g=final_norm_g if i == depth - 1 else None)
    return x.reshape(batch, seq_len, d)
```

```python
import functools
import math

import numpy as np
import jax
import jax.numpy as jnp
from jax import lax
from jax.experimental import pallas as pl
from jax.experimental.pallas import tpu as pltpu

F32 = jnp.float32
BF16 = jnp.bfloat16

RMS_EPS = 1e-6
MACARON_WEIGHT = 0.5
POOL_WINDOWS = (2, 4, 8, 16)
HYENA_ORDER = 2
HYENA_SHORT_CONV = 3
MLA_HEADS = 16
MLA_Q_LORA = 256
MLA_KV_LORA = 128
MLA_NOPE = 64
MLA_ROPE = 32
MLA_V = 64
ROPE_THETA = 10000.0
SOFTMAX_EXP2_SCALE = (MLA_NOPE + MLA_ROPE) ** -0.5 * math.log2(math.e)

LANES = 128
SUBLANES = 8
VMEM_LIMIT = 56 * 2**20

DFT_N1 = 64
DFT_N2 = 128
DFT_K1 = DFT_N1 // 2 + 1
DFT_QP = 40
DFT_PITCH = DFT_N2 + SUBLANES
PLANE_ROWS_PER_DOT = 8
PLANE_UNROLL = 4
K1_UNROLL = 11
HALO = 16
FFN_LOAD_STEPS = 8
SCORE_LEAD = 2


def _cparams(semantics):
    return pltpu.CompilerParams(dimension_semantics=semantics, vmem_limit_bytes=VMEM_LIMIT)


def _rms(x, g):
    return x * lax.rsqrt(jnp.mean(x * x, axis=-1, keepdims=True) + RMS_EPS) * g


def _dot(a, b):
    return jnp.dot(a, b, preferred_element_type=F32)


def _ffn_body(n_mix, has_next, has_final, *refs):
    refs = list(refs)
    x_ref, g_ref, wg_ref, wu_ref, wd_ref = refs[:5]
    del refs[:5]
    mix_refs = [refs.pop(0) for _ in range(n_mix)]
    w_mix_ref = refs.pop(0) if n_mix else None
    next_g_ref, next_w_ref = (refs.pop(0), refs.pop(0)) if has_next else (None, None)
    final_g_ref = refs.pop(0) if has_final else None
    o_ref = refs.pop(0)
    next_o_ref = refs.pop(0) if has_next else None
    wg_bf, wu_bf, wd_bf = refs
    step = pl.program_id(0)

    @pl.when(step < FFN_LOAD_STEPS)
    def _():
        for src_ref, dst_ref in ((wg_ref, wg_bf), (wu_ref, wu_bf), (wd_ref, wd_bf)):
            rows = src_ref.shape[0]
            dst_ref[pl.ds(pl.multiple_of(step * rows, rows), rows), :] = src_ref[...].astype(BF16)

    @pl.when(step >= FFN_LOAD_STEPS)
    def _():
        x = x_ref[...]
        row = 0
        for a_ref in mix_refs:
            k = a_ref.shape[1]
            x = x + _dot(a_ref[...].astype(BF16), w_mix_ref[row:row + k, :])
            row += k
        h = _rms(x, g_ref[...]).astype(BF16)
        gate = _dot(h, wg_bf[...])
        up = _dot(h, wu_bf[...])
        act = gate / (1.0 + jnp.exp(-gate)) * up
        y = x + MACARON_WEIGHT * _dot(act.astype(BF16), wd_bf[...])
        if has_final:
            y = _rms(y, final_g_ref[...])
        o_ref[...] = y
        if has_next:
            next_o_ref[...] = _dot(_rms(y, next_g_ref[...]).astype(BF16), next_w_ref[...])


def _resident(shape):
    return pl.BlockSpec(shape, lambda *_: (0,) * len(shape), pipeline_mode=pl.Buffered(1))


def _streamed_rows(stack, index):
    lead = len(index)
    rows, cols = stack.shape[lead:]
    return pl.BlockSpec(
        (None,) * lead + (rows // FFN_LOAD_STEPS, cols),
        lambda i: tuple(index) + (jnp.minimum(i, FFN_LOAD_STEPS - 1), 0))


def _ffn(x, g, w_stacks, index, *, mix=(), w_mix=None, next_g=None, w_next=None, final_g=None,
         tm=512):
    m, d = x.shape
    tile = lambda i: jnp.maximum(i - FFN_LOAD_STEPS, 0)
    row_block = lambda a: pl.BlockSpec((tm, a.shape[1]), lambda i: (tile(i), 0))
    args = [x, g.reshape(1, d), *w_stacks]
    in_specs = [row_block(x), _resident((1, d))] + [_streamed_rows(w, index) for w in w_stacks]
    for a in mix:
        args.append(a)
        in_specs.append(row_block(a))
    if mix:
        args.append(w_mix)
        in_specs.append(_resident(w_mix.shape))
    if w_next is not None:
        args += [next_g.reshape(1, d), w_next]
        in_specs += [_resident((1, d)), _resident(w_next.shape)]
    if final_g is not None:
        args.append(final_g.reshape(1, d))
        in_specs.append(_resident((1, d)))
    out_shape = [jax.ShapeDtypeStruct((m, d), F32)]
    out_specs = [pl.BlockSpec((tm, d), lambda i: (tile(i), 0))]
    if w_next is not None:
        out_shape.append(jax.ShapeDtypeStruct((m, w_next.shape[1]), F32))
        out_specs.append(pl.BlockSpec((tm, w_next.shape[1]), lambda i: (tile(i), 0)))
    lead = len(index)
    outs = pl.pallas_call(
        functools.partial(_ffn_body, len(mix), w_next is not None, final_g is not None),
        out_shape=out_shape,
        grid=(FFN_LOAD_STEPS + m // tm,),
        in_specs=in_specs,
        out_specs=out_specs,
        scratch_shapes=[pltpu.VMEM(w.shape[lead:], BF16) for w in w_stacks],
        compiler_params=_cparams(("arbitrary",)),
        name="ffn",
    )(*args)
    return outs if w_next is not None else outs[0]


def _pool_body(seq_len, cur_ref, prev_ref, next_ref, pw_ref, ps_ref, yp_ref, ext_ref):
    i = pl.program_id(1)
    tl = cur_ref.shape[0]
    pool_width = yp_ref.shape[1]
    group = pool_width // len(POOL_WINDOWS)
    ext_ref[0:HALO, :] = jnp.where(i > 0, prev_ref[...], 0.0)
    ext_ref[HALO:HALO + tl, :] = cur_ref[...]
    ext_ref[HALO + tl:, :] = jnp.where(i < pl.num_programs(1) - 1, next_ref[...], 0.0)

    t = i * tl + lax.broadcasted_iota(jnp.int32, (tl, group), 0)
    for g, w in enumerate(POOL_WINDOWS):
        c0 = g * group
        lo = jnp.clip(t - w // 2, 0, seq_len)
        hi = jnp.clip(t - w // 2 + w, 0, seq_len)
        cnt = (hi - lo).astype(F32)
        s = ext_ref[HALO - w // 2:HALO - w // 2 + tl, c0:c0 + group]
        for d in range(1 - w // 2, w - w // 2):
            s = s + ext_ref[HALO + d:HALO + d + tl, c0:c0 + group]
        p = s / cnt - cur_ref[:, c0:c0 + group]
        y = _dot(p.astype(BF16), pw_ref[g])
        yp_ref[:, c0:c0 + group] = y * ps_ref[:, c0:c0 + group]


def _pool_mixer(proj, pool_w, pool_scale, *, tl=512):
    b, seq_len, _ = proj.shape
    pool_width = pool_scale.shape[0]
    hb = tl // HALO
    last = seq_len // HALO - 1
    return pl.pallas_call(
        functools.partial(_pool_body, seq_len),
        out_shape=jax.ShapeDtypeStruct((b, seq_len, pool_width), F32),
        grid=(b, seq_len // tl),
        in_specs=[
            pl.BlockSpec((None, tl, pool_width), lambda bi, i: (bi, i, 0)),
            pl.BlockSpec((None, HALO, pool_width),
                         lambda bi, i: (bi, jnp.maximum(i * hb - 1, 0), 0)),
            pl.BlockSpec((None, HALO, pool_width),
                         lambda bi, i: (bi, jnp.minimum((i + 1) * hb, last), 0)),
            pl.BlockSpec(pool_w.shape, lambda bi, i: (0, 0, 0)),
            pl.BlockSpec((1, pool_width), lambda bi, i: (0, 0)),
        ],
        out_specs=pl.BlockSpec((None, tl, pool_width), lambda bi, i: (bi, i, 0)),
        scratch_shapes=[pltpu.VMEM((tl + 2 * HALO, pool_width), F32)],
        compiler_params=_cparams(("parallel", "parallel")),
        name="pool_mixer",
    )(proj, proj, proj, pool_w, pool_scale.reshape(1, pool_width))


@functools.lru_cache(maxsize=None)
def _dft_constants():
    n = DFT_N1 * DFT_N2
    half = DFT_N1 // 2
    k1 = np.arange(DFT_K1)
    n1 = np.arange(half)
    ang = 2.0 * np.pi * np.outer(k1, n1) / DFT_N1
    f_fwd = np.zeros((2 * DFT_QP, half))
    f_fwd[:DFT_K1] = np.cos(ang)
    f_fwd[DFT_QP:DFT_QP + DFT_K1] = -np.sin(ang)
    n2 = np.arange(DFT_N2)
    k2 = np.arange(DFT_N2)
    g = np.zeros((DFT_K1, 2 * DFT_N2, 2 * DFT_N2))
    for a in range(DFT_K1):
        ph = 2.0 * np.pi * np.outer(a + DFT_N1 * k2, n2) / n
        gre, gim = np.cos(ph), -np.sin(ph)
        g[a] = np.block([[gre, -gim], [gim, gre]])
    weight = np.full(DFT_K1, 2.0)
    weight[0] = 1.0
    weight[-1] = 1.0
    f_inv = np.zeros((half, 2 * DFT_QP))
    f_inv[:, :DFT_K1] = weight * np.cos(ang.T) / n
    f_inv[:, DFT_QP:DFT_QP + DFT_K1] = -weight * np.sin(ang.T) / n
    return (f_fwd.astype(np.float32), g.astype(np.float32),
            np.transpose(g, (0, 2, 1)).astype(np.float32), f_inv.astype(np.float32))


def _pad_rows(dst_ref, src):
    for n1 in range(DFT_N1 // 2):
        dst_ref[n1 * DFT_PITCH:n1 * DFT_PITCH + DFT_N2, :] = src[n1 * DFT_N2:(n1 + 1) * DFT_N2, :]


def _contract_planes(src_ref, dst_ref, f_ref):
    n_dst, n_src = f_ref.shape

    def step(j, carry):
        r0 = j * PLANE_ROWS_PER_DOT
        x = jnp.concatenate(
            [src_ref[pl.ds(r0 + r, n_src, stride=DFT_PITCH), :].astype(BF16)
             for r in range(PLANE_ROWS_PER_DOT)], axis=1)
        y = _dot(f_ref[...], x)
        for r in range(PLANE_ROWS_PER_DOT):
            dst_ref[pl.ds(r0 + r, n_dst, stride=DFT_PITCH), :] = y[:, r * LANES:(r + 1) * LANES]
        return carry

    lax.fori_loop(0, DFT_N2 // PLANE_ROWS_PER_DOT, step, 0, unroll=PLANE_UNROLL)


def _load_planes(a_ref, k1):
    re = a_ref[pl.ds(pl.multiple_of(k1 * DFT_PITCH, SUBLANES), DFT_N2), :]
    im = a_ref[pl.ds(pl.multiple_of((DFT_QP + k1) * DFT_PITCH, SUBLANES), DFT_N2), :]
    return jnp.concatenate([re, im], axis=0)


def _store_planes(a_ref, k1, z):
    a_ref[pl.ds(pl.multiple_of(k1 * DFT_PITCH, SUBLANES), DFT_N2), :] = z[:DFT_N2]
    a_ref[pl.ds(pl.multiple_of((DFT_QP + k1) * DFT_PITCH, SUBLANES), DFT_N2), :] = z[DFT_N2:]


def _spectrum_body(fw_ref, bw_ref, ssf_ref, ssb_ref, ff_ref, gt_ref, h_ref, u_ref, a_ref):
    norm = lax.rsqrt(ssf_ref[...] + ssb_ref[...]).T
    for part, src_ref in enumerate((fw_ref, bw_ref)):
        _pad_rows(u_ref, src_ref)
        _contract_planes(u_ref, a_ref, ff_ref)
        sign = 1.0 if part == 0 else -1.0

        def step(k1, carry):
            xt = _dot(_load_planes(a_ref, k1).T.astype(BF16), gt_ref[k1])
            re = xt[:, :DFT_N2] * norm
            im = xt[:, DFT_N2:] * (sign * norm)
            if part == 0:
                h_ref[k1, :, :DFT_N2] = re
                h_ref[k1, :, DFT_N2:] = im
            else:
                h_ref[k1, :, :DFT_N2] += re
                h_ref[k1, :, DFT_N2:] += im
            return carry

        lax.fori_loop(0, DFT_K1, step, 0, unroll=K1_UNROLL)


def _filter_spectrum(taps, sumsq, width):
    seq_len = taps.shape[0]
    ff, _, gt, _ = _dft_constants()
    cb = width // LANES
    half = DFT_N1 // 2
    return pl.pallas_call(
        _spectrum_body,
        out_shape=jax.ShapeDtypeStruct((HYENA_ORDER, DFT_K1, width, 2 * DFT_N2), F32),
        grid=(HYENA_ORDER, cb),
        in_specs=[
            pl.BlockSpec((seq_len, LANES), lambda o, c: (0, (2 * o) * cb + c)),
            pl.BlockSpec((seq_len, LANES), lambda o, c: (0, (2 * o + 1) * cb + c)),
            pl.BlockSpec((1, LANES), lambda o, c: (0, (2 * o) * cb + c)),
            pl.BlockSpec((1, LANES), lambda o, c: (0, (2 * o + 1) * cb + c)),
            pl.BlockSpec((2 * DFT_QP, half), lambda o, c: (0, 0)),
            pl.BlockSpec((DFT_K1, 2 * DFT_N2, 2 * DFT_N2), lambda o, c: (0, 0, 0)),
        ],
        out_specs=pl.BlockSpec((None, DFT_K1, LANES, 2 * DFT_N2), lambda o, c: (o, 0, c, 0)),
        scratch_shapes=[pltpu.VMEM((half * DFT_PITCH, LANES), F32),
                        pltpu.VMEM((2 * DFT_QP * DFT_PITCH, LANES), F32)],
        compiler_params=_cparams(("parallel", "parallel")),
        name="filter_spectrum",
    )(taps, taps, sumsq, sumsq, jnp.asarray(ff).astype(BF16), jnp.asarray(gt).astype(BF16))


def _short_conv_rows(x_ref, n1, w_ref, b_ref):
    r0 = n1 * DFT_N2
    cur = x_ref[r0:r0 + DFT_N2, :]
    row = lax.broadcasted_iota(jnp.int32, cur.shape, 0)
    if r0 == 0:
        prev = jnp.where(row == 0, 0.0, pltpu.roll(cur, 1, axis=0))
    else:
        prev = x_ref[r0 - 1:r0 + DFT_N2 - 1, :]
    if r0 + DFT_N2 == x_ref.shape[0]:
        nxt = jnp.where(row == DFT_N2 - 1, 0.0, pltpu.roll(cur, DFT_N2 - 1, axis=0))
    else:
        nxt = x_ref[r0 + 1:r0 + DFT_N2 + 1, :]
    return b_ref[...] + prev * w_ref[0:1, :] + cur * w_ref[1:2, :] + nxt * w_ref[2:3, :]


def _hyena_body(v_ref, gate_ref, vw_ref, vb_ref, gw_ref, gb_ref, h_ref, bias_ref, ff_ref, g_ref,
                gt_ref, fi_ref, o_ref, u_ref, y_ref, a_ref):
    order = pl.program_id(2)
    half = DFT_N1 // 2

    @pl.when(order == 0)
    def _():
        for n1 in range(half):
            u_ref[n1 * DFT_PITCH:n1 * DFT_PITCH + DFT_N2, :] = _short_conv_rows(
                v_ref, n1, vw_ref, vb_ref)

    _contract_planes(u_ref, a_ref, ff_ref)

    def freq_step(k1, carry):
        xt = _dot(_load_planes(a_ref, k1).T.astype(BF16), gt_ref[k1])
        xre, xim = xt[:, :DFT_N2], xt[:, DFT_N2:]
        hre, him = h_ref[k1, :, :DFT_N2], h_ref[k1, :, DFT_N2:]
        yt = jnp.concatenate([xre * hre - xim * him, xre * him + xim * hre], axis=1)
        _store_planes(a_ref, k1, _dot(yt.astype(BF16), g_ref[k1]).T)
        return carry

    lax.fori_loop(0, DFT_K1, freq_step, 0, unroll=K1_UNROLL)
    _contract_planes(a_ref, y_ref, fi_ref)

    bias = bias_ref[...]
    for n1 in range(half):
        rows = slice(n1 * DFT_PITCH, n1 * DFT_PITCH + DFT_N2)
        gate = _short_conv_rows(gate_ref, n1, gw_ref, gb_ref)
        z = gate * (y_ref[rows, :] + u_ref[rows, :] * bias)
        u_ref[rows, :] = z
        o_ref[n1 * DFT_N2:(n1 + 1) * DFT_N2, :] = z


def _hyena_mixer(proj, first_col, conv_w, conv_b, spec, bias):
    assert HYENA_SHORT_CONV == 3
    b, seq_len, _ = proj.shape
    width = bias.shape[1]
    cb = width // LANES
    c0 = first_col // LANES
    half = DFT_N1 // 2
    ff, g, gt, fi = _dft_constants()
    value_col = lambda c, bi, o: HYENA_ORDER * cb + c
    gate_col = lambda c, bi, o: o * cb + c
    conv_b = conv_b.reshape(1, -1)
    return pl.pallas_call(
        _hyena_body,
        out_shape=jax.ShapeDtypeStruct((b, seq_len, width), F32),
        grid=(cb, b, HYENA_ORDER),
        in_specs=[
            pl.BlockSpec((None, seq_len, LANES), lambda c, bi, o: (bi, 0, c0 + value_col(c, bi, o))),
            pl.BlockSpec((None, seq_len, LANES), lambda c, bi, o: (bi, 0, c0 + gate_col(c, bi, o))),
            pl.BlockSpec((HYENA_SHORT_CONV, LANES), lambda c, bi, o: (0, value_col(c, bi, o))),
            pl.BlockSpec((1, LANES), lambda c, bi, o: (0, value_col(c, bi, o))),
            pl.BlockSpec((HYENA_SHORT_CONV, LANES), lambda c, bi, o: (0, gate_col(c, bi, o))),
            pl.BlockSpec((1, LANES), lambda c, bi, o: (0, gate_col(c, bi, o))),
            pl.BlockSpec((None, DFT_K1, LANES, 2 * DFT_N2), lambda c, bi, o: (o, 0, c, 0)),
            pl.BlockSpec((None, 1, LANES), lambda c, bi, o: (o, 0, c)),
            _resident((2 * DFT_QP, half)),
            _resident((DFT_K1, 2 * DFT_N2, 2 * DFT_N2)),
            _resident((DFT_K1, 2 * DFT_N2, 2 * DFT_N2)),
            _resident((half, 2 * DFT_QP)),
        ],
        out_specs=pl.BlockSpec((None, seq_len, LANES), lambda c, bi, o: (bi, 0, c)),
        scratch_shapes=[pltpu.VMEM((half * DFT_PITCH, LANES), F32),
                        pltpu.VMEM((half * DFT_PITCH, LANES), F32),
                        pltpu.VMEM((2 * DFT_QP * DFT_PITCH, LANES), F32)],
        compiler_params=_cparams(("parallel", "parallel", "arbitrary")),
        name="hyena_mixer",
    )(proj, proj, conv_w, conv_b, conv_w, conv_b, spec, bias.reshape(HYENA_ORDER, 1, width),
      jnp.asarray(ff).astype(BF16), jnp.asarray(g).astype(BF16), jnp.asarray(gt).astype(BF16),
      jnp.asarray(fi).astype(BF16))


def _hdot(a, b):
    return jnp.dot(a, b, preferred_element_type=F32, precision=lax.Precision.HIGHEST)


def _sin_half_lanes(arg):
    rows = arg.shape[0] // 2
    lane = lax.broadcasted_iota(jnp.int32, (rows, LANES), 1)
    packed = jnp.where(lane < LANES // 2, arg[:rows], pltpu.roll(arg[rows:], LANES // 2, axis=1))
    s = jnp.sin(packed)
    return jnp.concatenate([s, pltpu.roll(s, LANES // 2, axis=1)], axis=0)


def _filter_taps_body(pack_sin, bwd_cols, z_ref, t_ref, w1_ref, b1_ref, w2_ref, b2_ref, w3_ref,
                      b3_ref, sf_ref, wo_ref, decay_ref, taps_ref, ss_ref):
    i = pl.program_id(0)
    sin = _sin_half_lanes if pack_sin else jnp.sin
    h = sin(sf_ref[0:1, :] * (_hdot(z_ref[...], w1_ref[...]) + b1_ref[...]))
    h = sin(sf_ref[1:2, :] * (_hdot(h, w2_ref[...]) + b2_ref[...]))
    h = sin(sf_ref[2:3, :] * (_hdot(h, w3_ref[...]) + b3_ref[...]))
    h = _dot(h.astype(BF16), wo_ref[...])
    h = h * jnp.exp(-t_ref[...] * jnp.abs(decay_ref[...]))
    row = i * h.shape[0] + lax.broadcasted_iota(jnp.int32, h.shape, 0)
    h = jnp.where((row == 0) & (bwd_cols[...] > 0.0), 0.0, h)
    taps_ref[...] = h

    @pl.when(i == 0)
    def _():
        ss_ref[...] = jnp.zeros_like(ss_ref)

    ss_ref[...] += jnp.sum(h * h, axis=0, keepdims=True)


def _filter_taps(z, t, w1, b1, w2, b2, w3, b3, sin_freq, w_out, decay, *, tl=512):
    seq_len = z.shape[0]
    hid = LANES
    n_out = w_out.shape[1]
    width = n_out // (2 * HYENA_ORDER)

    def pad2(a, rows, cols):
        return jnp.pad(a, ((0, rows - a.shape[0]), (0, cols - a.shape[1])))

    is_bwd = np.tile(np.repeat(np.array([0.0, 1.0], np.float32), width), HYENA_ORDER)[None, :]
    full = lambda a: pl.BlockSpec(a.shape, lambda i: (0, 0))
    args = (
        pad2(z, seq_len, hid), t.reshape(seq_len, 1),
        pad2(w1, hid, hid), pad2(b1[None, :], 1, hid),
        pad2(w2, hid, hid), pad2(b2[None, :], 1, hid),
        pad2(w3, hid, hid), pad2(b3[None, :], 1, hid),
        pad2(sin_freq, 3, hid), pad2(w_out, hid, n_out).astype(BF16), decay.reshape(1, n_out),
    )
    bwd_cols = jnp.asarray(is_bwd)
    return pl.pallas_call(
        functools.partial(_filter_taps_body, w1.shape[1] <= LANES // 2 and tl % 16 == 0),
        out_shape=(jax.ShapeDtypeStruct((seq_len, n_out), F32),
                   jax.ShapeDtypeStruct((1, n_out), F32)),
        grid=(seq_len // tl,),
        in_specs=[full(bwd_cols),
                  pl.BlockSpec((tl, hid), lambda i: (i, 0)),
                  pl.BlockSpec((tl, 1), lambda i: (i, 0))] + [full(a) for a in args[2:]],
        out_specs=(pl.BlockSpec((tl, n_out), lambda i: (i, 0)),
                   pl.BlockSpec((1, n_out), lambda i: (0, 0))),
        compiler_params=_cparams(("arbitrary",)),
        name="filter_taps",
    )(bwd_cols, *args)


def _rope_slab(x, cos, sin):
    half = MLA_ROPE // 2
    lane = lax.broadcasted_iota(jnp.int32, x.shape, 1)
    swapped = jnp.where(lane < MLA_NOPE + half, pltpu.roll(x, LANES - half, axis=1),
                        pltpu.roll(x, half, axis=1))
    return x * cos + swapped * sin


def _mla_qkv_body(c_ref, qg_ref, kvg_ref, wq_ref, wk_ref, wv_ref, cos_ref, sin_ref,
                  q_ref, k_ref, v_ref):
    c = c_ref[...]
    cos = cos_ref[...]
    sin = sin_ref[...]
    cq = _rms(c[:, :MLA_Q_LORA], qg_ref[...]).astype(BF16)
    ckv = _rms(c[:, MLA_Q_LORA:MLA_Q_LORA + MLA_KV_LORA], kvg_ref[...]).astype(BF16)
    kr0 = MLA_Q_LORA + MLA_KV_LORA
    k_rope = _rope_slab(c[:, kr0:kr0 + LANES], cos, sin)
    cos_q = cos * SOFTMAX_EXP2_SCALE
    sin_q = sin * SOFTMAX_EXP2_SCALE
    ones_lane = (lax.broadcasted_iota(jnp.int32, (1, LANES), 1) == MLA_V).astype(F32)
    for h0 in range(0, MLA_HEADS, 2):
        cols = slice(h0 * LANES, (h0 + 2) * LANES)
        q2 = _dot(cq, wq_ref[:, cols])
        k2 = _dot(ckv, wk_ref[:, cols])
        v2 = _dot(ckv, wv_ref[:, cols])
        for j in range(2):
            lanes = slice(j * LANES, (j + 1) * LANES)
            q_ref[h0 + j] = _rope_slab(q2[:, lanes], cos_q, sin_q).astype(BF16)
            k_ref[h0 + j] = (k2[:, lanes] + k_rope).astype(BF16)
            v_ref[h0 + j] = (v2[:, lanes] + ones_lane).astype(BF16)


def _mla_qkv(c, q_norm_g, kv_norm_g, wq, wk, wv, cos_t, sin_t, batch, *, tl=512):
    m, cw = c.shape
    seq_len = m // batch
    nl = seq_len // tl
    full = lambda a: pl.BlockSpec(a.shape, lambda bi, i: (0, 0))
    qg = q_norm_g.reshape(1, -1)
    kvg = kv_norm_g.reshape(1, -1)
    head_out = jax.ShapeDtypeStruct((batch, MLA_HEADS, seq_len, LANES), BF16)
    head_spec = pl.BlockSpec((None, MLA_HEADS, tl, LANES), lambda bi, i: (bi, 0, i, 0))
    return pl.pallas_call(
        _mla_qkv_body,
        out_shape=(head_out, head_out, head_out),
        grid=(batch, nl),
        in_specs=[pl.BlockSpec((tl, cw), lambda bi, i: (bi * nl + i, 0)),
                  full(qg), full(kvg), full(wq), full(wk), full(wv),
                  pl.BlockSpec((tl, LANES), lambda bi, i: (i, 0)),
                  pl.BlockSpec((tl, LANES), lambda bi, i: (i, 0))],
        out_specs=(head_spec, head_spec, head_spec),
        compiler_params=_cparams(("parallel", "parallel")),
        name="mla_qkv",
    )(c, qg, kvg, wq, wk, wv, cos_t, sin_t)


def _attention_body(q_ref, qn_ref, k_ref, v_ref, o_ref, *s_refs):
    pair = LANES // MLA_V
    n_heads = q_ref.shape[0]

    def scores(q, h):
        return lax.dot_general(q, k_ref[h], (((1,), (1,)), ((), ())), preferred_element_type=F32)

    @pl.when(pl.program_id(2) == 0)
    def _():
        for h, s_ref in enumerate(s_refs):
            s_ref[...] = scores(q_ref[h], h)

    ahead = {h: s_ref[...] for h, s_ref in enumerate(s_refs)}
    for g in range(n_heads // pair):
        outs = []
        for h in range(g * pair, (g + 1) * pair):
            lead = h + SCORE_LEAD
            if lead < n_heads:
                ahead[lead] = scores(q_ref[lead], lead)
            else:
                s_refs[lead - n_heads][...] = scores(qn_ref[lead - n_heads], lead - n_heads)
            s = ahead.pop(h)
            p = jnp.exp2(s - jnp.max(s, axis=-1, keepdims=True))
            r = _dot(p.astype(BF16), v_ref[h])
            outs.append(r / r[:, MLA_V:MLA_V + 1])
        lane = lax.broadcasted_iota(jnp.int32, outs[0].shape, 1)
        both = jnp.where(lane < MLA_V, outs[0], pltpu.roll(outs[1], MLA_V, axis=1))
        o_ref[:, g * LANES:(g + 1) * LANES] = both.astype(o_ref.dtype)


def _attention(q, k, v, *, tq=256, group=8):
    batch, heads, seq_len, _ = q.shape
    width = group * MLA_V
    n_tiles = seq_len // tq
    return pl.pallas_call(
        _attention_body,
        out_shape=jax.ShapeDtypeStruct((batch, seq_len, heads * MLA_V), BF16),
        grid=(batch, heads // group, n_tiles),
        in_specs=[
            pl.BlockSpec((None, group, tq, LANES), lambda bi, p, i: (bi, p, i, 0)),
            pl.BlockSpec((None, SCORE_LEAD, tq, LANES),
                         lambda bi, p, i: (bi, p * (group // SCORE_LEAD),
                                           jnp.minimum(i + 1, n_tiles - 1), 0)),
            pl.BlockSpec((None, group, seq_len, LANES), lambda bi, p, i: (bi, p, 0, 0)),
            pl.BlockSpec((None, group, seq_len, LANES), lambda bi, p, i: (bi, p, 0, 0)),
        ],
        out_specs=pl.BlockSpec((None, tq, width), lambda bi, p, i: (bi, i, p)),
        scratch_shapes=[pltpu.VMEM((tq, seq_len), F32)] * SCORE_LEAD,
        compiler_params=_cparams(("parallel", "parallel", "arbitrary")),
        name="attention",
    )(q, q, k, v)


def _mla_weights(w_dq, w_uq, w_dkv, w_ukv):
    d = w_dq.shape[0]
    qh = w_uq.reshape(MLA_Q_LORA, MLA_HEADS, MLA_NOPE + MLA_ROPE)
    zq = jnp.zeros((MLA_Q_LORA, MLA_HEADS, LANES - MLA_NOPE - MLA_ROPE), F32)
    wq = jnp.concatenate([qh, zq], axis=-1).reshape(MLA_Q_LORA, MLA_HEADS * LANES)
    kvh = w_ukv.reshape(MLA_KV_LORA, MLA_HEADS, MLA_NOPE + MLA_V)
    wk = jnp.concatenate([kvh[..., :MLA_NOPE],
                          jnp.zeros((MLA_KV_LORA, MLA_HEADS, LANES - MLA_NOPE), F32)],
                         axis=-1).reshape(MLA_KV_LORA, MLA_HEADS * LANES)
    wv = jnp.concatenate([kvh[..., MLA_NOPE:],
                          jnp.zeros((MLA_KV_LORA, MLA_HEADS, LANES - MLA_V), F32)],
                         axis=-1).reshape(MLA_KV_LORA, MLA_HEADS * LANES)
    w_kr = jnp.concatenate([jnp.zeros((d, MLA_NOPE), F32), w_dkv[:, MLA_KV_LORA:],
                            jnp.zeros((d, LANES - MLA_NOPE - MLA_ROPE), F32)], axis=-1)
    w_down = jnp.concatenate([w_dq, w_dkv[:, :MLA_KV_LORA], w_kr], axis=-1)
    return w_down.astype(BF16), wq.astype(BF16), wk.astype(BF16), wv.astype(BF16)


def _rope_tables(seq_len):
    inv_freq = ROPE_THETA ** (-jnp.arange(0, MLA_ROPE, 2, dtype=F32) / MLA_ROPE)
    ang = jnp.arange(seq_len, dtype=F32)[:, None] * inv_freq[None, :]
    cos = jnp.cos(ang)
    sin = jnp.sin(ang)
    tail = LANES - MLA_NOPE - MLA_ROPE
    cos_t = jnp.concatenate([jnp.ones((seq_len, MLA_NOPE), F32), cos, cos,
                             jnp.ones((seq_len, tail), F32)], axis=-1)
    sin_t = jnp.concatenate([jnp.zeros((seq_len, MLA_NOPE), F32), -sin, sin,
                             jnp.zeros((seq_len, tail), F32)], axis=-1)
    return cos_t, sin_t


def _position_features(seq_len, pos_dim):
    t = jnp.linspace(0.0, 1.0, seq_len, dtype=F32)
    bands = (pos_dim - 1) // 2
    w = 2.0 * math.pi * jnp.arange(seq_len, dtype=F32) / seq_len
    f = jnp.linspace(1e-4, bands - 1, bands, dtype=F32)
    phase = w[:, None] * f[None, :]
    z = jnp.concatenate([t[:, None], jnp.cos(phase), -jnp.sin(phase)], axis=-1)
    return t, z


def kernel(x, norm_g, ffn_w_gate, ffn_w_up, ffn_w_down, mix_w_in, pool_w, pool_scale, hyena_conv_w, hyena_conv_b, hyena_ffn_w1, hyena_ffn_b1, hyena_ffn_w2, hyena_ffn_b2, hyena_ffn_w3, hyena_ffn_b3, hyena_sin_freq, hyena_ffn_w_out, hyena_decay, hyena_bias, mix_w_out, mla_w_dq, mla_q_norm_g, mla_w_uq, mla_w_dkv, mla_kv_norm_g, mla_w_ukv, mla_w_o, final_norm_g):
    batch, seq_len, d = x.shape
    depth = norm_g.shape[0]
    assert 2 * seq_len == DFT_N1 * DFT_N2
    hy_width = hyena_bias.shape[-1]
    t_pos, z_pos = _position_features(seq_len, hyena_ffn_w1.shape[1])
    cos_t, sin_t = _rope_tables(seq_len)

    x = x.reshape(batch * seq_len, d)
    ffn_w = (ffn_w_gate, ffn_w_up, ffn_w_down)
    for i in range(depth):
        j = i // 2
        if i % 2 == 0:
            x, proj = _ffn(x, norm_g[i, 0], ffn_w, (i, 0), next_g=norm_g[i, 1],
                           w_next=mix_w_in[j].astype(BF16))
            proj = proj.reshape(batch, seq_len, -1)
            y_pool = _pool_mixer(proj, pool_w[j].astype(BF16), pool_scale[j])
            taps, sumsq = _filter_taps(z_pos, t_pos, hyena_ffn_w1[j], hyena_ffn_b1[j],
                                       hyena_ffn_w2[j], hyena_ffn_b2[j], hyena_ffn_w3[j],
                                       hyena_ffn_b3[j], hyena_sin_freq[j], hyena_ffn_w_out[j],
                                       hyena_decay[j])
            spec = _filter_spectrum(taps, sumsq, hy_width)
            y_hyena = _hyena_mixer(proj, pool_scale.shape[-1], hyena_conv_w[j], hyena_conv_b[j],
                                   spec, hyena_bias[j])
            mix = (y_pool.reshape(batch * seq_len, -1), y_hyena.reshape(batch * seq_len, -1))
            w_mix = mix_w_out[j].astype(BF16)
        else:
            w_down, wq, wk, wv = _mla_weights(mla_w_dq[j], mla_w_uq[j], mla_w_dkv[j],
                                              mla_w_ukv[j])
            x, c = _ffn(x, norm_g[i, 0], ffn_w, (i, 0), next_g=norm_g[i, 1], w_next=w_down)
            q, k, v = _mla_qkv(c, mla_q_norm_g[j], mla_kv_norm_g[j], wq, wk, wv,
                               cos_t, sin_t, batch)
            mix = (_attention(q, k, v).reshape(batch * seq_len, -1),)
            w_mix = mla_w_o[j].astype(BF16)
        x = _ffn(x, norm_g[i, 2], ffn_w, (i, 1), mix=mix, w_mix=w_mix,
                 final_g=final_norm_g if i == depth - 1 else None)
    return x.reshape(batch, seq_len, d)
```

```python
import functools
import math

import numpy as np
import jax
import jax.numpy as jnp
from jax import lax
from jax.experimental import pallas as pl
from jax.experimental.pallas import tpu as pltpu

F32 = jnp.float32
BF16 = jnp.bfloat16

RMS_EPS = 1e-6
MACARON_WEIGHT = 0.5
POOL_WINDOWS = (2, 4, 8, 16)
HYENA_ORDER = 2
HYENA_SHORT_CONV = 3
MLA_HEADS = 16
MLA_Q_LORA = 256
MLA_KV_LORA = 128
MLA_NOPE = 64
MLA_ROPE = 32
MLA_V = 64
ROPE_THETA = 10000.0
SOFTMAX_EXP2_SCALE = (MLA_NOPE + MLA_ROPE) ** -0.5 * math.log2(math.e)

LANES = 128
SUBLANES = 8
VMEM_LIMIT = 56 * 2**20

DFT_N1 = 64
DFT_N2 = 128
DFT_K1 = DFT_N1 // 2 + 1
DFT_QP = 40
DFT_PITCH = DFT_N2 + SUBLANES
PLANE_ROWS_PER_DOT = 8
PLANE_UNROLL = 16
K1_UNROLL = 33
HALO = 16
FFN_LOAD_STEPS = 8
SCORE_LEAD = 2


def _cparams(semantics):
    return pltpu.CompilerParams(dimension_semantics=semantics, vmem_limit_bytes=VMEM_LIMIT)


def _rms(x, g):
    return x * lax.rsqrt(jnp.mean(x * x, axis=-1, keepdims=True) + RMS_EPS) * g


def _dot(a, b):
    return jnp.dot(a, b, preferred_element_type=F32)


def _ffn_body(n_mix, has_next, has_final, *refs):
    refs = list(refs)
    x_ref, g_ref, wg_ref, wu_ref, wd_ref = refs[:5]
    del refs[:5]
    mix_refs = [refs.pop(0) for _ in range(n_mix)]
    w_mix_ref = refs.pop(0) if n_mix else None
    next_g_ref, next_w_ref = (refs.pop(0), refs.pop(0)) if has_next else (None, None)
    final_g_ref = refs.pop(0) if has_final else None
    o_ref = refs.pop(0)
    next_o_ref = refs.pop(0) if has_next else None
    wg_bf, wu_bf, wd_bf = refs
    step = pl.program_id(0)

    @pl.when(step < FFN_LOAD_STEPS)
    def _():
        for src_ref, dst_ref in ((wg_ref, wg_bf), (wu_ref, wu_bf), (wd_ref, wd_bf)):
            rows = src_ref.shape[0]
            dst_ref[pl.ds(pl.multiple_of(step * rows, rows), rows), :] = src_ref[...].astype(BF16)

    @pl.when(step >= FFN_LOAD_STEPS)
    def _():
        x = x_ref[...]
        row = 0
        for a_ref in mix_refs:
            k = a_ref.shape[1]
            x = x + _dot(a_ref[...].astype(BF16), w_mix_ref[row:row + k, :])
            row += k
        h = _rms(x, g_ref[...]).astype(BF16)
        gate = _dot(h, wg_bf[...])
        up = _dot(h, wu_bf[...])
        act = gate / (1.0 + jnp.exp(-gate)) * up
        y = x + MACARON_WEIGHT * _dot(act.astype(BF16), wd_bf[...])
        if has_final:
            y = _rms(y, final_g_ref[...])
        o_ref[...] = y
        if has_next:
            next_o_ref[...] = _dot(_rms(y, next_g_ref[...]).astype(BF16), next_w_ref[...])


def _resident(shape):
    return pl.BlockSpec(shape, lambda *_: (0,) * len(shape), pipeline_mode=pl.Buffered(1))


def _streamed_rows(stack, index):
    lead = len(index)
    rows, cols = stack.shape[lead:]
    return pl.BlockSpec(
        (None,) * lead + (rows // FFN_LOAD_STEPS, cols),
        lambda i: tuple(index) + (jnp.minimum(i, FFN_LOAD_STEPS - 1), 0))


def _ffn(x, g, w_stacks, index, *, mix=(), w_mix=None, next_g=None, w_next=None, final_g=None,
         tm=512):
    m, d = x.shape
    tile = lambda i: jnp.maximum(i - FFN_LOAD_STEPS, 0)
    row_block = lambda a: pl.BlockSpec((tm, a.shape[1]), lambda i: (tile(i), 0))
    args = [x, g.reshape(1, d), *w_stacks]
    in_specs = [row_block(x), _resident((1, d))] + [_streamed_rows(w, index) for w in w_stacks]
    for a in mix:
        args.append(a)
        in_specs.append(row_block(a))
    if mix:
        args.append(w_mix)
        in_specs.append(_resident(w_mix.shape))
    if w_next is not None:
        args += [next_g.reshape(1, d), w_next]
        in_specs += [_resident((1, d)), _resident(w_next.shape)]
    if final_g is not None:
        args.append(final_g.reshape(1, d))
        in_specs.append(_resident((1, d)))
    out_shape = [jax.ShapeDtypeStruct((m, d), F32)]
    out_specs = [pl.BlockSpec((tm, d), lambda i: (tile(i), 0))]
    if w_next is not None:
        out_shape.append(jax.ShapeDtypeStruct((m, w_next.shape[1]), F32))
        out_specs.append(pl.BlockSpec((tm, w_next.shape[1]), lambda i: (tile(i), 0)))
    lead = len(index)
    outs = pl.pallas_call(
        functools.partial(_ffn_body, len(mix), w_next is not None, final_g is not None),
        out_shape=out_shape,
        grid=(FFN_LOAD_STEPS + m // tm,),
        in_specs=in_specs,
        out_specs=out_specs,
        scratch_shapes=[pltpu.VMEM(w.shape[lead:], BF16) for w in w_stacks],
        compiler_params=_cparams(("arbitrary",)),
        name="ffn",
    )(*args)
    return outs if w_next is not None else outs[0]


def _pool_body(seq_len, cur_ref, prev_ref, next_ref, pw_ref, ps_ref, yp_ref, ext_ref):
    i = pl.program_id(1)
    tl = cur_ref.shape[0]
    pool_width = yp_ref.shape[1]
    group = pool_width // len(POOL_WINDOWS)
    ext_ref[0:HALO, :] = jnp.where(i > 0, prev_ref[...], 0.0)
    ext_ref[HALO:HALO + tl, :] = cur_ref[...]
    ext_ref[HALO + tl:, :] = jnp.where(i < pl.num_programs(1) - 1, next_ref[...], 0.0)

    t = i * tl + lax.broadcasted_iota(jnp.int32, (tl, group), 0)
    for g, w in enumerate(POOL_WINDOWS):
        c0 = g * group
        lo = jnp.clip(t - w // 2, 0, seq_len)
        hi = jnp.clip(t - w // 2 + w, 0, seq_len)
        cnt = (hi - lo).astype(F32)
        s = ext_ref[HALO - w // 2:HALO - w // 2 + tl, c0:c0 + group]
        for d in range(1 - w // 2, w - w // 2):
            s = s + ext_ref[HALO + d:HALO + d + tl, c0:c0 + group]
        p = s / cnt - cur_ref[:, c0:c0 + group]
        y = _dot(p.astype(BF16), pw_ref[g])
        yp_ref[:, c0:c0 + group] = y * ps_ref[:, c0:c0 + group]


def _pool_mixer(proj, pool_w, pool_scale, *, tl=512):
    b, seq_len, _ = proj.shape
    pool_width = pool_scale.shape[0]
    hb = tl // HALO
    last = seq_len // HALO - 1
    return pl.pallas_call(
        functools.partial(_pool_body, seq_len),
        out_shape=jax.ShapeDtypeStruct((b, seq_len, pool_width), F32),
        grid=(b, seq_len // tl),
        in_specs=[
            pl.BlockSpec((None, tl, pool_width), lambda bi, i: (bi, i, 0)),
            pl.BlockSpec((None, HALO, pool_width),
                         lambda bi, i: (bi, jnp.maximum(i * hb - 1, 0), 0)),
            pl.BlockSpec((None, HALO, pool_width),
                         lambda bi, i: (bi, jnp.minimum((i + 1) * hb, last), 0)),
            pl.BlockSpec(pool_w.shape, lambda bi, i: (0, 0, 0)),
            pl.BlockSpec((1, pool_width), lambda bi, i: (0, 0)),
        ],
        out_specs=pl.BlockSpec((None, tl, pool_width), lambda bi, i: (bi, i, 0)),
        scratch_shapes=[pltpu.VMEM((tl + 2 * HALO, pool_width), F32)],
        compiler_params=_cparams(("parallel", "parallel")),
        name="pool_mixer",
    )(proj, proj, proj, pool_w, pool_scale.reshape(1, pool_width))


@functools.lru_cache(maxsize=None)
def _dft_constants():
    n = DFT_N1 * DFT_N2
    half = DFT_N1 // 2
    k1 = np.arange(DFT_K1)
    n1 = np.arange(half)
    ang = 2.0 * np.pi * np.outer(k1, n1) / DFT_N1
    f_fwd = np.zeros((2 * DFT_QP, half))
    f_fwd[:DFT_K1] = np.cos(ang)
    f_fwd[DFT_QP:DFT_QP + DFT_K1] = -np.sin(ang)
    n2 = np.arange(DFT_N2)
    k2 = np.arange(DFT_N2)
    g = np.zeros((DFT_K1, 2 * DFT_N2, 2 * DFT_N2))
    for a in range(DFT_K1):
        ph = 2.0 * np.pi * np.outer(a + DFT_N1 * k2, n2) / n
        gre, gim = np.cos(ph), -np.sin(ph)
        g[a] = np.block([[gre, -gim], [gim, gre]])
    weight = np.full(DFT_K1, 2.0)
    weight[0] = 1.0
    weight[-1] = 1.0
    f_inv = np.zeros((half, 2 * DFT_QP))
    f_inv[:, :DFT_K1] = weight * np.cos(ang.T) / n
    f_inv[:, DFT_QP:DFT_QP + DFT_K1] = -weight * np.sin(ang.T) / n
    return (f_fwd.astype(np.float32), g.astype(np.float32),
            np.transpose(g, (0, 2, 1)).astype(np.float32), f_inv.astype(np.float32))


def _pad_rows(dst_ref, src):
    for n1 in range(DFT_N1 // 2):
        dst_ref[n1 * DFT_PITCH:n1 * DFT_PITCH + DFT_N2, :] = src[n1 * DFT_N2:(n1 + 1) * DFT_N2, :]


def _contract_planes(src_ref, dst_ref, f_ref):
    n_dst, n_src = f_ref.shape

    def step(j, carry):
        r0 = j * PLANE_ROWS_PER_DOT
        x = jnp.concatenate(
            [src_ref[pl.ds(r0 + r, n_src, stride=DFT_PITCH), :].astype(BF16)
             for r in range(PLANE_ROWS_PER_DOT)], axis=1)
        y = _dot(f_ref[...], x)
        for r in range(PLANE_ROWS_PER_DOT):
            dst_ref[pl.ds(r0 + r, n_dst, stride=DFT_PITCH), :] = y[:, r * LANES:(r + 1) * LANES]
        return carry

    lax.fori_loop(0, DFT_N2 // PLANE_ROWS_PER_DOT, step, 0, unroll=PLANE_UNROLL)


def _load_planes(a_ref, k1):
    re = a_ref[pl.ds(pl.multiple_of(k1 * DFT_PITCH, SUBLANES), DFT_N2), :]
    im = a_ref[pl.ds(pl.multiple_of((DFT_QP + k1) * DFT_PITCH, SUBLANES), DFT_N2), :]
    return jnp.concatenate([re, im], axis=0)


def _store_planes(a_ref, k1, z):
    a_ref[pl.ds(pl.multiple_of(k1 * DFT_PITCH, SUBLANES), DFT_N2), :] = z[:DFT_N2]
    a_ref[pl.ds(pl.multiple_of((DFT_QP + k1) * DFT_PITCH, SUBLANES), DFT_N2), :] = z[DFT_N2:]


def _spectrum_body(fw_ref, bw_ref, ssf_ref, ssb_ref, ff_ref, gt_ref, h_ref, u_ref, a_ref):
    norm = lax.rsqrt(ssf_ref[...] + ssb_ref[...]).T
    for part, src_ref in enumerate((fw_ref, bw_ref)):
        _pad_rows(u_ref, src_ref)
        _contract_planes(u_ref, a_ref, ff_ref)
        sign = 1.0 if part == 0 else -1.0

        def step(k1, carry):
            xt = _dot(_load_planes(a_ref, k1).T.astype(BF16), gt_ref[k1])
            re = xt[:, :DFT_N2] * norm
            im = xt[:, DFT_N2:] * (sign * norm)
            if part == 0:
                h_ref[k1, :, :DFT_N2] = re
                h_ref[k1, :, DFT_N2:] = im
            else:
                h_ref[k1, :, :DFT_N2] += re
                h_ref[k1, :, DFT_N2:] += im
            return carry

        lax.fori_loop(0, DFT_K1, step, 0, unroll=K1_UNROLL)


def _filter_spectrum(taps, sumsq, width):
    seq_len = taps.shape[0]
    ff, _, gt, _ = _dft_constants()
    cb = width // LANES
    half = DFT_N1 // 2
    return pl.pallas_call(
        _spectrum_body,
        out_shape=jax.ShapeDtypeStruct((HYENA_ORDER, DFT_K1, width, 2 * DFT_N2), F32),
        grid=(HYENA_ORDER, cb),
        in_specs=[
            pl.BlockSpec((seq_len, LANES), lambda o, c: (0, (2 * o) * cb + c)),
            pl.BlockSpec((seq_len, LANES), lambda o, c: (0, (2 * o + 1) * cb + c)),
            pl.BlockSpec((1, LANES), lambda o, c: (0, (2 * o) * cb + c)),
            pl.BlockSpec((1, LANES), lambda o, c: (0, (2 * o + 1) * cb + c)),
            pl.BlockSpec((2 * DFT_QP, half), lambda o, c: (0, 0)),
            pl.BlockSpec((DFT_K1, 2 * DFT_N2, 2 * DFT_N2), lambda o, c: (0, 0, 0)),
        ],
        out_specs=pl.BlockSpec((None, DFT_K1, LANES, 2 * DFT_N2), lambda o, c: (o, 0, c, 0)),
        scratch_shapes=[pltpu.VMEM((half * DFT_PITCH, LANES), F32),
                        pltpu.VMEM((2 * DFT_QP * DFT_PITCH, LANES), F32)],
        compiler_params=_cparams(("parallel", "parallel")),
        name="filter_spectrum",
    )(taps, taps, sumsq, sumsq, jnp.asarray(ff).astype(BF16), jnp.asarray(gt).astype(BF16))


def _short_conv_rows(x_ref, n1, w_ref, b_ref):
    r0 = n1 * DFT_N2
    cur = x_ref[r0:r0 + DFT_N2, :]
    row = lax.broadcasted_iota(jnp.int32, cur.shape, 0)
    if r0 == 0:
        prev = jnp.where(row == 0, 0.0, pltpu.roll(cur, 1, axis=0))
    else:
        prev = x_ref[r0 - 1:r0 + DFT_N2 - 1, :]
    if r0 + DFT_N2 == x_ref.shape[0]:
        nxt = jnp.where(row == DFT_N2 - 1, 0.0, pltpu.roll(cur, DFT_N2 - 1, axis=0))
    else:
        nxt = x_ref[r0 + 1:r0 + DFT_N2 + 1, :]
    return b_ref[...] + prev * w_ref[0:1, :] + cur * w_ref[1:2, :] + nxt * w_ref[2:3, :]


def _hyena_body(v_ref, gate_ref, vw_ref, vb_ref, gw_ref, gb_ref, h_ref, bias_ref, ff_ref, g_ref,
                gt_ref, fi_ref, o_ref, u_ref, y_ref, a_ref):
    order = pl.program_id(2)
    half = DFT_N1 // 2

    @pl.when(order == 0)
    def _():
        for n1 in range(half):
            u_ref[n1 * DFT_PITCH:n1 * DFT_PITCH + DFT_N2, :] = _short_conv_rows(
                v_ref, n1, vw_ref, vb_ref)

    _contract_planes(u_ref, a_ref, ff_ref)

    def freq_step(k1, carry):
        xt = _dot(_load_planes(a_ref, k1).T.astype(BF16), gt_ref[k1])
        xre, xim = xt[:, :DFT_N2], xt[:, DFT_N2:]
        hre, him = h_ref[k1, :, :DFT_N2], h_ref[k1, :, DFT_N2:]
        yt = jnp.concatenate([xre * hre - xim * him, xre * him + xim * hre], axis=1)
        _store_planes(a_ref, k1, _dot(yt.astype(BF16), g_ref[k1]).T)
        return carry

    lax.fori_loop(0, DFT_K1, freq_step, 0, unroll=K1_UNROLL)
    _contract_planes(a_ref, y_ref, fi_ref)

    bias = bias_ref[...]
    for n1 in range(half):
        rows = slice(n1 * DFT_PITCH, n1 * DFT_PITCH + DFT_N2)
        gate = _short_conv_rows(gate_ref, n1, gw_ref, gb_ref)
        z = gate * (y_ref[rows, :] + u_ref[rows, :] * bias)
        u_ref[rows, :] = z
        o_ref[n1 * DFT_N2:(n1 + 1) * DFT_N2, :] = z


def _hyena_mixer(proj, first_col, conv_w, conv_b, spec, bias):
    assert HYENA_SHORT_CONV == 3
    b, seq_len, _ = proj.shape
    width = bias.shape[1]
    cb = width // LANES
    c0 = first_col // LANES
    half = DFT_N1 // 2
    ff, g, gt, fi = _dft_constants()
    value_col = lambda c, bi, o: HYENA_ORDER * cb + c
    gate_col = lambda c, bi, o: o * cb + c
    conv_b = conv_b.reshape(1, -1)
    return pl.pallas_call(
        _hyena_body,
        out_shape=jax.ShapeDtypeStruct((b, seq_len, width), F32),
        grid=(cb, b, HYENA_ORDER),
        in_specs=[
            pl.BlockSpec((None, seq_len, LANES), lambda c, bi, o: (bi, 0, c0 + value_col(c, bi, o))),
            pl.BlockSpec((None, seq_len, LANES), lambda c, bi, o: (bi, 0, c0 + gate_col(c, bi, o))),
            pl.BlockSpec((HYENA_SHORT_CONV, LANES), lambda c, bi, o: (0, value_col(c, bi, o))),
            pl.BlockSpec((1, LANES), lambda c, bi, o: (0, value_col(c, bi, o))),
            pl.BlockSpec((HYENA_SHORT_CONV, LANES), lambda c, bi, o: (0, gate_col(c, bi, o))),
            pl.BlockSpec((1, LANES), lambda c, bi, o: (0, gate_col(c, bi, o))),
            pl.BlockSpec((None, DFT_K1, LANES, 2 * DFT_N2), lambda c, bi, o: (o, 0, c, 0)),
            pl.BlockSpec((None, 1, LANES), lambda c, bi, o: (o, 0, c)),
            _resident((2 * DFT_QP, half)),
            _resident((DFT_K1, 2 * DFT_N2, 2 * DFT_N2)),
            _resident((DFT_K1, 2 * DFT_N2, 2 * DFT_N2)),
            _resident((half, 2 * DFT_QP)),
        ],
        out_specs=pl.BlockSpec((None, seq_len, LANES), lambda c, bi, o: (bi, 0, c)),
        scratch_shapes=[pltpu.VMEM((half * DFT_PITCH, LANES), F32),
                        pltpu.VMEM((half * DFT_PITCH, LANES), F32),
                        pltpu.VMEM((2 * DFT_QP * DFT_PITCH, LANES), F32)],
        compiler_params=_cparams(("parallel", "parallel", "arbitrary")),
        name="hyena_mixer",
    )(proj, proj, conv_w, conv_b, conv_w, conv_b, spec, bias.reshape(HYENA_ORDER, 1, width),
      jnp.asarray(ff).astype(BF16), jnp.asarray(g).astype(BF16), jnp.asarray(gt).astype(BF16),
      jnp.asarray(fi).astype(BF16))


def _hdot(a, b):
    return jnp.dot(a, b, preferred_element_type=F32, precision=lax.Precision.HIGHEST)


def _sin_half_lanes(arg):
    rows = arg.shape[0] // 2
    lane = lax.broadcasted_iota(jnp.int32, (rows, LANES), 1)
    packed = jnp.where(lane < LANES // 2, arg[:rows], pltpu.roll(arg[rows:], LANES // 2, axis=1))
    s = jnp.sin(packed)
    return jnp.concatenate([s, pltpu.roll(s, LANES // 2, axis=1)], axis=0)


def _filter_taps_body(pack_sin, bwd_cols, z_ref, t_ref, w1_ref, b1_ref, w2_ref, b2_ref, w3_ref,
                      b3_ref, sf_ref, wo_ref, decay_ref, taps_ref, ss_ref):
    i = pl.program_id(0)
    sin = _sin_half_lanes if pack_sin else jnp.sin
    h = sin(sf_ref[0:1, :] * (_hdot(z_ref[...], w1_ref[...]) + b1_ref[...]))
    h = sin(sf_ref[1:2, :] * (_hdot(h, w2_ref[...]) + b2_ref[...]))
    h = sin(sf_ref[2:3, :] * (_hdot(h, w3_ref[...]) + b3_ref[...]))
    h = _dot(h.astype(BF16), wo_ref[...])
    h = h * jnp.exp(-t_ref[...] * jnp.abs(decay_ref[...]))
    row = i * h.shape[0] + lax.broadcasted_iota(jnp.int32, h.shape, 0)
    h = jnp.where((row == 0) & (bwd_cols[...] > 0.0), 0.0, h)
    taps_ref[...] = h

    @pl.when(i == 0)
    def _():
        ss_ref[...] = jnp.zeros_like(ss_ref)

    ss_ref[...] += jnp.sum(h * h, axis=0, keepdims=True)


def _filter_taps(z, t, w1, b1, w2, b2, w3, b3, sin_freq, w_out, decay, *, tl=512):
    seq_len = z.shape[0]
    hid = LANES
    n_out = w_out.shape[1]
    width = n_out // (2 * HYENA_ORDER)

    def pad2(a, rows, cols):
        return jnp.pad(a, ((0, rows - a.shape[0]), (0, cols - a.shape[1])))

    is_bwd = np.tile(np.repeat(np.array([0.0, 1.0], np.float32), width), HYENA_ORDER)[None, :]
    full = lambda a: pl.BlockSpec(a.shape, lambda i: (0, 0))
    args = (
        pad2(z, seq_len, hid), t.reshape(seq_len, 1),
        pad2(w1, hid, hid), pad2(b1[None, :], 1, hid),
        pad2(w2, hid, hid), pad2(b2[None, :], 1, hid),
        pad2(w3, hid, hid), pad2(b3[None, :], 1, hid),
        pad2(sin_freq, 3, hid), pad2(w_out, hid, n_out).astype(BF16), decay.reshape(1, n_out),
    )
    bwd_cols = jnp.asarray(is_bwd)
    return pl.pallas_call(
        functools.partial(_filter_taps_body, w1.shape[1] <= LANES // 2 and tl % 16 == 0),
        out_shape=(jax.ShapeDtypeStruct((seq_len, n_out), F32),
                   jax.ShapeDtypeStruct((1, n_out), F32)),
        grid=(seq_len // tl,),
        in_specs=[full(bwd_cols),
                  pl.BlockSpec((tl, hid), lambda i: (i, 0)),
                  pl.BlockSpec((tl, 1), lambda i: (i, 0))] + [full(a) for a in args[2:]],
        out_specs=(pl.BlockSpec((tl, n_out), lambda i: (i, 0)),
                   pl.BlockSpec((1, n_out), lambda i: (0, 0))),
        compiler_params=_cparams(("arbitrary",)),
        name="filter_taps",
    )(bwd_cols, *args)


def _rope_slab(x, cos, sin):
    half = MLA_ROPE // 2
    lane = lax.broadcasted_iota(jnp.int32, x.shape, 1)
    swapped = jnp.where(lane < MLA_NOPE + half, pltpu.roll(x, LANES - half, axis=1),
                        pltpu.roll(x, half, axis=1))
    return x * cos + swapped * sin


def _mla_qkv_body(c_ref, qg_ref, kvg_ref, wq_ref, wk_ref, wv_ref, cos_ref, sin_ref,
                  q_ref, k_ref, v_ref):
    c = c_ref[...]
    cos = cos_ref[...]
    sin = sin_ref[...]
    cq = _rms(c[:, :MLA_Q_LORA], qg_ref[...]).astype(BF16)
    ckv = _rms(c[:, MLA_Q_LORA:MLA_Q_LORA + MLA_KV_LORA], kvg_ref[...]).astype(BF16)
    kr0 = MLA_Q_LORA + MLA_KV_LORA
    k_rope = _rope_slab(c[:, kr0:kr0 + LANES], cos, sin)
    cos_q = cos * SOFTMAX_EXP2_SCALE
    sin_q = sin * SOFTMAX_EXP2_SCALE
    ones_lane = (lax.broadcasted_iota(jnp.int32, (1, LANES), 1) == MLA_V).astype(F32)
    for h0 in range(0, MLA_HEADS, 2):
        cols = slice(h0 * LANES, (h0 + 2) * LANES)
        q2 = _dot(cq, wq_ref[:, cols])
        k2 = _dot(ckv, wk_ref[:, cols])
        v2 = _dot(ckv, wv_ref[:, cols])
        for j in range(2):
            lanes = slice(j * LANES, (j + 1) * LANES)
            q_ref[h0 + j] = _rope_slab(q2[:, lanes], cos_q, sin_q).astype(BF16)
            k_ref[h0 + j] = (k2[:, lanes] + k_rope).astype(BF16)
            v_ref[h0 + j] = (v2[:, lanes] + ones_lane).astype(BF16)


def _mla_qkv(c, q_norm_g, kv_norm_g, wq, wk, wv, cos_t, sin_t, batch, *, tl=512):
    m, cw = c.shape
    seq_len = m // batch
    nl = seq_len // tl
    full = lambda a: pl.BlockSpec(a.shape, lambda bi, i: (0, 0))
    qg = q_norm_g.reshape(1, -1)
    kvg = kv_norm_g.reshape(1, -1)
    head_out = jax.ShapeDtypeStruct((batch, MLA_HEADS, seq_len, LANES), BF16)
    head_spec = pl.BlockSpec((None, MLA_HEADS, tl, LANES), lambda bi, i: (bi, 0, i, 0))
    return pl.pallas_call(
        _mla_qkv_body,
        out_shape=(head_out, head_out, head_out),
        grid=(batch, nl),
        in_specs=[pl.BlockSpec((tl, cw), lambda bi, i: (bi * nl + i, 0)),
                  full(qg), full(kvg), full(wq), full(wk), full(wv),
                  pl.BlockSpec((tl, LANES), lambda bi, i: (i, 0)),
                  pl.BlockSpec((tl, LANES), lambda bi, i: (i, 0))],
        out_specs=(head_spec, head_spec, head_spec),
        compiler_params=_cparams(("parallel", "parallel")),
        name="mla_qkv",
    )(c, qg, kvg, wq, wk, wv, cos_t, sin_t)


def _attention_body(q_ref, qn_ref, k_ref, v_ref, o_ref, *s_refs):
    pair = LANES // MLA_V
    n_heads = q_ref.shape[0]

    def scores(q, h):
        return lax.dot_general(q, k_ref[h], (((1,), (1,)), ((), ())), preferred_element_type=F32)

    @pl.when(pl.program_id(2) == 0)
    def _():
        for h, s_ref in enumerate(s_refs):
            s_ref[...] = scores(q_ref[h], h)

    ahead = {h: s_ref[...] for h, s_ref in enumerate(s_refs)}
    for g in range(n_heads // pair):
        outs = []
        for h in range(g * pair, (g + 1) * pair):
            lead = h + SCORE_LEAD
            if lead < n_heads:
                ahead[lead] = scores(q_ref[lead], lead)
            else:
                s_refs[lead - n_heads][...] = scores(qn_ref[lead - n_heads], lead - n_heads)
            s = ahead.pop(h)
            p = jnp.exp2(s - jnp.max(s, axis=-1, keepdims=True))
            r = _dot(p.astype(BF16), v_ref[h])
            outs.append(r / r[:, MLA_V:MLA_V + 1])
        lane = lax.broadcasted_iota(jnp.int32, outs[0].shape, 1)
        both = jnp.where(lane < MLA_V, outs[0], pltpu.roll(outs[1], MLA_V, axis=1))
        o_ref[:, g * LANES:(g + 1) * LANES] = both.astype(o_ref.dtype)


def _attention(q, k, v, *, tq=256, group=8):
    batch, heads, seq_len, _ = q.shape
    width = group * MLA_V
    n_tiles = seq_len // tq
    return pl.pallas_call(
        _attention_body,
        out_shape=jax.ShapeDtypeStruct((batch, seq_len, heads * MLA_V), BF16),
        grid=(batch, heads // group, n_tiles),
        in_specs=[
            pl.BlockSpec((None, group, tq, LANES), lambda bi, p, i: (bi, p, i, 0)),
            pl.BlockSpec((None, SCORE_LEAD, tq, LANES),
                         lambda bi, p, i: (bi, p * (group // SCORE_LEAD),
                                           jnp.minimum(i + 1, n_tiles - 1), 0)),
            pl.BlockSpec((None, group, seq_len, LANES), lambda bi, p, i: (bi, p, 0, 0)),
            pl.BlockSpec((None, group, seq_len, LANES), lambda bi, p, i: (bi, p, 0, 0)),
        ],
        out_specs=pl.BlockSpec((None, tq, width), lambda bi, p, i: (bi, i, p)),
        scratch_shapes=[pltpu.VMEM((tq, seq_len), F32)] * SCORE_LEAD,
        compiler_params=_cparams(("parallel", "parallel", "arbitrary")),
        name="attention",
    )(q, q, k, v)


def _mla_weights(w_dq, w_uq, w_dkv, w_ukv):
    d = w_dq.shape[0]
    qh = w_uq.reshape(MLA_Q_LORA, MLA_HEADS, MLA_NOPE + MLA_ROPE)
    zq = jnp.zeros((MLA_Q_LORA, MLA_HEADS, LANES - MLA_NOPE - MLA_ROPE), F32)
    wq = jnp.concatenate([qh, zq], axis=-1).reshape(MLA_Q_LORA, MLA_HEADS * LANES)
    kvh = w_ukv.reshape(MLA_KV_LORA, MLA_HEADS, MLA_NOPE + MLA_V)
    wk = jnp.concatenate([kvh[..., :MLA_NOPE],
                          jnp.zeros((MLA_KV_LORA, MLA_HEADS, LANES - MLA_NOPE), F32)],
                         axis=-1).reshape(MLA_KV_LORA, MLA_HEADS * LANES)
    wv = jnp.concatenate([kvh[..., MLA_NOPE:],
                          jnp.zeros((MLA_KV_LORA, MLA_HEADS, LANES - MLA_V), F32)],
                         axis=-1).reshape(MLA_KV_LORA, MLA_HEADS * LANES)
    w_kr = jnp.concatenate([jnp.zeros((d, MLA_NOPE), F32), w_dkv[:, MLA_KV_LORA:],
                            jnp.zeros((d, LANES - MLA_NOPE - MLA_ROPE), F32)], axis=-1)
    w_down = jnp.concatenate([w_dq, w_dkv[:, :MLA_KV_LORA], w_kr], axis=-1)
    return w_down.astype(BF16), wq.astype(BF16), wk.astype(BF16), wv.astype(BF16)


def _rope_tables(seq_len):
    inv_freq = ROPE_THETA ** (-jnp.arange(0, MLA_ROPE, 2, dtype=F32) / MLA_ROPE)
    ang = jnp.arange(seq_len, dtype=F32)[:, None] * inv_freq[None, :]
    cos = jnp.cos(ang)
    sin = jnp.sin(ang)
    tail = LANES - MLA_NOPE - MLA_ROPE
    cos_t = jnp.concatenate([jnp.ones((seq_len, MLA_NOPE), F32), cos, cos,
                             jnp.ones((seq_len, tail), F32)], axis=-1)
    sin_t = jnp.concatenate([jnp.zeros((seq_len, MLA_NOPE), F32), -sin, sin,
                             jnp.zeros((seq_len, tail), F32)], axis=-1)
    return cos_t, sin_t


def _position_features(seq_len, pos_dim):
    t = jnp.linspace(0.0, 1.0, seq_len, dtype=F32)
    bands = (pos_dim - 1) // 2
    w = 2.0 * math.pi * jnp.arange(seq_len, dtype=F32) / seq_len
    f = jnp.linspace(1e-4, bands - 1, bands, dtype=F32)
    phase = w[:, None] * f[None, :]
    z = jnp.concatenate([t[:, None], jnp.cos(phase), -jnp.sin(phase)], axis=-1)
    return t, z


def kernel(x, norm_g, ffn_w_gate, ffn_w_up, ffn_w_down, mix_w_in, pool_w, pool_scale, hyena_conv_w, hyena_conv_b, hyena_ffn_w1, hyena_ffn_b1, hyena_ffn_w2, hyena_ffn_b2, hyena_ffn_w3, hyena_ffn_b3, hyena_sin_freq, hyena_ffn_w_out, hyena_decay, hyena_bias, mix_w_out, mla_w_dq, mla_q_norm_g, mla_w_uq, mla_w_dkv, mla_kv_norm_g, mla_w_ukv, mla_w_o, final_norm_g):
    batch, seq_len, d = x.shape
    depth = norm_g.shape[0]
    assert 2 * seq_len == DFT_N1 * DFT_N2
    hy_width = hyena_bias.shape[-1]
    t_pos, z_pos = _position_features(seq_len, hyena_ffn_w1.shape[1])
    cos_t, sin_t = _rope_tables(seq_len)

    x = x.reshape(batch * seq_len, d)
    ffn_w = (ffn_w_gate, ffn_w_up, ffn_w_down)
    for i in range(depth):
        j = i // 2
        if i % 2 == 0:
            x, proj = _ffn(x, norm_g[i, 0], ffn_w, (i, 0), next_g=norm_g[i, 1],
                           w_next=mix_w_in[j].astype(BF16))
            proj = proj.reshape(batch, seq_len, -1)
            y_pool = _pool_mixer(proj, pool_w[j].astype(BF16), pool_scale[j])
            taps, sumsq = _filter_taps(z_pos, t_pos, hyena_ffn_w1[j], hyena_ffn_b1[j],
                                       hyena_ffn_w2[j], hyena_ffn_b2[j], hyena_ffn_w3[j],
                                       hyena_ffn_b3[j], hyena_sin_freq[j], hyena_ffn_w_out[j],
                                       hyena_decay[j])
            spec = _filter_spectrum(taps, sumsq, hy_width)
            y_hyena = _hyena_mixer(proj, pool_scale.shape[-1], hyena_conv_w[j], hyena_conv_b[j],
                                   spec, hyena_bias[j])
            mix = (y_pool.reshape(batch * seq_len, -1), y_hyena.reshape(batch * seq_len, -1))
            w_mix = mix_w_out[j].astype(BF16)
        else:
            w_down, wq, wk, wv = _mla_weights(mla_w_dq[j], mla_w_uq[j], mla_w_dkv[j],
                                              mla_w_ukv[j])
            x, c = _ffn(x, norm_g[i, 0], ffn_w, (i, 0), next_g=norm_g[i, 1], w_next=w_down)
            q, k, v = _mla_qkv(c, mla_q_norm_g[j], mla_kv_norm_g[j], wq, wk, wv,
                               cos_t, sin_t, batch)
            mix = (_attention(q, k, v).reshape(batch * seq_len, -1),)
            w_mix = mla_w_o[j].astype(BF16)
        x = _ffn(x, norm_g[i, 2], ffn_w, (i, 1), mix=mix, w_mix=w_mix,
                 final_g=final_norm_g if i == depth - 1 else None)
    return x.reshape(batch, seq_len, d)
```

```python
import functools
import math

import numpy as np
import jax
import jax.numpy as jnp
from jax import lax
from jax.experimental import pallas as pl
from jax.experimental.pallas import tpu as pltpu

F32 = jnp.float32
BF16 = jnp.bfloat16

RMS_EPS = 1e-6
MACARON_WEIGHT = 0.5
POOL_WINDOWS = (2, 4, 8, 16)
HYENA_ORDER = 2
HYENA_SHORT_CONV = 3
MLA_HEADS = 16
MLA_Q_LORA = 256
MLA_KV_LORA = 128
MLA_NOPE = 64
MLA_ROPE = 32
MLA_V = 64
ROPE_THETA = 10000.0
SOFTMAX_EXP2_SCALE = (MLA_NOPE + MLA_ROPE) ** -0.5 * math.log2(math.e)

LANES = 128
SUBLANES = 8
VMEM_LIMIT = 56 * 2**20

DFT_N1 = 64
DFT_N2 = 128
DFT_K1 = DFT_N1 // 2 + 1
DFT_QP = 40
DFT_PITCH = DFT_N2 + SUBLANES
PLANE_ROWS_PER_DOT = 8
PLANE_UNROLL = 16
K1_UNROLL = 33
HALO = 16
FFN_LOAD_STEPS = 8
SCORE_LEAD = 3
SCORE_CARRY = 1


def _cparams(semantics):
    return pltpu.CompilerParams(dimension_semantics=semantics, vmem_limit_bytes=VMEM_LIMIT)


def _rms(x, g):
    return x * lax.rsqrt(jnp.mean(x * x, axis=-1, keepdims=True) + RMS_EPS) * g


def _dot(a, b):
    return jnp.dot(a, b, preferred_element_type=F32)


def _ffn_body(n_mix, has_next, has_final, *refs):
    refs = list(refs)
    x_ref, g_ref, wg_ref, wu_ref, wd_ref = refs[:5]
    del refs[:5]
    mix_refs = [refs.pop(0) for _ in range(n_mix)]
    w_mix_ref = refs.pop(0) if n_mix else None
    next_g_ref, next_w_ref = (refs.pop(0), refs.pop(0)) if has_next else (None, None)
    final_g_ref = refs.pop(0) if has_final else None
    o_ref = refs.pop(0)
    next_o_ref = refs.pop(0) if has_next else None
    wg_bf, wu_bf, wd_bf = refs
    step = pl.program_id(0)

    @pl.when(step < FFN_LOAD_STEPS)
    def _():
        for src_ref, dst_ref in ((wg_ref, wg_bf), (wu_ref, wu_bf), (wd_ref, wd_bf)):
            rows = src_ref.shape[0]
            dst_ref[pl.ds(pl.multiple_of(step * rows, rows), rows), :] = src_ref[...].astype(BF16)

    @pl.when(step >= FFN_LOAD_STEPS)
    def _():
        x = x_ref[...]
        row = 0
        for a_ref in mix_refs:
            k = a_ref.shape[1]
            x = x + _dot(a_ref[...].astype(BF16), w_mix_ref[row:row + k, :])
            row += k
        h = _rms(x, g_ref[...]).astype(BF16)
        gate = _dot(h, wg_bf[...])
        up = _dot(h, wu_bf[...])
        act = gate / (1.0 + jnp.exp(-gate)) * up
        y = x + MACARON_WEIGHT * _dot(act.astype(BF16), wd_bf[...])
        if has_final:
            y = _rms(y, final_g_ref[...])
        o_ref[...] = y
        if has_next:
            next_o_ref[...] = _dot(_rms(y, next_g_ref[...]).astype(BF16), next_w_ref[...])


def _resident(shape):
    return pl.BlockSpec(shape, lambda *_: (0,) * len(shape), pipeline_mode=pl.Buffered(1))


def _streamed_rows(stack, index):
    lead = len(index)
    rows, cols = stack.shape[lead:]
    return pl.BlockSpec(
        (None,) * lead + (rows // FFN_LOAD_STEPS, cols),
        lambda i: tuple(index) + (jnp.minimum(i, FFN_LOAD_STEPS - 1), 0))


def _ffn(x, g, w_stacks, index, *, mix=(), w_mix=None, next_g=None, w_next=None, final_g=None,
         tm=512):
    m, d = x.shape
    tile = lambda i: jnp.maximum(i - FFN_LOAD_STEPS, 0)
    row_block = lambda a: pl.BlockSpec((tm, a.shape[1]), lambda i: (tile(i), 0))
    args = [x, g.reshape(1, d), *w_stacks]
    in_specs = [row_block(x), _resident((1, d))] + [_streamed_rows(w, index) for w in w_stacks]
    for a in mix:
        args.append(a)
        in_specs.append(row_block(a))
    if mix:
        args.append(w_mix)
        in_specs.append(_resident(w_mix.shape))
    if w_next is not None:
        args += [next_g.reshape(1, d), w_next]
        in_specs += [_resident((1, d)), _resident(w_next.shape)]
    if final_g is not None:
        args.append(final_g.reshape(1, d))
        in_specs.append(_resident((1, d)))
    out_shape = [jax.ShapeDtypeStruct((m, d), F32)]
    out_specs = [pl.BlockSpec((tm, d), lambda i: (tile(i), 0))]
    if w_next is not None:
        out_shape.append(jax.ShapeDtypeStruct((m, w_next.shape[1]), F32))
        out_specs.append(pl.BlockSpec((tm, w_next.shape[1]), lambda i: (tile(i), 0)))
    lead = len(index)
    outs = pl.pallas_call(
        functools.partial(_ffn_body, len(mix), w_next is not None, final_g is not None),
        out_shape=out_shape,
        grid=(FFN_LOAD_STEPS + m // tm,),
        in_specs=in_specs,
        out_specs=out_specs,
        scratch_shapes=[pltpu.VMEM(w.shape[lead:], BF16) for w in w_stacks],
        compiler_params=_cparams(("arbitrary",)),
        name="ffn",
    )(*args)
    return outs if w_next is not None else outs[0]


def _pool_body(seq_len, cur_ref, prev_ref, next_ref, pw_ref, ps_ref, yp_ref, ext_ref):
    i = pl.program_id(1)
    tl = cur_ref.shape[0]
    pool_width = yp_ref.shape[1]
    group = pool_width // len(POOL_WINDOWS)
    ext_ref[0:HALO, :] = jnp.where(i > 0, prev_ref[...], 0.0)
    ext_ref[HALO:HALO + tl, :] = cur_ref[...]
    ext_ref[HALO + tl:, :] = jnp.where(i < pl.num_programs(1) - 1, next_ref[...], 0.0)

    t = i * tl + lax.broadcasted_iota(jnp.int32, (tl, group), 0)
    for g, w in enumerate(POOL_WINDOWS):
        c0 = g * group
        lo = jnp.clip(t - w // 2, 0, seq_len)
        hi = jnp.clip(t - w // 2 + w, 0, seq_len)
        cnt = (hi - lo).astype(F32)
        s = ext_ref[HALO - w // 2:HALO - w // 2 + tl, c0:c0 + group]
        for d in range(1 - w // 2, w - w // 2):
            s = s + ext_ref[HALO + d:HALO + d + tl, c0:c0 + group]
        p = s / cnt - cur_ref[:, c0:c0 + group]
        y = _dot(p.astype(BF16), pw_ref[g])
        yp_ref[:, c0:c0 + group] = y * ps_ref[:, c0:c0 + group]


def _pool_mixer(proj, pool_w, pool_scale, *, tl=512):
    b, seq_len, _ = proj.shape
    pool_width = pool_scale.shape[0]
    hb = tl // HALO
    last = seq_len // HALO - 1
    return pl.pallas_call(
        functools.partial(_pool_body, seq_len),
        out_shape=jax.ShapeDtypeStruct((b, seq_len, pool_width), F32),
        grid=(b, seq_len // tl),
        in_specs=[
            pl.BlockSpec((None, tl, pool_width), lambda bi, i: (bi, i, 0)),
            pl.BlockSpec((None, HALO, pool_width),
                         lambda bi, i: (bi, jnp.maximum(i * hb - 1, 0), 0)),
            pl.BlockSpec((None, HALO, pool_width),
                         lambda bi, i: (bi, jnp.minimum((i + 1) * hb, last), 0)),
            pl.BlockSpec(pool_w.shape, lambda bi, i: (0, 0, 0)),
            pl.BlockSpec((1, pool_width), lambda bi, i: (0, 0)),
        ],
        out_specs=pl.BlockSpec((None, tl, pool_width), lambda bi, i: (bi, i, 0)),
        scratch_shapes=[pltpu.VMEM((tl + 2 * HALO, pool_width), F32)],
        compiler_params=_cparams(("parallel", "parallel")),
        name="pool_mixer",
    )(proj, proj, proj, pool_w, pool_scale.reshape(1, pool_width))


@functools.lru_cache(maxsize=None)
def _dft_constants():
    n = DFT_N1 * DFT_N2
    half = DFT_N1 // 2
    k1 = np.arange(DFT_K1)
    n1 = np.arange(half)
    ang = 2.0 * np.pi * np.outer(k1, n1) / DFT_N1
    f_fwd = np.zeros((2 * DFT_QP, half))
    f_fwd[:DFT_K1] = np.cos(ang)
    f_fwd[DFT_QP:DFT_QP + DFT_K1] = -np.sin(ang)
    n2 = np.arange(DFT_N2)
    k2 = np.arange(DFT_N2)
    g = np.zeros((DFT_K1, 2 * DFT_N2, 2 * DFT_N2))
    for a in range(DFT_K1):
        ph = 2.0 * np.pi * np.outer(a + DFT_N1 * k2, n2) / n
        gre, gim = np.cos(ph), -np.sin(ph)
        g[a] = np.block([[gre, -gim], [gim, gre]])
    weight = np.full(DFT_K1, 2.0)
    weight[0] = 1.0
    weight[-1] = 1.0
    f_inv = np.zeros((half, 2 * DFT_QP))
    f_inv[:, :DFT_K1] = weight * np.cos(ang.T) / n
    f_inv[:, DFT_QP:DFT_QP + DFT_K1] = -weight * np.sin(ang.T) / n
    return (f_fwd.astype(np.float32), g.astype(np.float32),
            np.transpose(g, (0, 2, 1)).astype(np.float32), f_inv.astype(np.float32))


def _pad_rows(dst_ref, src):
    for n1 in range(DFT_N1 // 2):
        dst_ref[n1 * DFT_PITCH:n1 * DFT_PITCH + DFT_N2, :] = src[n1 * DFT_N2:(n1 + 1) * DFT_N2, :]


def _contract_planes(src_ref, dst_ref, f_ref):
    n_dst, n_src = f_ref.shape

    def step(j, carry):
        r0 = j * PLANE_ROWS_PER_DOT
        x = jnp.concatenate(
            [src_ref[pl.ds(r0 + r, n_src, stride=DFT_PITCH), :].astype(BF16)
             for r in range(PLANE_ROWS_PER_DOT)], axis=1)
        y = _dot(f_ref[...], x)
        for r in range(PLANE_ROWS_PER_DOT):
            dst_ref[pl.ds(r0 + r, n_dst, stride=DFT_PITCH), :] = y[:, r * LANES:(r + 1) * LANES]
        return carry

    lax.fori_loop(0, DFT_N2 // PLANE_ROWS_PER_DOT, step, 0, unroll=PLANE_UNROLL)


def _load_planes(a_ref, k1):
    re = a_ref[pl.ds(pl.multiple_of(k1 * DFT_PITCH, SUBLANES), DFT_N2), :]
    im = a_ref[pl.ds(pl.multiple_of((DFT_QP + k1) * DFT_PITCH, SUBLANES), DFT_N2), :]
    return jnp.concatenate([re, im], axis=0)


def _store_planes(a_ref, k1, z):
    a_ref[pl.ds(pl.multiple_of(k1 * DFT_PITCH, SUBLANES), DFT_N2), :] = z[:DFT_N2]
    a_ref[pl.ds(pl.multiple_of((DFT_QP + k1) * DFT_PITCH, SUBLANES), DFT_N2), :] = z[DFT_N2:]


def _spectrum_body(fw_ref, bw_ref, ssf_ref, ssb_ref, ff_ref, gt_ref, h_ref, u_ref, a_ref):
    norm = lax.rsqrt(ssf_ref[...] + ssb_ref[...]).T
    for part, src_ref in enumerate((fw_ref, bw_ref)):
        _pad_rows(u_ref, src_ref)
        _contract_planes(u_ref, a_ref, ff_ref)
        sign = 1.0 if part == 0 else -1.0

        def step(k1, carry):
            xt = _dot(_load_planes(a_ref, k1).T.astype(BF16), gt_ref[k1])
            re = xt[:, :DFT_N2] * norm
            im = xt[:, DFT_N2:] * (sign * norm)
            if part == 0:
                h_ref[k1, :, :DFT_N2] = re
                h_ref[k1, :, DFT_N2:] = im
            else:
                h_ref[k1, :, :DFT_N2] += re
                h_ref[k1, :, DFT_N2:] += im
            return carry

        lax.fori_loop(0, DFT_K1, step, 0, unroll=K1_UNROLL)


def _filter_spectrum(taps, sumsq, width):
    seq_len = taps.shape[0]
    ff, _, gt, _ = _dft_constants()
    cb = width // LANES
    half = DFT_N1 // 2
    return pl.pallas_call(
        _spectrum_body,
        out_shape=jax.ShapeDtypeStruct((HYENA_ORDER, DFT_K1, width, 2 * DFT_N2), F32),
        grid=(HYENA_ORDER, cb),
        in_specs=[
            pl.BlockSpec((seq_len, LANES), lambda o, c: (0, (2 * o) * cb + c)),
            pl.BlockSpec((seq_len, LANES), lambda o, c: (0, (2 * o + 1) * cb + c)),
            pl.BlockSpec((1, LANES), lambda o, c: (0, (2 * o) * cb + c)),
            pl.BlockSpec((1, LANES), lambda o, c: (0, (2 * o + 1) * cb + c)),
            pl.BlockSpec((2 * DFT_QP, half), lambda o, c: (0, 0)),
            pl.BlockSpec((DFT_K1, 2 * DFT_N2, 2 * DFT_N2), lambda o, c: (0, 0, 0)),
        ],
        out_specs=pl.BlockSpec((None, DFT_K1, LANES, 2 * DFT_N2), lambda o, c: (o, 0, c, 0)),
        scratch_shapes=[pltpu.VMEM((half * DFT_PITCH, LANES), F32),
                        pltpu.VMEM((2 * DFT_QP * DFT_PITCH, LANES), F32)],
        compiler_params=_cparams(("parallel", "parallel")),
        name="filter_spectrum",
    )(taps, taps, sumsq, sumsq, jnp.asarray(ff).astype(BF16), jnp.asarray(gt).astype(BF16))


def _short_conv_rows(x_ref, n1, w_ref, b_ref):
    r0 = n1 * DFT_N2
    cur = x_ref[r0:r0 + DFT_N2, :]
    row = lax.broadcasted_iota(jnp.int32, cur.shape, 0)
    if r0 == 0:
        prev = jnp.where(row == 0, 0.0, pltpu.roll(cur, 1, axis=0))
    else:
        prev = x_ref[r0 - 1:r0 + DFT_N2 - 1, :]
    if r0 + DFT_N2 == x_ref.shape[0]:
        nxt = jnp.where(row == DFT_N2 - 1, 0.0, pltpu.roll(cur, DFT_N2 - 1, axis=0))
    else:
        nxt = x_ref[r0 + 1:r0 + DFT_N2 + 1, :]
    return b_ref[...] + prev * w_ref[0:1, :] + cur * w_ref[1:2, :] + nxt * w_ref[2:3, :]


def _hyena_body(v_ref, gate_ref, vw_ref, vb_ref, gw_ref, gb_ref, h_ref, bias_ref, ff_ref, g_ref,
                gt_ref, fi_ref, o_ref, u_ref, y_ref, a_ref):
    order = pl.program_id(2)
    half = DFT_N1 // 2

    @pl.when(order == 0)
    def _():
        for n1 in range(half):
            u_ref[n1 * DFT_PITCH:n1 * DFT_PITCH + DFT_N2, :] = _short_conv_rows(
                v_ref, n1, vw_ref, vb_ref)

    _contract_planes(u_ref, a_ref, ff_ref)

    def freq_step(k1, carry):
        xt = _dot(_load_planes(a_ref, k1).T.astype(BF16), gt_ref[k1])
        xre, xim = xt[:, :DFT_N2], xt[:, DFT_N2:]
        hre, him = h_ref[k1, :, :DFT_N2], h_ref[k1, :, DFT_N2:]
        yt = jnp.concatenate([xre * hre - xim * him, xre * him + xim * hre], axis=1)
        _store_planes(a_ref, k1, _dot(yt.astype(BF16), g_ref[k1]).T)
        return carry

    lax.fori_loop(0, DFT_K1, freq_step, 0, unroll=K1_UNROLL)
    _contract_planes(a_ref, y_ref, fi_ref)

    bias = bias_ref[...]
    for n1 in range(half):
        rows = slice(n1 * DFT_PITCH, n1 * DFT_PITCH + DFT_N2)
        gate = _short_conv_rows(gate_ref, n1, gw_ref, gb_ref)
        z = gate * (y_ref[rows, :] + u_ref[rows, :] * bias)
        u_ref[rows, :] = z
        o_ref[n1 * DFT_N2:(n1 + 1) * DFT_N2, :] = z


def _hyena_mixer(proj, first_col, conv_w, conv_b, spec, bias):
    assert HYENA_SHORT_CONV == 3
    b, seq_len, _ = proj.shape
    width = bias.shape[1]
    cb = width // LANES
    c0 = first_col // LANES
    half = DFT_N1 // 2
    ff, g, gt, fi = _dft_constants()
    value_col = lambda c, bi, o: HYENA_ORDER * cb + c
    gate_col = lambda c, bi, o: o * cb + c
    conv_b = conv_b.reshape(1, -1)
    return pl.pallas_call(
        _hyena_body,
        out_shape=jax.ShapeDtypeStruct((b, seq_len, width), F32),
        grid=(cb, b, HYENA_ORDER),
        in_specs=[
            pl.BlockSpec((None, seq_len, LANES), lambda c, bi, o: (bi, 0, c0 + value_col(c, bi, o))),
            pl.BlockSpec((None, seq_len, LANES), lambda c, bi, o: (bi, 0, c0 + gate_col(c, bi, o))),
            pl.BlockSpec((HYENA_SHORT_CONV, LANES), lambda c, bi, o: (0, value_col(c, bi, o))),
            pl.BlockSpec((1, LANES), lambda c, bi, o: (0, value_col(c, bi, o))),
            pl.BlockSpec((HYENA_SHORT_CONV, LANES), lambda c, bi, o: (0, gate_col(c, bi, o))),
            pl.BlockSpec((1, LANES), lambda c, bi, o: (0, gate_col(c, bi, o))),
            pl.BlockSpec((None, DFT_K1, LANES, 2 * DFT_N2), lambda c, bi, o: (o, 0, c, 0)),
            pl.BlockSpec((None, 1, LANES), lambda c, bi, o: (o, 0, c)),
            _resident((2 * DFT_QP, half)),
            _resident((DFT_K1, 2 * DFT_N2, 2 * DFT_N2)),
            _resident((DFT_K1, 2 * DFT_N2, 2 * DFT_N2)),
            _resident((half, 2 * DFT_QP)),
        ],
        out_specs=pl.BlockSpec((None, seq_len, LANES), lambda c, bi, o: (bi, 0, c)),
        scratch_shapes=[pltpu.VMEM((half * DFT_PITCH, LANES), F32),
                        pltpu.VMEM((half * DFT_PITCH, LANES), F32),
                        pltpu.VMEM((2 * DFT_QP * DFT_PITCH, LANES), F32)],
        compiler_params=_cparams(("parallel", "parallel", "arbitrary")),
        name="hyena_mixer",
    )(proj, proj, conv_w, conv_b, conv_w, conv_b, spec, bias.reshape(HYENA_ORDER, 1, width),
      jnp.asarray(ff).astype(BF16), jnp.asarray(g).astype(BF16), jnp.asarray(gt).astype(BF16),
      jnp.asarray(fi).astype(BF16))


def _hdot(a, b):
    return jnp.dot(a, b, preferred_element_type=F32, precision=lax.Precision.HIGHEST)


def _sin_half_lanes(arg):
    rows = arg.shape[0] // 2
    lane = lax.broadcasted_iota(jnp.int32, (rows, LANES), 1)
    packed = jnp.where(lane < LANES // 2, arg[:rows], pltpu.roll(arg[rows:], LANES // 2, axis=1))
    s = jnp.sin(packed)
    return jnp.concatenate([s, pltpu.roll(s, LANES // 2, axis=1)], axis=0)


def _filter_taps_body(pack_sin, bwd_cols, z_ref, t_ref, w1_ref, b1_ref, w2_ref, b2_ref, w3_ref,
                      b3_ref, sf_ref, wo_ref, decay_ref, taps_ref, ss_ref):
    i = pl.program_id(0)
    sin = _sin_half_lanes if pack_sin else jnp.sin
    h = sin(sf_ref[0:1, :] * (_hdot(z_ref[...], w1_ref[...]) + b1_ref[...]))
    h = sin(sf_ref[1:2, :] * (_hdot(h, w2_ref[...]) + b2_ref[...]))
    h = sin(sf_ref[2:3, :] * (_hdot(h, w3_ref[...]) + b3_ref[...]))
    h = _dot(h.astype(BF16), wo_ref[...])
    h = h * jnp.exp(-t_ref[...] * jnp.abs(decay_ref[...]))
    row = i * h.shape[0] + lax.broadcasted_iota(jnp.int32, h.shape, 0)
    h = jnp.where((row == 0) & (bwd_cols[...] > 0.0), 0.0, h)
    taps_ref[...] = h

    @pl.when(i == 0)
    def _():
        ss_ref[...] = jnp.zeros_like(ss_ref)

    ss_ref[...] += jnp.sum(h * h, axis=0, keepdims=True)


def _filter_taps(z, t, w1, b1, w2, b2, w3, b3, sin_freq, w_out, decay, *, tl=512):
    seq_len = z.shape[0]
    hid = LANES
    n_out = w_out.shape[1]
    width = n_out // (2 * HYENA_ORDER)

    def pad2(a, rows, cols):
        return jnp.pad(a, ((0, rows - a.shape[0]), (0, cols - a.shape[1])))

    is_bwd = np.tile(np.repeat(np.array([0.0, 1.0], np.float32), width), HYENA_ORDER)[None, :]
    full = lambda a: pl.BlockSpec(a.shape, lambda i: (0, 0))
    args = (
        pad2(z, seq_len, hid), t.reshape(seq_len, 1),
        pad2(w1, hid, hid), pad2(b1[None, :], 1, hid),
        pad2(w2, hid, hid), pad2(b2[None, :], 1, hid),
        pad2(w3, hid, hid), pad2(b3[None, :], 1, hid),
        pad2(sin_freq, 3, hid), pad2(w_out, hid, n_out).astype(BF16), decay.reshape(1, n_out),
    )
    bwd_cols = jnp.asarray(is_bwd)
    return pl.pallas_call(
        functools.partial(_filter_taps_body, w1.shape[1] <= LANES // 2 and tl % 16 == 0),
        out_shape=(jax.ShapeDtypeStruct((seq_len, n_out), F32),
                   jax.ShapeDtypeStruct((1, n_out), F32)),
        grid=(seq_len // tl,),
        in_specs=[full(bwd_cols),
                  pl.BlockSpec((tl, hid), lambda i: (i, 0)),
                  pl.BlockSpec((tl, 1), lambda i: (i, 0))] + [full(a) for a in args[2:]],
        out_specs=(pl.BlockSpec((tl, n_out), lambda i: (i, 0)),
                   pl.BlockSpec((1, n_out), lambda i: (0, 0))),
        compiler_params=_cparams(("arbitrary",)),
        name="filter_taps",
    )(bwd_cols, *args)


def _rope_slab(x, cos, sin):
    half = MLA_ROPE // 2
    lane = lax.broadcasted_iota(jnp.int32, x.shape, 1)
    swapped = jnp.where(lane < MLA_NOPE + half, pltpu.roll(x, LANES - half, axis=1),
                        pltpu.roll(x, half, axis=1))
    return x * cos + swapped * sin


def _mla_qkv_body(c_ref, qg_ref, kvg_ref, wq_ref, wk_ref, wv_ref, cos_ref, sin_ref,
                  q_ref, k_ref, v_ref):
    c = c_ref[...]
    cos = cos_ref[...]
    sin = sin_ref[...]
    cq = _rms(c[:, :MLA_Q_LORA], qg_ref[...]).astype(BF16)
    ckv = _rms(c[:, MLA_Q_LORA:MLA_Q_LORA + MLA_KV_LORA], kvg_ref[...]).astype(BF16)
    kr0 = MLA_Q_LORA + MLA_KV_LORA
    k_rope = _rope_slab(c[:, kr0:kr0 + LANES], cos, sin)
    cos_q = cos * SOFTMAX_EXP2_SCALE
    sin_q = sin * SOFTMAX_EXP2_SCALE
    ones_lane = (lax.broadcasted_iota(jnp.int32, (1, LANES), 1) == MLA_V).astype(F32)
    for h0 in range(0, MLA_HEADS, 2):
        cols = slice(h0 * LANES, (h0 + 2) * LANES)
        q2 = _dot(cq, wq_ref[:, cols])
        k2 = _dot(ckv, wk_ref[:, cols])
        v2 = _dot(ckv, wv_ref[:, cols])
        for j in range(2):
            lanes = slice(j * LANES, (j + 1) * LANES)
            q_ref[h0 + j] = _rope_slab(q2[:, lanes], cos_q, sin_q).astype(BF16)
            k_ref[h0 + j] = (k2[:, lanes] + k_rope).astype(BF16)
            v_ref[h0 + j] = (v2[:, lanes] + ones_lane).astype(BF16)


def _mla_qkv(c, q_norm_g, kv_norm_g, wq, wk, wv, cos_t, sin_t, batch, *, tl=512):
    m, cw = c.shape
    seq_len = m // batch
    nl = seq_len // tl
    full = lambda a: pl.BlockSpec(a.shape, lambda bi, i: (0, 0))
    qg = q_norm_g.reshape(1, -1)
    kvg = kv_norm_g.reshape(1, -1)
    head_out = jax.ShapeDtypeStruct((batch, MLA_HEADS, seq_len, LANES), BF16)
    head_spec = pl.BlockSpec((None, MLA_HEADS, tl, LANES), lambda bi, i: (bi, 0, i, 0))
    return pl.pallas_call(
        _mla_qkv_body,
        out_shape=(head_out, head_out, head_out),
        grid=(batch, nl),
        in_specs=[pl.BlockSpec((tl, cw), lambda bi, i: (bi * nl + i, 0)),
                  full(qg), full(kvg), full(wq), full(wk), full(wv),
                  pl.BlockSpec((tl, LANES), lambda bi, i: (i, 0)),
                  pl.BlockSpec((tl, LANES), lambda bi, i: (i, 0))],
        out_specs=(head_spec, head_spec, head_spec),
        compiler_params=_cparams(("parallel", "parallel")),
        name="mla_qkv",
    )(c, qg, kvg, wq, wk, wv, cos_t, sin_t)


def _attention_body(q_ref, qn_ref, k_ref, v_ref, o_ref, *s_refs):
    pair = LANES // MLA_V
    n_heads = q_ref.shape[0]
    n_carry = len(s_refs)

    def scores(q, h):
        return lax.dot_general(q, k_ref[h], (((1,), (1,)), ((), ())), preferred_element_type=F32)

    @pl.when(pl.program_id(2) == 0)
    def _():
        for h, s_ref in enumerate(s_refs):
            s_ref[...] = scores(q_ref[h], h)

    ahead = {h: s_ref[...] for h, s_ref in enumerate(s_refs)}
    for h in range(n_carry, SCORE_LEAD):
        ahead[h] = scores(q_ref[h], h)
    for g in range(n_heads // pair):
        outs = []
        for h in range(g * pair, (g + 1) * pair):
            lead = h + SCORE_LEAD
            if lead < n_heads:
                ahead[lead] = scores(q_ref[lead], lead)
            elif lead - n_heads < n_carry:
                s_refs[lead - n_heads][...] = scores(qn_ref[lead - n_heads], lead - n_heads)
            s = ahead.pop(h)
            p = jnp.exp2(s - jnp.max(s, axis=-1, keepdims=True))
            r = _dot(p.astype(BF16), v_ref[h])
            outs.append(r / r[:, MLA_V:MLA_V + 1])
        lane = lax.broadcasted_iota(jnp.int32, outs[0].shape, 1)
        both = jnp.where(lane < MLA_V, outs[0], pltpu.roll(outs[1], MLA_V, axis=1))
        o_ref[:, g * LANES:(g + 1) * LANES] = both.astype(o_ref.dtype)


def _attention(q, k, v, *, tq=256, group=8):
    batch, heads, seq_len, _ = q.shape
    width = group * MLA_V
    n_tiles = seq_len // tq
    return pl.pallas_call(
        _attention_body,
        out_shape=jax.ShapeDtypeStruct((batch, seq_len, heads * MLA_V), BF16),
        grid=(batch, heads // group, n_tiles),
        in_specs=[
            pl.BlockSpec((None, group, tq, LANES), lambda bi, p, i: (bi, p, i, 0)),
            pl.BlockSpec((None, SCORE_CARRY, tq, LANES),
                         lambda bi, p, i: (bi, p * (group // SCORE_CARRY),
                                           jnp.minimum(i + 1, n_tiles - 1), 0)),
            pl.BlockSpec((None, group, seq_len, LANES), lambda bi, p, i: (bi, p, 0, 0)),
            pl.BlockSpec((None, group, seq_len, LANES), lambda bi, p, i: (bi, p, 0, 0)),
        ],
        out_specs=pl.BlockSpec((None, tq, width), lambda bi, p, i: (bi, i, p)),
        scratch_shapes=[pltpu.VMEM((tq, seq_len), F32)] * SCORE_CARRY,
        compiler_params=_cparams(("parallel", "parallel", "arbitrary")),
        name="attention",
    )(q, q, k, v)


def _mla_weights(w_dq, w_uq, w_dkv, w_ukv):
    d = w_dq.shape[0]
    qh = w_uq.reshape(MLA_Q_LORA, MLA_HEADS, MLA_NOPE + MLA_ROPE)
    zq = jnp.zeros((MLA_Q_LORA, MLA_HEADS, LANES - MLA_NOPE - MLA_ROPE), F32)
    wq = jnp.concatenate([qh, zq], axis=-1).reshape(MLA_Q_LORA, MLA_HEADS * LANES)
    kvh = w_ukv.reshape(MLA_KV_LORA, MLA_HEADS, MLA_NOPE + MLA_V)
    wk = jnp.concatenate([kvh[..., :MLA_NOPE],
                          jnp.zeros((MLA_KV_LORA, MLA_HEADS, LANES - MLA_NOPE), F32)],
                         axis=-1).reshape(MLA_KV_LORA, MLA_HEADS * LANES)
    wv = jnp.concatenate([kvh[..., MLA_NOPE:],
                          jnp.zeros((MLA_KV_LORA, MLA_HEADS, LANES - MLA_V), F32)],
                         axis=-1).reshape(MLA_KV_LORA, MLA_HEADS * LANES)
    w_kr = jnp.concatenate([jnp.zeros((d, MLA_NOPE), F32), w_dkv[:, MLA_KV_LORA:],
                            jnp.zeros((d, LANES - MLA_NOPE - MLA_ROPE), F32)], axis=-1)
    w_down = jnp.concatenate([w_dq, w_dkv[:, :MLA_KV_LORA], w_kr], axis=-1)
    return w_down.astype(BF16), wq.astype(BF16), wk.astype(BF16), wv.astype(BF16)


def _rope_tables(seq_len):
    inv_freq = ROPE_THETA ** (-jnp.arange(0, MLA_ROPE, 2, dtype=F32) / MLA_ROPE)
    ang = jnp.arange(seq_len, dtype=F32)[:, None] * inv_freq[None, :]
    cos = jnp.cos(ang)
    sin = jnp.sin(ang)
    tail = LANES - MLA_NOPE - MLA_ROPE
    cos_t = jnp.concatenate([jnp.ones((seq_len, MLA_NOPE), F32), cos, cos,
                             jnp.ones((seq_len, tail), F32)], axis=-1)
    sin_t = jnp.concatenate([jnp.zeros((seq_len, MLA_NOPE), F32), -sin, sin,
                             jnp.zeros((seq_len, tail), F32)], axis=-1)
    return cos_t, sin_t


def _position_features(seq_len, pos_dim):
    t = jnp.linspace(0.0, 1.0, seq_len, dtype=F32)
    bands = (pos_dim - 1) // 2
    w = 2.0 * math.pi * jnp.arange(seq_len, dtype=F32) / seq_len
    f = jnp.linspace(1e-4, bands - 1, bands, dtype=F32)
    phase = w[:, None] * f[None, :]
    z = jnp.concatenate([t[:, None], jnp.cos(phase), -jnp.sin(phase)], axis=-1)
    return t, z


def kernel(x, norm_g, ffn_w_gate, ffn_w_up, ffn_w_down, mix_w_in, pool_w, pool_scale, hyena_conv_w, hyena_conv_b, hyena_ffn_w1, hyena_ffn_b1, hyena_ffn_w2, hyena_ffn_b2, hyena_ffn_w3, hyena_ffn_b3, hyena_sin_freq, hyena_ffn_w_out, hyena_decay, hyena_bias, mix_w_out, mla_w_dq, mla_q_norm_g, mla_w_uq, mla_w_dkv, mla_kv_norm_g, mla_w_ukv, mla_w_o, final_norm_g):
    batch, seq_len, d = x.shape
    depth = norm_g.shape[0]
    assert 2 * seq_len == DFT_N1 * DFT_N2
    hy_width = hyena_bias.shape[-1]
    t_pos, z_pos = _position_features(seq_len, hyena_ffn_w1.shape[1])
    cos_t, sin_t = _rope_tables(seq_len)

    x = x.reshape(batch * seq_len, d)
    ffn_w = (ffn_w_gate, ffn_w_up, ffn_w_down)
    for i in range(depth):
        j = i // 2
        if i % 2 == 0:
            x, proj = _ffn(x, norm_g[i, 0], ffn_w, (i, 0), next_g=norm_g[i, 1],
                           w_next=mix_w_in[j].astype(BF16))
            proj = proj.reshape(batch, seq_len, -1)
            y_pool = _pool_mixer(proj, pool_w[j].astype(BF16), pool_scale[j])
            taps, sumsq = _filter_taps(z_pos, t_pos, hyena_ffn_w1[j], hyena_ffn_b1[j],
                                       hyena_ffn_w2[j], hyena_ffn_b2[j], hyena_ffn_w3[j],
                                       hyena_ffn_b3[j], hyena_sin_freq[j], hyena_ffn_w_out[j],
                                       hyena_decay[j])
            spec = _filter_spectrum(taps, sumsq, hy_width)
            y_hyena = _hyena_mixer(proj, pool_scale.shape[-1], hyena_conv_w[j], hyena_conv_b[j],
                                   spec, hyena_bias[j])
            mix = (y_pool.reshape(batch * seq_len, -1), y_hyena.reshape(batch * seq_len, -1))
            w_mix = mix_w_out[j].astype(BF16)
        else:
            w_down, wq, wk, wv = _mla_weights(mla_w_dq[j], mla_w_uq[j], mla_w_dkv[j],
                                              mla_w_ukv[j])
            x, c = _ffn(x, norm_g[i, 0], ffn_w, (i, 0), next_g=norm_g[i, 1], w_next=w_down)
            q, k, v = _mla_qkv(c, mla_q_norm_g[j], mla_kv_norm_g[j], wq, wk, wv,
                               cos_t, sin_t, batch)
            mix = (_attention(q, k, v).reshape(batch * seq_len, -1),)
            w_mix = mla_w_o[j].astype(BF16)
        x = _ffn(x, norm_g[i, 2], ffn_w, (i, 1), mix=mix, w_mix=w_mix,
                 final_g=final_norm_g if i == depth - 1 else None)
    return x.reshape(batch, seq_len, d)
```

```python
import functools
import math

import numpy as np
import jax
import jax.numpy as jnp
from jax import lax
from jax.experimental import pallas as pl
from jax.experimental.pallas import tpu as pltpu

F32 = jnp.float32
BF16 = jnp.bfloat16

RMS_EPS = 1e-6
MACARON_WEIGHT = 0.5
POOL_WINDOWS = (2, 4, 8, 16)
HYENA_ORDER = 2
HYENA_SHORT_CONV = 3
MLA_HEADS = 16
MLA_Q_LORA = 256
MLA_KV_LORA = 128
MLA_NOPE = 64
MLA_ROPE = 32
MLA_V = 64
ROPE_THETA = 10000.0
SOFTMAX_EXP2_SCALE = (MLA_NOPE + MLA_ROPE) ** -0.5 * math.log2(math.e)

LANES = 128
SUBLANES = 8
VMEM_LIMIT = 56 * 2**20

DFT_N1 = 64
DFT_N2 = 128
DFT_K1 = DFT_N1 // 2 + 1
DFT_QP = 40
DFT_PITCH = DFT_N2 + SUBLANES
PLANE_ROWS_PER_DOT = 8
PLANE_UNROLL = 16
K1_UNROLL = 33
HALO = 16
FFN_LOAD_STEPS = 8
SCORE_LEAD = 3
SCORE_CARRY = 1


def _cparams(semantics):
    return pltpu.CompilerParams(dimension_semantics=semantics, vmem_limit_bytes=VMEM_LIMIT)


def _rms(x, g):
    return x * lax.rsqrt(jnp.mean(x * x, axis=-1, keepdims=True) + RMS_EPS) * g


def _dot(a, b):
    return jnp.dot(a, b, preferred_element_type=F32)


def _ffn_body(n_mix, has_next, has_final, *refs):
    refs = list(refs)
    x_ref, g_ref, wg_ref, wu_ref, wd_ref = refs[:5]
    del refs[:5]
    mix_refs = [refs.pop(0) for _ in range(n_mix)]
    w_mix_ref = refs.pop(0) if n_mix else None
    next_g_ref, next_w_ref = (refs.pop(0), refs.pop(0)) if has_next else (None, None)
    final_g_ref = refs.pop(0) if has_final else None
    o_ref = refs.pop(0)
    next_o_ref = refs.pop(0) if has_next else None
    wg_bf, wu_bf, wd_bf = refs
    step = pl.program_id(0)

    @pl.when(step < FFN_LOAD_STEPS)
    def _():
        for src_ref, dst_ref in ((wg_ref, wg_bf), (wu_ref, wu_bf), (wd_ref, wd_bf)):
            rows = src_ref.shape[0]
            dst_ref[pl.ds(pl.multiple_of(step * rows, rows), rows), :] = src_ref[...].astype(BF16)

    @pl.when(step >= FFN_LOAD_STEPS)
    def _():
        x = x_ref[...]
        row = 0
        for a_ref in mix_refs:
            k = a_ref.shape[1]
            x = x + _dot(a_ref[...].astype(BF16), w_mix_ref[row:row + k, :])
            row += k
        h = _rms(x, g_ref[...]).astype(BF16)
        gate = _dot(h, wg_bf[...])
        up = _dot(h, wu_bf[...])
        act = gate / (1.0 + jnp.exp(-gate)) * up
        y = x + MACARON_WEIGHT * _dot(act.astype(BF16), wd_bf[...])
        if has_final:
            y = _rms(y, final_g_ref[...])
        o_ref[...] = y
        if has_next:
            next_o_ref[...] = _dot(_rms(y, next_g_ref[...]).astype(BF16), next_w_ref[...])


def _resident(shape):
    return pl.BlockSpec(shape, lambda *_: (0,) * len(shape), pipeline_mode=pl.Buffered(1))


def _streamed_rows(stack, index):
    lead = len(index)
    rows, cols = stack.shape[lead:]
    return pl.BlockSpec(
        (None,) * lead + (rows // FFN_LOAD_STEPS, cols),
        lambda i: tuple(index) + (jnp.minimum(i, FFN_LOAD_STEPS - 1), 0))


def _ffn(x, g, w_stacks, index, *, mix=(), w_mix=None, next_g=None, w_next=None, final_g=None,
         tm=512):
    m, d = x.shape
    tile = lambda i: jnp.maximum(i - FFN_LOAD_STEPS, 0)
    row_block = lambda a: pl.BlockSpec((tm, a.shape[1]), lambda i: (tile(i), 0))
    args = [x, g.reshape(1, d), *w_stacks]
    in_specs = [row_block(x), _resident((1, d))] + [_streamed_rows(w, index) for w in w_stacks]
    for a in mix:
        args.append(a)
        in_specs.append(row_block(a))
    if mix:
        args.append(w_mix)
        in_specs.append(_resident(w_mix.shape))
    if w_next is not None:
        args += [next_g.reshape(1, d), w_next]
        in_specs += [_resident((1, d)), _resident(w_next.shape)]
    if final_g is not None:
        args.append(final_g.reshape(1, d))
        in_specs.append(_resident((1, d)))
    out_shape = [jax.ShapeDtypeStruct((m, d), F32)]
    out_specs = [pl.BlockSpec((tm, d), lambda i: (tile(i), 0))]
    if w_next is not None:
        out_shape.append(jax.ShapeDtypeStruct((m, w_next.shape[1]), F32))
        out_specs.append(pl.BlockSpec((tm, w_next.shape[1]), lambda i: (tile(i), 0)))
    lead = len(index)
    outs = pl.pallas_call(
        functools.partial(_ffn_body, len(mix), w_next is not None, final_g is not None),
        out_shape=out_shape,
        grid=(FFN_LOAD_STEPS + m // tm,),
        in_specs=in_specs,
        out_specs=out_specs,
        scratch_shapes=[pltpu.VMEM(w.shape[lead:], BF16) for w in w_stacks],
        compiler_params=_cparams(("arbitrary",)),
        name="ffn",
    )(*args)
    return outs if w_next is not None else outs[0]


def _pool_body(seq_len, cur_ref, prev_ref, next_ref, pw_ref, ps_ref, yp_ref, ext_ref):
    i = pl.program_id(1)
    tl = cur_ref.shape[0]
    pool_width = yp_ref.shape[1]
    group = pool_width // len(POOL_WINDOWS)
    ext_ref[0:HALO, :] = jnp.where(i > 0, prev_ref[...], 0.0)
    ext_ref[HALO:HALO + tl, :] = cur_ref[...]
    ext_ref[HALO + tl:, :] = jnp.where(i < pl.num_programs(1) - 1, next_ref[...], 0.0)

    t = i * tl + lax.broadcasted_iota(jnp.int32, (tl, group), 0)
    for g, w in enumerate(POOL_WINDOWS):
        c0 = g * group
        lo = jnp.clip(t - w // 2, 0, seq_len)
        hi = jnp.clip(t - w // 2 + w, 0, seq_len)
        cnt = (hi - lo).astype(F32)
        s = ext_ref[:, c0:c0 + group]
        n_rows = s.shape[0]
        span = 1
        while span < w:
            s = s + pltpu.roll(s, n_rows - span, axis=0)
            span *= 2
        s = pltpu.roll(s, w // 2, axis=0)[HALO:HALO + tl]
        p = s / cnt - cur_ref[:, c0:c0 + group]
        y = _dot(p.astype(BF16), pw_ref[g])
        yp_ref[:, c0:c0 + group] = y * ps_ref[:, c0:c0 + group]


def _pool_mixer(proj, pool_w, pool_scale, *, tl=1024):
    b, seq_len, _ = proj.shape
    pool_width = pool_scale.shape[0]
    assert all(w & (w - 1) == 0 and w <= HALO for w in POOL_WINDOWS)
    hb = tl // HALO
    last = seq_len // HALO - 1
    return pl.pallas_call(
        functools.partial(_pool_body, seq_len),
        out_shape=jax.ShapeDtypeStruct((b, seq_len, pool_width), F32),
        grid=(b, seq_len // tl),
        in_specs=[
            pl.BlockSpec((None, tl, pool_width), lambda bi, i: (bi, i, 0)),
            pl.BlockSpec((None, HALO, pool_width),
                         lambda bi, i: (bi, jnp.maximum(i * hb - 1, 0), 0)),
            pl.BlockSpec((None, HALO, pool_width),
                         lambda bi, i: (bi, jnp.minimum((i + 1) * hb, last), 0)),
            pl.BlockSpec(pool_w.shape, lambda bi, i: (0, 0, 0)),
            pl.BlockSpec((1, pool_width), lambda bi, i: (0, 0)),
        ],
        out_specs=pl.BlockSpec((None, tl, pool_width), lambda bi, i: (bi, i, 0)),
        scratch_shapes=[pltpu.VMEM((tl + 2 * HALO, pool_width), F32)],
        compiler_params=_cparams(("parallel", "parallel")),
        name="pool_mixer",
    )(proj, proj, proj, pool_w, pool_scale.reshape(1, pool_width))


@functools.lru_cache(maxsize=None)
def _dft_constants():
    n = DFT_N1 * DFT_N2
    half = DFT_N1 // 2
    k1 = np.arange(DFT_K1)
    n1 = np.arange(half)
    ang = 2.0 * np.pi * np.outer(k1, n1) / DFT_N1
    f_fwd = np.zeros((2 * DFT_QP, half))
    f_fwd[:DFT_K1] = np.cos(ang)
    f_fwd[DFT_QP:DFT_QP + DFT_K1] = -np.sin(ang)
    n2 = np.arange(DFT_N2)
    k2 = np.arange(DFT_N2)
    g = np.zeros((DFT_K1, 2 * DFT_N2, 2 * DFT_N2))
    for a in range(DFT_K1):
        ph = 2.0 * np.pi * np.outer(a + DFT_N1 * k2, n2) / n
        gre, gim = np.cos(ph), -np.sin(ph)
        g[a] = np.block([[gre, -gim], [gim, gre]])
    weight = np.full(DFT_K1, 2.0)
    weight[0] = 1.0
    weight[-1] = 1.0
    f_inv = np.zeros((half, 2 * DFT_QP))
    f_inv[:, :DFT_K1] = weight * np.cos(ang.T) / n
    f_inv[:, DFT_QP:DFT_QP + DFT_K1] = -weight * np.sin(ang.T) / n
    return (f_fwd.astype(np.float32), g.astype(np.float32),
            np.transpose(g, (0, 2, 1)).astype(np.float32), f_inv.astype(np.float32))


def _pad_rows(dst_ref, src):
    for n1 in range(DFT_N1 // 2):
        dst_ref[n1 * DFT_PITCH:n1 * DFT_PITCH + DFT_N2, :] = src[n1 * DFT_N2:(n1 + 1) * DFT_N2, :]


def _contract_planes(src_ref, dst_ref, f_ref):
    n_dst, n_src = f_ref.shape

    def step(j, carry):
        r0 = j * PLANE_ROWS_PER_DOT
        x = jnp.concatenate(
            [src_ref[pl.ds(r0 + r, n_src, stride=DFT_PITCH), :].astype(BF16)
             for r in range(PLANE_ROWS_PER_DOT)], axis=1)
        y = _dot(f_ref[...], x)
        for r in range(PLANE_ROWS_PER_DOT):
            dst_ref[pl.ds(r0 + r, n_dst, stride=DFT_PITCH), :] = y[:, r * LANES:(r + 1) * LANES]
        return carry

    lax.fori_loop(0, DFT_N2 // PLANE_ROWS_PER_DOT, step, 0, unroll=PLANE_UNROLL)


def _load_planes(a_ref, k1):
    re = a_ref[pl.ds(pl.multiple_of(k1 * DFT_PITCH, SUBLANES), DFT_N2), :]
    im = a_ref[pl.ds(pl.multiple_of((DFT_QP + k1) * DFT_PITCH, SUBLANES), DFT_N2), :]
    return jnp.concatenate([re, im], axis=0)


def _store_planes(a_ref, k1, z):
    a_ref[pl.ds(pl.multiple_of(k1 * DFT_PITCH, SUBLANES), DFT_N2), :] = z[:DFT_N2]
    a_ref[pl.ds(pl.multiple_of((DFT_QP + k1) * DFT_PITCH, SUBLANES), DFT_N2), :] = z[DFT_N2:]


def _spectrum_body(fw_ref, bw_ref, ssf_ref, ssb_ref, ff_ref, gt_ref, h_ref, u_ref, a_ref):
    norm = lax.rsqrt(ssf_ref[...] + ssb_ref[...]).T
    for part, src_ref in enumerate((fw_ref, bw_ref)):
        _pad_rows(u_ref, src_ref)
        _contract_planes(u_ref, a_ref, ff_ref)
        sign = 1.0 if part == 0 else -1.0

        def step(k1, carry):
            xt = _dot(_load_planes(a_ref, k1).T.astype(BF16), gt_ref[k1])
            re = xt[:, :DFT_N2] * norm
            im = xt[:, DFT_N2:] * (sign * norm)
            if part == 0:
                h_ref[k1, :, :DFT_N2] = re
                h_ref[k1, :, DFT_N2:] = im
            else:
                h_ref[k1, :, :DFT_N2] += re
                h_ref[k1, :, DFT_N2:] += im
            return carry

        lax.fori_loop(0, DFT_K1, step, 0, unroll=K1_UNROLL)


def _filter_spectrum(taps, sumsq, width):
    seq_len = taps.shape[0]
    ff, _, gt, _ = _dft_constants()
    cb = width // LANES
    half = DFT_N1 // 2
    return pl.pallas_call(
        _spectrum_body,
        out_shape=jax.ShapeDtypeStruct((HYENA_ORDER, DFT_K1, width, 2 * DFT_N2), F32),
        grid=(HYENA_ORDER, cb),
        in_specs=[
            pl.BlockSpec((seq_len, LANES), lambda o, c: (0, (2 * o) * cb + c)),
            pl.BlockSpec((seq_len, LANES), lambda o, c: (0, (2 * o + 1) * cb + c)),
            pl.BlockSpec((1, LANES), lambda o, c: (0, (2 * o) * cb + c)),
            pl.BlockSpec((1, LANES), lambda o, c: (0, (2 * o + 1) * cb + c)),
            pl.BlockSpec((2 * DFT_QP, half), lambda o, c: (0, 0)),
            pl.BlockSpec((DFT_K1, 2 * DFT_N2, 2 * DFT_N2), lambda o, c: (0, 0, 0)),
        ],
        out_specs=pl.BlockSpec((None, DFT_K1, LANES, 2 * DFT_N2), lambda o, c: (o, 0, c, 0)),
        scratch_shapes=[pltpu.VMEM((half * DFT_PITCH, LANES), F32),
                        pltpu.VMEM((2 * DFT_QP * DFT_PITCH, LANES), F32)],
        compiler_params=_cparams(("parallel", "parallel")),
        name="filter_spectrum",
    )(taps, taps, sumsq, sumsq, jnp.asarray(ff).astype(BF16), jnp.asarray(gt).astype(BF16))


def _short_conv_rows(x_ref, n1, w_ref, b_ref):
    r0 = n1 * DFT_N2
    cur = x_ref[r0:r0 + DFT_N2, :]
    row = lax.broadcasted_iota(jnp.int32, cur.shape, 0)
    if r0 == 0:
        prev = jnp.where(row == 0, 0.0, pltpu.roll(cur, 1, axis=0))
    else:
        prev = x_ref[r0 - 1:r0 + DFT_N2 - 1, :]
    if r0 + DFT_N2 == x_ref.shape[0]:
        nxt = jnp.where(row == DFT_N2 - 1, 0.0, pltpu.roll(cur, DFT_N2 - 1, axis=0))
    else:
        nxt = x_ref[r0 + 1:r0 + DFT_N2 + 1, :]
    return b_ref[...] + prev * w_ref[0:1, :] + cur * w_ref[1:2, :] + nxt * w_ref[2:3, :]


def _hyena_body(v_ref, gate_ref, vw_ref, vb_ref, gw_ref, gb_ref, h_ref, bias_ref, ff_ref, g_ref,
                gt_ref, fi_ref, o_ref, u_ref, y_ref, a_ref):
    order = pl.program_id(2)
    half = DFT_N1 // 2

    @pl.when(order == 0)
    def _():
        for n1 in range(half):
            u_ref[n1 * DFT_PITCH:n1 * DFT_PITCH + DFT_N2, :] = _short_conv_rows(
                v_ref, n1, vw_ref, vb_ref)

    _contract_planes(u_ref, a_ref, ff_ref)

    def freq_step(k1, carry):
        xt = _dot(_load_planes(a_ref, k1).T.astype(BF16), gt_ref[k1])
        xre, xim = xt[:, :DFT_N2], xt[:, DFT_N2:]
        hre, him = h_ref[k1, :, :DFT_N2], h_ref[k1, :, DFT_N2:]
        yt = jnp.concatenate([xre * hre - xim * him, xre * him + xim * hre], axis=1)
        _store_planes(a_ref, k1, _dot(yt.astype(BF16), g_ref[k1]).T)
        return carry

    lax.fori_loop(0, DFT_K1, freq_step, 0, unroll=K1_UNROLL)
    _contract_planes(a_ref, y_ref, fi_ref)

    bias = bias_ref[...]
    for n1 in range(half):
        rows = slice(n1 * DFT_PITCH, n1 * DFT_PITCH + DFT_N2)
        gate = _short_conv_rows(gate_ref, n1, gw_ref, gb_ref)
        z = gate * (y_ref[rows, :] + u_ref[rows, :] * bias)
        u_ref[rows, :] = z
        o_ref[n1 * DFT_N2:(n1 + 1) * DFT_N2, :] = z


def _hyena_mixer(proj, first_col, conv_w, conv_b, spec, bias):
    assert HYENA_SHORT_CONV == 3
    b, seq_len, _ = proj.shape
    width = bias.shape[1]
    cb = width // LANES
    c0 = first_col // LANES
    half = DFT_N1 // 2
    ff, g, gt, fi = _dft_constants()
    value_col = lambda c, bi, o: HYENA_ORDER * cb + c
    gate_col = lambda c, bi, o: o * cb + c
    conv_b = conv_b.reshape(1, -1)
    return pl.pallas_call(
        _hyena_body,
        out_shape=jax.ShapeDtypeStruct((b, seq_len, width), F32),
        grid=(cb, b, HYENA_ORDER),
        in_specs=[
            pl.BlockSpec((None, seq_len, LANES), lambda c, bi, o: (bi, 0, c0 + value_col(c, bi, o))),
            pl.BlockSpec((None, seq_len, LANES), lambda c, bi, o: (bi, 0, c0 + gate_col(c, bi, o))),
            pl.BlockSpec((HYENA_SHORT_CONV, LANES), lambda c, bi, o: (0, value_col(c, bi, o))),
            pl.BlockSpec((1, LANES), lambda c, bi, o: (0, value_col(c, bi, o))),
            pl.BlockSpec((HYENA_SHORT_CONV, LANES), lambda c, bi, o: (0, gate_col(c, bi, o))),
            pl.BlockSpec((1, LANES), lambda c, bi, o: (0, gate_col(c, bi, o))),
            pl.BlockSpec((None, DFT_K1, LANES, 2 * DFT_N2), lambda c, bi, o: (o, 0, c, 0)),
            pl.BlockSpec((None, 1, LANES), lambda c, bi, o: (o, 0, c)),
            _resident((2 * DFT_QP, half)),
            _resident((DFT_K1, 2 * DFT_N2, 2 * DFT_N2)),
            _resident((DFT_K1, 2 * DFT_N2, 2 * DFT_N2)),
            _resident((half, 2 * DFT_QP)),
        ],
        out_specs=pl.BlockSpec((None, seq_len, LANES), lambda c, bi, o: (bi, 0, c)),
        scratch_shapes=[pltpu.VMEM((half * DFT_PITCH, LANES), F32),
                        pltpu.VMEM((half * DFT_PITCH, LANES), F32),
                        pltpu.VMEM((2 * DFT_QP * DFT_PITCH, LANES), F32)],
        compiler_params=_cparams(("parallel", "parallel", "arbitrary")),
        name="hyena_mixer",
    )(proj, proj, conv_w, conv_b, conv_w, conv_b, spec, bias.reshape(HYENA_ORDER, 1, width),
      jnp.asarray(ff).astype(BF16), jnp.asarray(g).astype(BF16), jnp.asarray(gt).astype(BF16),
      jnp.asarray(fi).astype(BF16))


def _hdot(a, b):
    return jnp.dot(a, b, preferred_element_type=F32, precision=lax.Precision.HIGHEST)


def _sin_half_lanes(arg):
    rows = arg.shape[0] // 2
    lane = lax.broadcasted_iota(jnp.int32, (rows, LANES), 1)
    packed = jnp.where(lane < LANES // 2, arg[:rows], pltpu.roll(arg[rows:], LANES // 2, axis=1))
    s = jnp.sin(packed)
    return jnp.concatenate([s, pltpu.roll(s, LANES // 2, axis=1)], axis=0)


def _filter_taps_body(pack_sin, bwd_cols, z_ref, t_ref, w1_ref, b1_ref, w2_ref, b2_ref, w3_ref,
                      b3_ref, sf_ref, wo_ref, decay_ref, taps_ref, ss_ref):
    i = pl.program_id(0)
    sin = _sin_half_lanes if pack_sin else jnp.sin
    h = sin(sf_ref[0:1, :] * (_hdot(z_ref[...], w1_ref[...]) + b1_ref[...]))
    h = sin(sf_ref[1:2, :] * (_hdot(h, w2_ref[...]) + b2_ref[...]))
    h = sin(sf_ref[2:3, :] * (_hdot(h, w3_ref[...]) + b3_ref[...]))
    h = _dot(h.astype(BF16), wo_ref[...])
    h = h * jnp.exp(-t_ref[...] * jnp.abs(decay_ref[...]))
    row = i * h.shape[0] + lax.broadcasted_iota(jnp.int32, h.shape, 0)
    h = jnp.where((row == 0) & (bwd_cols[...] > 0.0), 0.0, h)
    taps_ref[...] = h

    @pl.when(i == 0)
    def _():
        ss_ref[...] = jnp.zeros_like(ss_ref)

    ss_ref[...] += jnp.sum(h * h, axis=0, keepdims=True)


def _filter_taps(z, t, w1, b1, w2, b2, w3, b3, sin_freq, w_out, decay, *, tl=512):
    seq_len = z.shape[0]
    hid = LANES
    n_out = w_out.shape[1]
    width = n_out // (2 * HYENA_ORDER)

    def pad2(a, rows, cols):
        return jnp.pad(a, ((0, rows - a.shape[0]), (0, cols - a.shape[1])))

    is_bwd = np.tile(np.repeat(np.array([0.0, 1.0], np.float32), width), HYENA_ORDER)[None, :]
    full = lambda a: pl.BlockSpec(a.shape, lambda i: (0, 0))
    args = (
        pad2(z, seq_len, hid), t.reshape(seq_len, 1),
        pad2(w1, hid, hid), pad2(b1[None, :], 1, hid),
        pad2(w2, hid, hid), pad2(b2[None, :], 1, hid),
        pad2(w3, hid, hid), pad2(b3[None, :], 1, hid),
        pad2(sin_freq, 3, hid), pad2(w_out, hid, n_out).astype(BF16), decay.reshape(1, n_out),
    )
    bwd_cols = jnp.asarray(is_bwd)
    return pl.pallas_call(
        functools.partial(_filter_taps_body, w1.shape[1] <= LANES // 2 and tl % 16 == 0),
        out_shape=(jax.ShapeDtypeStruct((seq_len, n_out), F32),
                   jax.ShapeDtypeStruct((1, n_out), F32)),
        grid=(seq_len // tl,),
        in_specs=[full(bwd_cols),
                  pl.BlockSpec((tl, hid), lambda i: (i, 0)),
                  pl.BlockSpec((tl, 1), lambda i: (i, 0))] + [full(a) for a in args[2:]],
        out_specs=(pl.BlockSpec((tl, n_out), lambda i: (i, 0)),
                   pl.BlockSpec((1, n_out), lambda i: (0, 0))),
        compiler_params=_cparams(("arbitrary",)),
        name="filter_taps",
    )(bwd_cols, *args)


def _rope_slab(x, cos, sin):
    half = MLA_ROPE // 2
    lane = lax.broadcasted_iota(jnp.int32, x.shape, 1)
    swapped = jnp.where(lane < MLA_NOPE + half, pltpu.roll(x, LANES - half, axis=1),
                        pltpu.roll(x, half, axis=1))
    return x * cos + swapped * sin


def _mla_qkv_body(c_ref, qg_ref, kvg_ref, wq_ref, wqs_ref, wk_ref, wv_ref, cos_ref, sin_ref,
                  q_ref, k_ref, v_ref):
    c = c_ref[...]
    cos = cos_ref[...]
    sin = sin_ref[...]
    cq = _rms(c[:, :MLA_Q_LORA], qg_ref[...]).astype(BF16)
    ckv = _rms(c[:, MLA_Q_LORA:MLA_Q_LORA + MLA_KV_LORA], kvg_ref[...]).astype(BF16)
    kr0 = MLA_Q_LORA + MLA_KV_LORA
    k_rope = _rope_slab(c[:, kr0:kr0 + LANES], cos, sin)
    cos_q = cos * SOFTMAX_EXP2_SCALE
    sin_q = sin * SOFTMAX_EXP2_SCALE
    ones_lane = (lax.broadcasted_iota(jnp.int32, (1, LANES), 1) == MLA_V).astype(F32)
    for h0 in range(0, MLA_HEADS, 2):
        cols = slice(h0 * LANES, (h0 + 2) * LANES)
        q2 = _dot(cq, wq_ref[:, cols])
        q2_swap = _dot(cq, wqs_ref[:, cols])
        k2 = _dot(ckv, wk_ref[:, cols])
        v2 = _dot(ckv, wv_ref[:, cols])
        for j in range(2):
            lanes = slice(j * LANES, (j + 1) * LANES)
            q_ref[h0 + j] = (q2[:, lanes] * cos_q + q2_swap[:, lanes] * sin_q).astype(BF16)
            k_ref[h0 + j] = (k2[:, lanes] + k_rope).astype(BF16)
            v_ref[h0 + j] = (v2[:, lanes] + ones_lane).astype(BF16)


def _mla_qkv(c, q_norm_g, kv_norm_g, wq, wq_swap, wk, wv, cos_t, sin_t, batch, *, tl=1024):
    m, cw = c.shape
    seq_len = m // batch
    nl = seq_len // tl
    full = lambda a: pl.BlockSpec(a.shape, lambda bi, i: (0, 0))
    qg = q_norm_g.reshape(1, -1)
    kvg = kv_norm_g.reshape(1, -1)
    head_out = jax.ShapeDtypeStruct((batch, MLA_HEADS, seq_len, LANES), BF16)
    head_spec = pl.BlockSpec((None, MLA_HEADS, tl, LANES), lambda bi, i: (bi, 0, i, 0))
    return pl.pallas_call(
        _mla_qkv_body,
        out_shape=(head_out, head_out, head_out),
        grid=(batch, nl),
        in_specs=[pl.BlockSpec((tl, cw), lambda bi, i: (bi * nl + i, 0)),
                  full(qg), full(kvg), full(wq), full(wq_swap), full(wk), full(wv),
                  pl.BlockSpec((tl, LANES), lambda bi, i: (i, 0)),
                  pl.BlockSpec((tl, LANES), lambda bi, i: (i, 0))],
        out_specs=(head_spec, head_spec, head_spec),
        compiler_params=_cparams(("parallel", "parallel")),
        name="mla_qkv",
    )(c, qg, kvg, wq, wq_swap, wk, wv, cos_t, sin_t)


def _attention_body(q_ref, qn_ref, k_ref, v_ref, o_ref, *s_refs):
    pair = LANES // MLA_V
    n_heads = q_ref.shape[0]
    n_carry = len(s_refs)

    def scores(q, h):
        return lax.dot_general(q, k_ref[h], (((1,), (1,)), ((), ())), preferred_element_type=F32)

    @pl.when(pl.program_id(2) == 0)
    def _():
        for h, s_ref in enumerate(s_refs):
            s_ref[...] = scores(q_ref[h], h)

    ahead = {h: s_ref[...] for h, s_ref in enumerate(s_refs)}
    for h in range(n_carry, SCORE_LEAD):
        ahead[h] = scores(q_ref[h], h)
    for g in range(n_heads // pair):
        outs = []
        for h in range(g * pair, (g + 1) * pair):
            lead = h + SCORE_LEAD
            if lead < n_heads:
                ahead[lead] = scores(q_ref[lead], lead)
            elif lead - n_heads < n_carry:
                s_refs[lead - n_heads][...] = scores(qn_ref[lead - n_heads], lead - n_heads)
            s = ahead.pop(h)
            p = jnp.exp2(s - jnp.max(s, axis=-1, keepdims=True))
            r = _dot(p.astype(BF16), v_ref[h])
            outs.append(r / r[:, MLA_V:MLA_V + 1])
        lane = lax.broadcasted_iota(jnp.int32, outs[0].shape, 1)
        both = jnp.where(lane < MLA_V, outs[0], pltpu.roll(outs[1], MLA_V, axis=1))
        o_ref[:, g * LANES:(g + 1) * LANES] = both.astype(o_ref.dtype)


def _attention(q, k, v, *, tq=256, group=8):
    batch, heads, seq_len, _ = q.shape
    width = group * MLA_V
    n_tiles = seq_len // tq
    return pl.pallas_call(
        _attention_body,
        out_shape=jax.ShapeDtypeStruct((batch, seq_len, heads * MLA_V), BF16),
        grid=(batch, heads // group, n_tiles),
        in_specs=[
            pl.BlockSpec((None, group, tq, LANES), lambda bi, p, i: (bi, p, i, 0)),
            pl.BlockSpec((None, SCORE_CARRY, tq, LANES),
                         lambda bi, p, i: (bi, p * (group // SCORE_CARRY),
                                           jnp.minimum(i + 1, n_tiles - 1), 0)),
            pl.BlockSpec((None, group, seq_len, LANES), lambda bi, p, i: (bi, p, 0, 0)),
            pl.BlockSpec((None, group, seq_len, LANES), lambda bi, p, i: (bi, p, 0, 0)),
        ],
        out_specs=pl.BlockSpec((None, tq, width), lambda bi, p, i: (bi, i, p)),
        scratch_shapes=[pltpu.VMEM((tq, seq_len), F32)] * SCORE_CARRY,
        compiler_params=_cparams(("parallel", "parallel", "arbitrary")),
        name="attention",
    )(q, q, k, v)


def _mla_weights(w_dq, w_uq, w_dkv, w_ukv):
    d = w_dq.shape[0]
    qh = w_uq.reshape(MLA_Q_LORA, MLA_HEADS, MLA_NOPE + MLA_ROPE)
    zq = jnp.zeros((MLA_Q_LORA, MLA_HEADS, LANES - MLA_NOPE - MLA_ROPE), F32)
    wq = jnp.concatenate([qh, zq], axis=-1).reshape(MLA_Q_LORA, MLA_HEADS * LANES)
    half = MLA_ROPE // 2
    wq_swap = jnp.concatenate([jnp.zeros((MLA_Q_LORA, MLA_HEADS, MLA_NOPE), F32),
                               qh[..., MLA_NOPE + half:], qh[..., MLA_NOPE:MLA_NOPE + half], zq],
                              axis=-1).reshape(MLA_Q_LORA, MLA_HEADS * LANES)
    kvh = w_ukv.reshape(MLA_KV_LORA, MLA_HEADS, MLA_NOPE + MLA_V)
    wk = jnp.concatenate([kvh[..., :MLA_NOPE],
                          jnp.zeros((MLA_KV_LORA, MLA_HEADS, LANES - MLA_NOPE), F32)],
                         axis=-1).reshape(MLA_KV_LORA, MLA_HEADS * LANES)
    wv = jnp.concatenate([kvh[..., MLA_NOPE:],
                          jnp.zeros((MLA_KV_LORA, MLA_HEADS, LANES - MLA_V), F32)],
                         axis=-1).reshape(MLA_KV_LORA, MLA_HEADS * LANES)
    w_kr = jnp.concatenate([jnp.zeros((d, MLA_NOPE), F32), w_dkv[:, MLA_KV_LORA:],
                            jnp.zeros((d, LANES - MLA_NOPE - MLA_ROPE), F32)], axis=-1)
    w_down = jnp.concatenate([w_dq, w_dkv[:, :MLA_KV_LORA], w_kr], axis=-1)
    return (w_down.astype(BF16), wq.astype(BF16), wq_swap.astype(BF16), wk.astype(BF16),
            wv.astype(BF16))


def _rope_tables(seq_len):
    inv_freq = ROPE_THETA ** (-jnp.arange(0, MLA_ROPE, 2, dtype=F32) / MLA_ROPE)
    ang = jnp.arange(seq_len, dtype=F32)[:, None] * inv_freq[None, :]
    cos = jnp.cos(ang)
    sin = jnp.sin(ang)
    tail = LANES - MLA_NOPE - MLA_ROPE
    cos_t = jnp.concatenate([jnp.ones((seq_len, MLA_NOPE), F32), cos, cos,
                             jnp.ones((seq_len, tail), F32)], axis=-1)
    sin_t = jnp.concatenate([jnp.zeros((seq_len, MLA_NOPE), F32), -sin, sin,
                             jnp.zeros((seq_len, tail), F32)], axis=-1)
    return cos_t, sin_t


def _position_features(seq_len, pos_dim):
    t = jnp.linspace(0.0, 1.0, seq_len, dtype=F32)
    bands = (pos_dim - 1) // 2
    w = 2.0 * math.pi * jnp.arange(seq_len, dtype=F32) / seq_len
    f = jnp.linspace(1e-4, bands - 1, bands, dtype=F32)
    phase = w[:, None] * f[None, :]
    z = jnp.concatenate([t[:, None], jnp.cos(phase), -jnp.sin(phase)], axis=-1)
    return t, z


def kernel(x, norm_g, ffn_w_gate, ffn_w_up, ffn_w_down, mix_w_in, pool_w, pool_scale, hyena_conv_w, hyena_conv_b, hyena_ffn_w1, hyena_ffn_b1, hyena_ffn_w2, hyena_ffn_b2, hyena_ffn_w3, hyena_ffn_b3, hyena_sin_freq, hyena_ffn_w_out, hyena_decay, hyena_bias, mix_w_out, mla_w_dq, mla_q_norm_g, mla_w_uq, mla_w_dkv, mla_kv_norm_g, mla_w_ukv, mla_w_o, final_norm_g):
    batch, seq_len, d = x.shape
    depth = norm_g.shape[0]
    assert 2 * seq_len == DFT_N1 * DFT_N2
    hy_width = hyena_bias.shape[-1]
    t_pos, z_pos = _position_features(seq_len, hyena_ffn_w1.shape[1])
    cos_t, sin_t = _rope_tables(seq_len)

    x = x.reshape(batch * seq_len, d)
    ffn_w = (ffn_w_gate, ffn_w_up, ffn_w_down)
    for i in range(depth):
        j = i // 2
        if i % 2 == 0:
            x, proj = _ffn(x, norm_g[i, 0], ffn_w, (i, 0), next_g=norm_g[i, 1],
                           w_next=mix_w_in[j].astype(BF16))
            proj = proj.reshape(batch, seq_len, -1)
            y_pool = _pool_mixer(proj, pool_w[j].astype(BF16), pool_scale[j])
            taps, sumsq = _filter_taps(z_pos, t_pos, hyena_ffn_w1[j], hyena_ffn_b1[j],
                                       hyena_ffn_w2[j], hyena_ffn_b2[j], hyena_ffn_w3[j],
                                       hyena_ffn_b3[j], hyena_sin_freq[j], hyena_ffn_w_out[j],
                                       hyena_decay[j])
            spec = _filter_spectrum(taps, sumsq, hy_width)
            y_hyena = _hyena_mixer(proj, pool_scale.shape[-1], hyena_conv_w[j], hyena_conv_b[j],
                                   spec, hyena_bias[j])
            mix = (y_pool.reshape(batch * seq_len, -1), y_hyena.reshape(batch * seq_len, -1))
            w_mix = mix_w_out[j].astype(BF16)
        else:
            w_down, wq, wq_swap, wk, wv = _mla_weights(mla_w_dq[j], mla_w_uq[j], mla_w_dkv[j],
                                                       mla_w_ukv[j])
            x, c = _ffn(x, norm_g[i, 0], ffn_w, (i, 0), next_g=norm_g[i, 1], w_next=w_down)
            q, k, v = _mla_qkv(c, mla_q_norm_g[j], mla_kv_norm_g[j], wq, wq_swap, wk, wv,
                               cos_t, sin_t, batch)
            mix = (_attention(q, k, v).reshape(batch * seq_len, -1),)
            w_mix = mla_w_o[j].astype(BF16)
        x = _ffn(x, norm_g[i, 2], ffn_w, (i, 1), mix=mix, w_mix=w_mix,
                 final_g=final_norm_g if i == depth - 1 else None)
    return x.reshape(batch, seq_len, d)
```

```python
import functools
import math

import numpy as np
import jax
import jax.numpy as jnp
from jax import lax
from jax.experimental import pallas as pl
from jax.experimental.pallas import tpu as pltpu

F32 = jnp.float32
BF16 = jnp.bfloat16

RMS_EPS = 1e-6
MACARON_WEIGHT = 0.5
POOL_WINDOWS = (2, 4, 8, 16)
HYENA_ORDER = 2
HYENA_SHORT_CONV = 3
MLA_HEADS = 16
MLA_Q_LORA = 256
MLA_KV_LORA = 128
MLA_NOPE = 64
MLA_ROPE = 32
MLA_V = 64
ROPE_THETA = 10000.0
SOFTMAX_EXP2_SCALE = (MLA_NOPE + MLA_ROPE) ** -0.5 * math.log2(math.e)

LANES = 128
SUBLANES = 8
VMEM_LIMIT = 56 * 2**20

DFT_N1 = 64
DFT_N2 = 128
DFT_K1 = DFT_N1 // 2 + 1
DFT_QP = 40
DFT_PITCH = DFT_N2 + SUBLANES
PLANE_ROWS_PER_DOT = 8
PLANE_UNROLL = 16
K1_UNROLL = 33
HALO = 16
FFN_LOAD_STEPS = 8
SCORE_LEAD = 3
SCORE_CARRY = 1


def _cparams(semantics):
    return pltpu.CompilerParams(dimension_semantics=semantics, vmem_limit_bytes=VMEM_LIMIT)


def _rms(x, g):
    return x * lax.rsqrt(jnp.mean(x * x, axis=-1, keepdims=True) + RMS_EPS) * g


def _dot(a, b):
    return jnp.dot(a, b, preferred_element_type=F32)


def _ffn_body(n_mix, has_next, has_final, *refs):
    refs = list(refs)
    x_ref, g_ref, wg_ref, wu_ref, wd_ref = refs[:5]
    del refs[:5]
    mix_refs = [refs.pop(0) for _ in range(n_mix)]
    w_mix_ref = refs.pop(0) if n_mix else None
    next_g_ref, next_w_ref = (refs.pop(0), refs.pop(0)) if has_next else (None, None)
    final_g_ref = refs.pop(0) if has_final else None
    o_ref = refs.pop(0)
    next_o_ref = refs.pop(0) if has_next else None
    wg_bf, wu_bf, wd_bf = refs
    step = pl.program_id(0)

    @pl.when(step < FFN_LOAD_STEPS)
    def _():
        for src_ref, dst_ref in ((wg_ref, wg_bf), (wu_ref, wu_bf), (wd_ref, wd_bf)):
            rows = src_ref.shape[0]
            dst_ref[pl.ds(pl.multiple_of(step * rows, rows), rows), :] = src_ref[...].astype(BF16)

    @pl.when(step >= FFN_LOAD_STEPS)
    def _():
        x = x_ref[...]
        row = 0
        for a_ref in mix_refs:
            k = a_ref.shape[1]
            x = x + _dot(a_ref[...].astype(BF16), w_mix_ref[row:row + k, :])
            row += k
        h = _rms(x, g_ref[...]).astype(BF16)
        gate = _dot(h, wg_bf[...])
        up = _dot(h, wu_bf[...])
        act = gate / (1.0 + jnp.exp(-gate)) * up
        y = x + MACARON_WEIGHT * _dot(act.astype(BF16), wd_bf[...])
        if has_final:
            y = _rms(y, final_g_ref[...])
        o_ref[...] = y
        if has_next:
            next_o_ref[...] = _dot(_rms(y, next_g_ref[...]).astype(BF16), next_w_ref[...])


def _resident(shape):
    return pl.BlockSpec(shape, lambda *_: (0,) * len(shape), pipeline_mode=pl.Buffered(1))


def _streamed_rows(stack, index):
    lead = len(index)
    rows, cols = stack.shape[lead:]
    return pl.BlockSpec(
        (None,) * lead + (rows // FFN_LOAD_STEPS, cols),
        lambda i: tuple(index) + (jnp.minimum(i, FFN_LOAD_STEPS - 1), 0))


def _ffn(x, g, w_stacks, index, *, mix=(), w_mix=None, next_g=None, w_next=None, final_g=None,
         tm=512):
    m, d = x.shape
    tile = lambda i: jnp.maximum(i - FFN_LOAD_STEPS, 0)
    row_block = lambda a: pl.BlockSpec((tm, a.shape[1]), lambda i: (tile(i), 0))
    args = [x, g.reshape(1, d), *w_stacks]
    in_specs = [row_block(x), _resident((1, d))] + [_streamed_rows(w, index) for w in w_stacks]
    for a in mix:
        args.append(a)
        in_specs.append(row_block(a))
    if mix:
        args.append(w_mix)
        in_specs.append(_resident(w_mix.shape))
    if w_next is not None:
        args += [next_g.reshape(1, d), w_next]
        in_specs += [_resident((1, d)), _resident(w_next.shape)]
    if final_g is not None:
        args.append(final_g.reshape(1, d))
        in_specs.append(_resident((1, d)))
    out_shape = [jax.ShapeDtypeStruct((m, d), F32)]
    out_specs = [pl.BlockSpec((tm, d), lambda i: (tile(i), 0))]
    if w_next is not None:
        out_shape.append(jax.ShapeDtypeStruct((m, w_next.shape[1]), F32))
        out_specs.append(pl.BlockSpec((tm, w_next.shape[1]), lambda i: (tile(i), 0)))
    lead = len(index)
    outs = pl.pallas_call(
        functools.partial(_ffn_body, len(mix), w_next is not None, final_g is not None),
        out_shape=out_shape,
        grid=(FFN_LOAD_STEPS + m // tm,),
        in_specs=in_specs,
        out_specs=out_specs,
        scratch_shapes=[pltpu.VMEM(w.shape[lead:], BF16) for w in w_stacks],
        compiler_params=_cparams(("arbitrary",)),
        name="ffn",
    )(*args)
    return outs if w_next is not None else outs[0]


def _pool_body(seq_len, cur_ref, prev_ref, next_ref, pw_ref, ps_ref, yp_ref, ext_ref):
    i = pl.program_id(1)
    tl = cur_ref.shape[0]
    pool_width = yp_ref.shape[1]
    group = pool_width // len(POOL_WINDOWS)
    ext_ref[0:HALO, :] = jnp.where(i > 0, prev_ref[...], 0.0)
    ext_ref[HALO:HALO + tl, :] = cur_ref[...]
    ext_ref[HALO + tl:, :] = jnp.where(i < pl.num_programs(1) - 1, next_ref[...], 0.0)

    t = i * tl + lax.broadcasted_iota(jnp.int32, (tl, group), 0)
    for g, w in enumerate(POOL_WINDOWS):
        c0 = g * group
        lo = jnp.clip(t - w // 2, 0, seq_len)
        hi = jnp.clip(t - w // 2 + w, 0, seq_len)
        cnt = (hi - lo).astype(F32)
        s = ext_ref[:, c0:c0 + group]
        n_rows = s.shape[0]
        span = 1
        while span < w:
            s = s + pltpu.roll(s, n_rows - span, axis=0)
            span *= 2
        s = pltpu.roll(s, w // 2, axis=0)[HALO:HALO + tl]
        p = s / cnt - cur_ref[:, c0:c0 + group]
        y = _dot(p.astype(BF16), pw_ref[g])
        yp_ref[:, c0:c0 + group] = y * ps_ref[:, c0:c0 + group]


def _pool_mixer(proj, pool_w, pool_scale, *, tl=2048):
    b, seq_len, _ = proj.shape
    pool_width = pool_scale.shape[0]
    assert all(w & (w - 1) == 0 and w <= HALO for w in POOL_WINDOWS)
    hb = tl // HALO
    last = seq_len // HALO - 1
    return pl.pallas_call(
        functools.partial(_pool_body, seq_len),
        out_shape=jax.ShapeDtypeStruct((b, seq_len, pool_width), F32),
        grid=(b, seq_len // tl),
        in_specs=[
            pl.BlockSpec((None, tl, pool_width), lambda bi, i: (bi, i, 0)),
            pl.BlockSpec((None, HALO, pool_width),
                         lambda bi, i: (bi, jnp.maximum(i * hb - 1, 0), 0)),
            pl.BlockSpec((None, HALO, pool_width),
                         lambda bi, i: (bi, jnp.minimum((i + 1) * hb, last), 0)),
            pl.BlockSpec(pool_w.shape, lambda bi, i: (0, 0, 0)),
            pl.BlockSpec((1, pool_width), lambda bi, i: (0, 0)),
        ],
        out_specs=pl.BlockSpec((None, tl, pool_width), lambda bi, i: (bi, i, 0)),
        scratch_shapes=[pltpu.VMEM((tl + 2 * HALO, pool_width), F32)],
        compiler_params=_cparams(("parallel", "parallel")),
        name="pool_mixer",
    )(proj, proj, proj, pool_w, pool_scale.reshape(1, pool_width))


@functools.lru_cache(maxsize=None)
def _dft_constants():
    n = DFT_N1 * DFT_N2
    half = DFT_N1 // 2
    k1 = np.arange(DFT_K1)
    n1 = np.arange(half)
    ang = 2.0 * np.pi * np.outer(k1, n1) / DFT_N1
    f_fwd = np.zeros((2 * DFT_QP, half))
    f_fwd[:DFT_K1] = np.cos(ang)
    f_fwd[DFT_QP:DFT_QP + DFT_K1] = -np.sin(ang)
    n2 = np.arange(DFT_N2)
    k2 = np.arange(DFT_N2)
    g = np.zeros((DFT_K1, 2 * DFT_N2, 2 * DFT_N2))
    for a in range(DFT_K1):
        ph = 2.0 * np.pi * np.outer(a + DFT_N1 * k2, n2) / n
        gre, gim = np.cos(ph), -np.sin(ph)
        g[a] = np.block([[gre, -gim], [gim, gre]])
    weight = np.full(DFT_K1, 2.0)
    weight[0] = 1.0
    weight[-1] = 1.0
    f_inv = np.zeros((half, 2 * DFT_QP))
    f_inv[:, :DFT_K1] = weight * np.cos(ang.T) / n
    f_inv[:, DFT_QP:DFT_QP + DFT_K1] = -weight * np.sin(ang.T) / n
    return (f_fwd.astype(np.float32), g.astype(np.float32),
            np.transpose(g, (0, 2, 1)).astype(np.float32), f_inv.astype(np.float32))


def _pad_rows(dst_ref, src):
    for n1 in range(DFT_N1 // 2):
        dst_ref[n1 * DFT_PITCH:n1 * DFT_PITCH + DFT_N2, :] = src[n1 * DFT_N2:(n1 + 1) * DFT_N2, :]


def _contract_planes(src_ref, dst_ref, f_ref):
    n_dst, n_src = f_ref.shape

    def step(j, carry):
        r0 = j * PLANE_ROWS_PER_DOT
        x = jnp.concatenate(
            [src_ref[pl.ds(r0 + r, n_src, stride=DFT_PITCH), :].astype(BF16)
             for r in range(PLANE_ROWS_PER_DOT)], axis=1)
        y = _dot(f_ref[...], x)
        for r in range(PLANE_ROWS_PER_DOT):
            dst_ref[pl.ds(r0 + r, n_dst, stride=DFT_PITCH), :] = y[:, r * LANES:(r + 1) * LANES]
        return carry

    lax.fori_loop(0, DFT_N2 // PLANE_ROWS_PER_DOT, step, 0, unroll=PLANE_UNROLL)


def _load_planes(a_ref, k1):
    re = a_ref[pl.ds(pl.multiple_of(k1 * DFT_PITCH, SUBLANES), DFT_N2), :]
    im = a_ref[pl.ds(pl.multiple_of((DFT_QP + k1) * DFT_PITCH, SUBLANES), DFT_N2), :]
    return jnp.concatenate([re, im], axis=0)


def _store_planes(a_ref, k1, z):
    a_ref[pl.ds(pl.multiple_of(k1 * DFT_PITCH, SUBLANES), DFT_N2), :] = z[:DFT_N2]
    a_ref[pl.ds(pl.multiple_of((DFT_QP + k1) * DFT_PITCH, SUBLANES), DFT_N2), :] = z[DFT_N2:]


def _spectrum_body(fw_ref, bw_ref, ssf_ref, ssb_ref, ff_ref, gt_ref, h_ref, u_ref, a_ref):
    norm = lax.rsqrt(ssf_ref[...] + ssb_ref[...]).T
    for part, src_ref in enumerate((fw_ref, bw_ref)):
        _pad_rows(u_ref, src_ref)
        _contract_planes(u_ref, a_ref, ff_ref)
        sign = 1.0 if part == 0 else -1.0

        def step(k1, carry):
            xt = _dot(_load_planes(a_ref, k1).T.astype(BF16), gt_ref[k1])
            re = xt[:, :DFT_N2] * norm
            im = xt[:, DFT_N2:] * (sign * norm)
            if part == 0:
                h_ref[k1, :, :DFT_N2] = re
                h_ref[k1, :, DFT_N2:] = im
            else:
                h_ref[k1, :, :DFT_N2] += re
                h_ref[k1, :, DFT_N2:] += im
            return carry

        lax.fori_loop(0, DFT_K1, step, 0, unroll=K1_UNROLL)


def _filter_spectrum(taps, sumsq, width):
    seq_len = taps.shape[0]
    ff, _, gt, _ = _dft_constants()
    cb = width // LANES
    half = DFT_N1 // 2
    return pl.pallas_call(
        _spectrum_body,
        out_shape=jax.ShapeDtypeStruct((HYENA_ORDER, DFT_K1, width, 2 * DFT_N2), F32),
        grid=(HYENA_ORDER, cb),
        in_specs=[
            pl.BlockSpec((seq_len, LANES), lambda o, c: (0, (2 * o) * cb + c)),
            pl.BlockSpec((seq_len, LANES), lambda o, c: (0, (2 * o + 1) * cb + c)),
            pl.BlockSpec((1, LANES), lambda o, c: (0, (2 * o) * cb + c)),
            pl.BlockSpec((1, LANES), lambda o, c: (0, (2 * o + 1) * cb + c)),
            pl.BlockSpec((2 * DFT_QP, half), lambda o, c: (0, 0)),
            pl.BlockSpec((DFT_K1, 2 * DFT_N2, 2 * DFT_N2), lambda o, c: (0, 0, 0)),
        ],
        out_specs=pl.BlockSpec((None, DFT_K1, LANES, 2 * DFT_N2), lambda o, c: (o, 0, c, 0)),
        scratch_shapes=[pltpu.VMEM((half * DFT_PITCH, LANES), F32),
                        pltpu.VMEM((2 * DFT_QP * DFT_PITCH, LANES), F32)],
        compiler_params=_cparams(("parallel", "parallel")),
        name="filter_spectrum",
    )(taps, taps, sumsq, sumsq, jnp.asarray(ff).astype(BF16), jnp.asarray(gt).astype(BF16))


def _short_conv_rows(x_ref, n1, w_ref, b_ref):
    r0 = n1 * DFT_N2
    cur = x_ref[r0:r0 + DFT_N2, :]
    row = lax.broadcasted_iota(jnp.int32, cur.shape, 0)
    if r0 == 0:
        prev = jnp.where(row == 0, 0.0, pltpu.roll(cur, 1, axis=0))
    else:
        prev = x_ref[r0 - 1:r0 + DFT_N2 - 1, :]
    if r0 + DFT_N2 == x_ref.shape[0]:
        nxt = jnp.where(row == DFT_N2 - 1, 0.0, pltpu.roll(cur, DFT_N2 - 1, axis=0))
    else:
        nxt = x_ref[r0 + 1:r0 + DFT_N2 + 1, :]
    return b_ref[...] + prev * w_ref[0:1, :] + cur * w_ref[1:2, :] + nxt * w_ref[2:3, :]


def _hyena_body(v_ref, gate_ref, vw_ref, vb_ref, gw_ref, gb_ref, h_ref, bias_ref, ff_ref, g_ref,
                gt_ref, fi_ref, o_ref, u_ref, y_ref, a_ref):
    order = pl.program_id(2)
    half = DFT_N1 // 2

    @pl.when(order == 0)
    def _():
        for n1 in range(half):
            u_ref[n1 * DFT_PITCH:n1 * DFT_PITCH + DFT_N2, :] = _short_conv_rows(
                v_ref, n1, vw_ref, vb_ref)

    _contract_planes(u_ref, a_ref, ff_ref)

    def freq_step(k1, carry):
        xt = _dot(_load_planes(a_ref, k1).T.astype(BF16), gt_ref[k1])
        xre, xim = xt[:, :DFT_N2], xt[:, DFT_N2:]
        hre, him = h_ref[k1, :, :DFT_N2], h_ref[k1, :, DFT_N2:]
        yt = jnp.concatenate([xre * hre - xim * him, xre * him + xim * hre], axis=1)
        _store_planes(a_ref, k1, _dot(yt.astype(BF16), g_ref[k1]).T)
        return carry

    lax.fori_loop(0, DFT_K1, freq_step, 0, unroll=K1_UNROLL)
    _contract_planes(a_ref, y_ref, fi_ref)

    bias = bias_ref[...]
    for n1 in range(half):
        rows = slice(n1 * DFT_PITCH, n1 * DFT_PITCH + DFT_N2)
        gate = _short_conv_rows(gate_ref, n1, gw_ref, gb_ref)
        z = gate * (y_ref[rows, :] + u_ref[rows, :] * bias)
        u_ref[rows, :] = z
        o_ref[n1 * DFT_N2:(n1 + 1) * DFT_N2, :] = z


def _hyena_mixer(proj, first_col, conv_w, conv_b, spec, bias):
    assert HYENA_SHORT_CONV == 3
    b, seq_len, _ = proj.shape
    width = bias.shape[1]
    cb = width // LANES
    c0 = first_col // LANES
    half = DFT_N1 // 2
    ff, g, gt, fi = _dft_constants()
    value_col = lambda c, bi, o: HYENA_ORDER * cb + c
    gate_col = lambda c, bi, o: o * cb + c
    conv_b = conv_b.reshape(1, -1)
    return pl.pallas_call(
        _hyena_body,
        out_shape=jax.ShapeDtypeStruct((b, seq_len, width), F32),
        grid=(cb, b, HYENA_ORDER),
        in_specs=[
            pl.BlockSpec((None, seq_len, LANES), lambda c, bi, o: (bi, 0, c0 + value_col(c, bi, o))),
            pl.BlockSpec((None, seq_len, LANES), lambda c, bi, o: (bi, 0, c0 + gate_col(c, bi, o))),
            pl.BlockSpec((HYENA_SHORT_CONV, LANES), lambda c, bi, o: (0, value_col(c, bi, o))),
            pl.BlockSpec((1, LANES), lambda c, bi, o: (0, value_col(c, bi, o))),
            pl.BlockSpec((HYENA_SHORT_CONV, LANES), lambda c, bi, o: (0, gate_col(c, bi, o))),
            pl.BlockSpec((1, LANES), lambda c, bi, o: (0, gate_col(c, bi, o))),
            pl.BlockSpec((None, DFT_K1, LANES, 2 * DFT_N2), lambda c, bi, o: (o, 0, c, 0)),
            pl.BlockSpec((None, 1, LANES), lambda c, bi, o: (o, 0, c)),
            _resident((2 * DFT_QP, half)),
            _resident((DFT_K1, 2 * DFT_N2, 2 * DFT_N2)),
            _resident((DFT_K1, 2 * DFT_N2, 2 * DFT_N2)),
            _resident((half, 2 * DFT_QP)),
        ],
        out_specs=pl.BlockSpec((None, seq_len, LANES), lambda c, bi, o: (bi, 0, c)),
        scratch_shapes=[pltpu.VMEM((half * DFT_PITCH, LANES), F32),
                        pltpu.VMEM((half * DFT_PITCH, LANES), F32),
                        pltpu.VMEM((2 * DFT_QP * DFT_PITCH, LANES), F32)],
        compiler_params=_cparams(("parallel", "parallel", "arbitrary")),
        name="hyena_mixer",
    )(proj, proj, conv_w, conv_b, conv_w, conv_b, spec, bias.reshape(HYENA_ORDER, 1, width),
      jnp.asarray(ff).astype(BF16), jnp.asarray(g).astype(BF16), jnp.asarray(gt).astype(BF16),
      jnp.asarray(fi).astype(BF16))


def _hdot(a, b):
    return jnp.dot(a, b, preferred_element_type=F32, precision=lax.Precision.HIGHEST)


def _sin_half_lanes(arg):
    rows = arg.shape[0] // 2
    lane = lax.broadcasted_iota(jnp.int32, (rows, LANES), 1)
    packed = jnp.where(lane < LANES // 2, arg[:rows], pltpu.roll(arg[rows:], LANES // 2, axis=1))
    s = jnp.sin(packed)
    return jnp.concatenate([s, pltpu.roll(s, LANES // 2, axis=1)], axis=0)


def _filter_taps_body(pack_sin, bwd_cols, z_ref, t_ref, w1_ref, b1_ref, w2_ref, b2_ref, w3_ref,
                      b3_ref, sf_ref, wo_ref, decay_ref, taps_ref, ss_ref):
    i = pl.program_id(0)
    sin = _sin_half_lanes if pack_sin else jnp.sin
    h = sin(sf_ref[0:1, :] * (_hdot(z_ref[...], w1_ref[...]) + b1_ref[...]))
    h = sin(sf_ref[1:2, :] * (_hdot(h, w2_ref[...]) + b2_ref[...]))
    h = sin(sf_ref[2:3, :] * (_hdot(h, w3_ref[...]) + b3_ref[...]))
    h = _dot(h.astype(BF16), wo_ref[...])
    h = h * jnp.exp(-t_ref[...] * jnp.abs(decay_ref[...]))
    row = i * h.shape[0] + lax.broadcasted_iota(jnp.int32, h.shape, 0)
    h = jnp.where((row == 0) & (bwd_cols[...] > 0.0), 0.0, h)
    taps_ref[...] = h

    @pl.when(i == 0)
    def _():
        ss_ref[...] = jnp.zeros_like(ss_ref)

    ss_ref[...] += jnp.sum(h * h, axis=0, keepdims=True)


def _filter_taps(z, t, w1, b1, w2, b2, w3, b3, sin_freq, w_out, decay, *, tl=1024):
    seq_len = z.shape[0]
    hid = LANES
    n_out = w_out.shape[1]
    width = n_out // (2 * HYENA_ORDER)

    def pad2(a, rows, cols):
        return jnp.pad(a, ((0, rows - a.shape[0]), (0, cols - a.shape[1])))

    is_bwd = np.tile(np.repeat(np.array([0.0, 1.0], np.float32), width), HYENA_ORDER)[None, :]
    full = lambda a: pl.BlockSpec(a.shape, lambda i: (0, 0))
    args = (
        pad2(z, seq_len, hid), t.reshape(seq_len, 1),
        pad2(w1, hid, hid), pad2(b1[None, :], 1, hid),
        pad2(w2, hid, hid), pad2(b2[None, :], 1, hid),
        pad2(w3, hid, hid), pad2(b3[None, :], 1, hid),
        pad2(sin_freq, 3, hid), pad2(w_out, hid, n_out).astype(BF16), decay.reshape(1, n_out),
    )
    bwd_cols = jnp.asarray(is_bwd)
    return pl.pallas_call(
        functools.partial(_filter_taps_body, w1.shape[1] <= LANES // 2 and tl % 16 == 0),
        out_shape=(jax.ShapeDtypeStruct((seq_len, n_out), F32),
                   jax.ShapeDtypeStruct((1, n_out), F32)),
        grid=(seq_len // tl,),
        in_specs=[full(bwd_cols),
                  pl.BlockSpec((tl, hid), lambda i: (i, 0)),
                  pl.BlockSpec((tl, 1), lambda i: (i, 0))] + [full(a) for a in args[2:]],
        out_specs=(pl.BlockSpec((tl, n_out), lambda i: (i, 0)),
                   pl.BlockSpec((1, n_out), lambda i: (0, 0))),
        compiler_params=_cparams(("arbitrary",)),
        name="filter_taps",
    )(bwd_cols, *args)


def _rope_slab(x, cos, sin):
    half = MLA_ROPE // 2
    lane = lax.broadcasted_iota(jnp.int32, x.shape, 1)
    swapped = jnp.where(lane < MLA_NOPE + half, pltpu.roll(x, LANES - half, axis=1),
                        pltpu.roll(x, half, axis=1))
    return x * cos + swapped * sin


def _mla_qkv_body(c_ref, qg_ref, kvg_ref, wq_ref, wqs_ref, wk_ref, wv_ref, cos_ref, sin_ref,
                  q_ref, k_ref, v_ref):
    c = c_ref[...]
    cos = cos_ref[...]
    sin = sin_ref[...]
    cq = _rms(c[:, :MLA_Q_LORA], qg_ref[...]).astype(BF16)
    ckv = _rms(c[:, MLA_Q_LORA:MLA_Q_LORA + MLA_KV_LORA], kvg_ref[...]).astype(BF16)
    kr0 = MLA_Q_LORA + MLA_KV_LORA
    k_rope = _rope_slab(c[:, kr0:kr0 + LANES], cos, sin)
    cos_q = cos * SOFTMAX_EXP2_SCALE
    sin_q = sin * SOFTMAX_EXP2_SCALE
    ones_lane = (lax.broadcasted_iota(jnp.int32, (1, LANES), 1) == MLA_V).astype(F32)
    for h0 in range(0, MLA_HEADS, 2):
        cols = slice(h0 * LANES, (h0 + 2) * LANES)
        q2 = _dot(cq, wq_ref[:, cols])
        q2_swap = _dot(cq, wqs_ref[:, cols])
        k2 = _dot(ckv, wk_ref[:, cols])
        v2 = _dot(ckv, wv_ref[:, cols])
        for j in range(2):
            lanes = slice(j * LANES, (j + 1) * LANES)
            q_ref[h0 + j] = (q2[:, lanes] * cos_q + q2_swap[:, lanes] * sin_q).astype(BF16)
            k_ref[h0 + j] = (k2[:, lanes] + k_rope).astype(BF16)
            v_ref[h0 + j] = (v2[:, lanes] + ones_lane).astype(BF16)


def _mla_qkv(c, q_norm_g, kv_norm_g, wq, wq_swap, wk, wv, cos_t, sin_t, batch, *, tl=1024):
    m, cw = c.shape
    seq_len = m // batch
    nl = seq_len // tl
    full = lambda a: pl.BlockSpec(a.shape, lambda bi, i: (0, 0))
    qg = q_norm_g.reshape(1, -1)
    kvg = kv_norm_g.reshape(1, -1)
    head_out = jax.ShapeDtypeStruct((batch, MLA_HEADS, seq_len, LANES), BF16)
    head_spec = pl.BlockSpec((None, MLA_HEADS, tl, LANES), lambda bi, i: (bi, 0, i, 0))
    return pl.pallas_call(
        _mla_qkv_body,
        out_shape=(head_out, head_out, head_out),
        grid=(batch, nl),
        in_specs=[pl.BlockSpec((tl, cw), lambda bi, i: (bi * nl + i, 0)),
                  full(qg), full(kvg), full(wq), full(wq_swap), full(wk), full(wv),
                  pl.BlockSpec((tl, LANES), lambda bi, i: (i, 0)),
                  pl.BlockSpec((tl, LANES), lambda bi, i: (i, 0))],
        out_specs=(head_spec, head_spec, head_spec),
        compiler_params=_cparams(("parallel", "parallel")),
        name="mla_qkv",
    )(c, qg, kvg, wq, wq_swap, wk, wv, cos_t, sin_t)


def _attention_body(q_ref, qn_ref, k_ref, v_ref, o_ref, *s_refs):
    pair = LANES // MLA_V
    n_heads = q_ref.shape[0]
    n_carry = len(s_refs)

    def scores(q, h):
        return lax.dot_general(q, k_ref[h], (((1,), (1,)), ((), ())), preferred_element_type=F32)

    @pl.when(pl.program_id(2) == 0)
    def _():
        for h, s_ref in enumerate(s_refs):
            s_ref[...] = scores(q_ref[h], h)

    ahead = {h: s_ref[...] for h, s_ref in enumerate(s_refs)}
    for h in range(n_carry, SCORE_LEAD):
        ahead[h] = scores(q_ref[h], h)
    for g in range(n_heads // pair):
        outs = []
        for h in range(g * pair, (g + 1) * pair):
            lead = h + SCORE_LEAD
            if lead < n_heads:
                ahead[lead] = scores(q_ref[lead], lead)
            elif lead - n_heads < n_carry:
                s_refs[lead - n_heads][...] = scores(qn_ref[lead - n_heads], lead - n_heads)
            s = ahead.pop(h)
            p = jnp.exp2(s - jnp.max(s, axis=-1, keepdims=True))
            r = _dot(p.astype(BF16), v_ref[h])
            outs.append(r / r[:, MLA_V:MLA_V + 1])
        lane = lax.broadcasted_iota(jnp.int32, outs[0].shape, 1)
        both = jnp.where(lane < MLA_V, outs[0], pltpu.roll(outs[1], MLA_V, axis=1))
        o_ref[:, g * LANES:(g + 1) * LANES] = both.astype(o_ref.dtype)


def _attention(q, k, v, *, tq=256, group=8):
    batch, heads, seq_len, _ = q.shape
    width = group * MLA_V
    n_tiles = seq_len // tq
    return pl.pallas_call(
        _attention_body,
        out_shape=jax.ShapeDtypeStruct((batch, seq_len, heads * MLA_V), BF16),
        grid=(batch, heads // group, n_tiles),
        in_specs=[
            pl.BlockSpec((None, group, tq, LANES), lambda bi, p, i: (bi, p, i, 0)),
            pl.BlockSpec((None, SCORE_CARRY, tq, LANES),
                         lambda bi, p, i: (bi, p * (group // SCORE_CARRY),
                                           jnp.minimum(i + 1, n_tiles - 1), 0)),
            pl.BlockSpec((None, group, seq_len, LANES), lambda bi, p, i: (bi, p, 0, 0)),
            pl.BlockSpec((None, group, seq_len, LANES), lambda bi, p, i: (bi, p, 0, 0)),
        ],
        out_specs=pl.BlockSpec((None, tq, width), lambda bi, p, i: (bi, i, p)),
        scratch_shapes=[pltpu.VMEM((tq, seq_len), F32)] * SCORE_CARRY,
        compiler_params=_cparams(("parallel", "parallel", "arbitrary")),
        name="attention",
    )(q, q, k, v)


def _mla_weights(w_dq, w_uq, w_dkv, w_ukv):
    d = w_dq.shape[0]
    qh = w_uq.reshape(MLA_Q_LORA, MLA_HEADS, MLA_NOPE + MLA_ROPE)
    zq = jnp.zeros((MLA_Q_LORA, MLA_HEADS, LANES - MLA_NOPE - MLA_ROPE), F32)
    wq = jnp.concatenate([qh, zq], axis=-1).reshape(MLA_Q_LORA, MLA_HEADS * LANES)
    half = MLA_ROPE // 2
    wq_swap = jnp.concatenate([jnp.zeros((MLA_Q_LORA, MLA_HEADS, MLA_NOPE), F32),
                               qh[..., MLA_NOPE + half:], qh[..., MLA_NOPE:MLA_NOPE + half], zq],
                              axis=-1).reshape(MLA_Q_LORA, MLA_HEADS * LANES)
    kvh = w_ukv.reshape(MLA_KV_LORA, MLA_HEADS, MLA_NOPE + MLA_V)
    wk = jnp.concatenate([kvh[..., :MLA_NOPE],
                          jnp.zeros((MLA_KV_LORA, MLA_HEADS, LANES - MLA_NOPE), F32)],
                         axis=-1).reshape(MLA_KV_LORA, MLA_HEADS * LANES)
    wv = jnp.concatenate([kvh[..., MLA_NOPE:],
                          jnp.zeros((MLA_KV_LORA, MLA_HEADS, LANES - MLA_V), F32)],
                         axis=-1).reshape(MLA_KV_LORA, MLA_HEADS * LANES)
    w_kr = jnp.concatenate([jnp.zeros((d, MLA_NOPE), F32), w_dkv[:, MLA_KV_LORA:],
                            jnp.zeros((d, LANES - MLA_NOPE - MLA_ROPE), F32)], axis=-1)
    w_down = jnp.concatenate([w_dq, w_dkv[:, :MLA_KV_LORA], w_kr], axis=-1)
    return (w_down.astype(BF16), wq.astype(BF16), wq_swap.astype(BF16), wk.astype(BF16),
            wv.astype(BF16))


def _rope_tables(seq_len):
    inv_freq = ROPE_THETA ** (-jnp.arange(0, MLA_ROPE, 2, dtype=F32) / MLA_ROPE)
    ang = jnp.arange(seq_len, dtype=F32)[:, None] * inv_freq[None, :]
    cos = jnp.cos(ang)
    sin = jnp.sin(ang)
    tail = LANES - MLA_NOPE - MLA_ROPE
    cos_t = jnp.concatenate([jnp.ones((seq_len, MLA_NOPE), F32), cos, cos,
                             jnp.ones((seq_len, tail), F32)], axis=-1)
    sin_t = jnp.concatenate([jnp.zeros((seq_len, MLA_NOPE), F32), -sin, sin,
                             jnp.zeros((seq_len, tail), F32)], axis=-1)
    return cos_t, sin_t


def _position_features(seq_len, pos_dim):
    t = jnp.linspace(0.0, 1.0, seq_len, dtype=F32)
    bands = (pos_dim - 1) // 2
    w = 2.0 * math.pi * jnp.arange(seq_len, dtype=F32) / seq_len
    f = jnp.linspace(1e-4, bands - 1, bands, dtype=F32)
    phase = w[:, None] * f[None, :]
    z = jnp.concatenate([t[:, None], jnp.cos(phase), -jnp.sin(phase)], axis=-1)
    return t, z


def kernel(x, norm_g, ffn_w_gate, ffn_w_up, ffn_w_down, mix_w_in, pool_w, pool_scale, hyena_conv_w, hyena_conv_b, hyena_ffn_w1, hyena_ffn_b1, hyena_ffn_w2, hyena_ffn_b2, hyena_ffn_w3, hyena_ffn_b3, hyena_sin_freq, hyena_ffn_w_out, hyena_decay, hyena_bias, mix_w_out, mla_w_dq, mla_q_norm_g, mla_w_uq, mla_w_dkv, mla_kv_norm_g, mla_w_ukv, mla_w_o, final_norm_g):
    batch, seq_len, d = x.shape
    depth = norm_g.shape[0]
    assert 2 * seq_len == DFT_N1 * DFT_N2
    hy_width = hyena_bias.shape[-1]
    t_pos, z_pos = _position_features(seq_len, hyena_ffn_w1.shape[1])
    cos_t, sin_t = _rope_tables(seq_len)

    x = x.reshape(batch * seq_len, d)
    ffn_w = (ffn_w_gate, ffn_w_up, ffn_w_down)
    for i in range(depth):
        j = i // 2
        if i % 2 == 0:
            x, proj = _ffn(x, norm_g[i, 0], ffn_w, (i, 0), next_g=norm_g[i, 1],
                           w_next=mix_w_in[j].astype(BF16))
            proj = proj.reshape(batch, seq_len, -1)
            y_pool = _pool_mixer(proj, pool_w[j].astype(BF16), pool_scale[j])
            taps, sumsq = _filter_taps(z_pos, t_pos, hyena_ffn_w1[j], hyena_ffn_b1[j],
                                       hyena_ffn_w2[j], hyena_ffn_b2[j], hyena_ffn_w3[j],
                                       hyena_ffn_b3[j], hyena_sin_freq[j], hyena_ffn_w_out[j],
                                       hyena_decay[j])
            spec = _filter_spectrum(taps, sumsq, hy_width)
            y_hyena = _hyena_mixer(proj, pool_scale.shape[-1], hyena_conv_w[j], hyena_conv_b[j],
                                   spec, hyena_bias[j])
            mix = (y_pool.reshape(batch * seq_len, -1), y_hyena.reshape(batch * seq_len, -1))
            w_mix = mix_w_out[j].astype(BF16)
        else:
            w_down, wq, wq_swap, wk, wv = _mla_weights(mla_w_dq[j], mla_w_uq[j], mla_w_dkv[j],
                                                       mla_w_ukv[j])
            x, c = _ffn(x, norm_g[i, 0], ffn_w, (i, 0), next_g=norm_g[i, 1], w_next=w_down)
            q, k, v = _mla_qkv(c, mla_q_norm_g[j], mla_kv_norm_g[j], wq, wq_swap, wk, wv,
                               cos_t, sin_t, batch)
            mix = (_attention(q, k, v).reshape(batch * seq_len, -1),)
            w_mix = mla_w_o[j].astype(BF16)
        x = _ffn(x, norm_g[i, 2], ffn_w, (i, 1), mix=mix, w_mix=w_mix,
                 final_g=final_norm_g if i == depth - 1 else None)
    return x.reshape(batch, seq_len, d)
```
